```python
import jax, jax.numpy as jnp
from jax import lax
import numpy as np

D_MODEL = 2048
BATCH = 4
SEQ = 2048
DEPTH = 1
DEC_BATCH = 32
DEC_SEQ = 1
PAST_LEN = 16384
PAGE_SIZE = 128

N_MLSTM_HEADS = 4
MLSTM_DK = D_MODEL // 8
MLSTM_DV = D_MODEL // 4
MLSTM_QK = N_MLSTM_HEADS * MLSTM_DK
MLSTM_V = N_MLSTM_HEADS * MLSTM_DV
MLSTM_CHUNK = 64
HEAD_DIM = 64
N_Q_HEADS = D_MODEL // HEAD_DIM
N_KV_HEADS = 4
GROUP = N_Q_HEADS // N_KV_HEADS
ATT_Q = N_Q_HEADS * HEAD_DIM
ATT_KV = N_KV_HEADS * HEAD_DIM
WINDOW = 128
ROT_DIM = HEAD_DIM // 4
ROPE_THETA = 500000.0
D_FF = 11 * D_MODEL // 4
CONV_WIDTH = 3
EPS = 1e-6
D_IN = 2 * MLSTM_QK + 2 * MLSTM_V + 2 * N_MLSTM_HEADS + ATT_Q + 2 * ATT_KV + 2 * D_MODEL

kernel_name = "hybrid_mlstm_swa_sink_convffn_step"


def _rmsnorm(x, g):
    x32 = x.astype(jnp.float32)
    y = x32 * lax.rsqrt(jnp.mean(x32 * x32, axis=-1, keepdims=True) + EPS) * g.astype(jnp.float32)
    return y.astype(x.dtype)


def _rope(x, pos):
    half = ROT_DIM // 2
    inv = ROPE_THETA ** (-jnp.arange(half, dtype=jnp.float32) * 2.0 / ROT_DIM)
    ang = pos.astype(jnp.float32)[:, None] * inv[None, :]
    cos = jnp.cos(ang)[:, None, :]
    sin = jnp.sin(ang)[:, None, :]
    x32 = x.astype(jnp.float32)
    x1, x2, rest = x32[..., :half], x32[..., half:ROT_DIM], x32[..., ROT_DIM:]
    out = jnp.concatenate([x1 * cos - x2 * sin, x2 * cos + x1 * sin, rest], axis=-1)
    return out.astype(x.dtype)


def _mlstm_chunk(carry, xs):
    C, n, m = carry
    q, k, v, ig, lf = xs
    L = q.shape[2]
    b = jnp.cumsum(lf, axis=-1)
    causal = jnp.tril(jnp.ones((L, L), dtype=bool))
    D = jnp.where(causal, b[..., :, None] - b[..., None, :] + ig[..., None, :], -jnp.inf)
    inter = b + m[..., None]
    m_t = jnp.maximum(jnp.max(D, axis=-1), inter)
    S = jnp.einsum("bhlk,bhsk->bhls", q, k) * jnp.exp(D - m_t[..., None])
    w_inter = jnp.exp(inter - m_t)
    num = jnp.einsum("bhls,bhsv->bhlv", S, v) + w_inter[..., None] * jnp.einsum("bhvk,bhlk->bhlv", C, q)
    den = jnp.sum(S, axis=-1) + w_inter * jnp.einsum("bhk,bhlk->bhl", n, q)
    h = num / jnp.maximum(jnp.abs(den), jnp.exp(-m_t))[..., None]
    m_new = m_t[..., -1]
    w_last = jnp.exp(b[..., -1:] - b + ig - m_new[..., None])
    decay = jnp.exp(b[..., -1] + m - m_new)
    C_new = decay[..., None, None] * C + jnp.einsum("bhl,bhlv,bhlk->bhvk", w_last, v, k)
    n_new = decay[..., None] * n + jnp.einsum("bhl,bhlk->bhk", w_last, k)
    return (C_new, n_new, m_new), h


def _mlstm(q, k, v, ig, lf, C0, n0, m0, chunk):
    B, T = q.shape[:2]
    nc = T // chunk

    def to_chunks(a):
        a = a.reshape((B, nc, chunk) + a.shape[2:])
        a = jnp.moveaxis(a, 3, 2)
        return jnp.moveaxis(a, 1, 0)

    xs = (to_chunks(q), to_chunks(k), to_chunks(v), to_chunks(ig), to_chunks(lf))
    (C, n, m), h = lax.scan(_mlstm_chunk, (C0, n0, m0), xs)
    h = jnp.moveaxis(h, 0, 1)
    h = jnp.moveaxis(h, 2, 3).reshape(B, T, N_MLSTM_HEADS, MLSTM_DV)
    return h, C, n, m


def _sink_attention(q, k, v, mask, sinks):
    s = jnp.einsum("ntkgd,nskd->nkgts", q, k).astype(jnp.float32) * (HEAD_DIM ** -0.5)
    s = jnp.where(mask, s, -jnp.inf)
    sink = jnp.broadcast_to(sinks.astype(jnp.float32)[None, :, :, None, None], s.shape[:-1] + (1,))
    p = jax.nn.softmax(jnp.concatenate([s, sink], axis=-1), axis=-1)[..., :-1]
    return jnp.einsum("nkgts,nskd->ntkgd", p.astype(v.dtype), v)


def _swa_prompt(q, k, v, sinks):
    B, T = q.shape[:2]
    nb = T // WINDOW
    qb = q.reshape(B * nb, WINDOW, N_KV_HEADS, GROUP, HEAD_DIM)

    def band(a):
        ab = a.reshape(B, nb, WINDOW, N_KV_HEADS, HEAD_DIM)
        prev = jnp.concatenate([jnp.zeros_like(ab[:, :1]), ab[:, :-1]], axis=1)
        return jnp.concatenate([prev, ab], axis=2).reshape(B * nb, 2 * WINDOW, N_KV_HEADS, HEAD_DIM)

    a_idx = jnp.arange(WINDOW)[:, None]
    s_idx = jnp.arange(2 * WINDOW)[None, :]
    in_band = (s_idx >= a_idx) & (s_idx <= a_idx + WINDOW)
    blk = jnp.arange(nb)[:, None, None]
    valid = in_band[None] & ((blk > 0) | (s_idx[None] >= WINDOW))
    mask = jnp.broadcast_to(valid[None], (B, nb, WINDOW, 2 * WINDOW)).reshape(B * nb, 1, 1, WINDOW, 2 * WINDOW)
    o = _sink_attention(qb, band(k), band(v), mask, sinks)
    return o.reshape(B, T, ATT_Q)


def _swa_sample(q, k, v, k_past, v_past, pos, sinks):
    B, T = q.shape[:2]
    kk = jnp.concatenate([k_past.astype(k.dtype), k], axis=1)
    vv = jnp.concatenate([v_past.astype(v.dtype), v], axis=1)
    kpos = pos[0] - WINDOW + jnp.arange(WINDOW + T)
    diff = pos[:, None] - kpos[None, :]
    mask = ((diff >= 0) & (diff <= WINDOW))[None, None, None]
    o = _sink_attention(q.reshape(B, T, N_KV_HEADS, GROUP, HEAD_DIM), kk, vv, mask, sinks)
    return o.reshape(B, T, ATT_Q), kk[:, T:], vv[:, T:]


def _conv_ffn(h, conv_past, w_up, w_conv, b_conv, w_down):
    u = h @ w_up
    T = u.shape[1]
    uu = jnp.concatenate([conv_past.astype(u.dtype), u], axis=1)
    c = b_conv
    for j in range(CONV_WIDTH):
        c = c + uu[:, j:j + T] * w_conv[j]
    g, val = jnp.split(c, 2, axis=-1)
    y = jax.nn.gelu(g, approximate=True) * val
    return y @ w_down, uu[:, -(CONV_WIDTH - 1):]


def _layer(x, pos, C0, n0, m0, k_past, v_past, conv_past, chunk,
           g_pre_mix, w_in, b_if, sinks, w_branch_a, w_branch_b, w_out, g_post_mix,
           g_pre_ffn, w_up, w_conv, b_conv, w_down, g_post_ffn):
    B, T, _ = x.shape
    f32 = jnp.float32
    sizes = [MLSTM_QK, MLSTM_QK, MLSTM_V, MLSTM_V, N_MLSTM_HEADS, N_MLSTM_HEADS,
             ATT_Q, ATT_KV, ATT_KV, D_MODEL, D_MODEL]
    idx = [int(i) for i in np.cumsum(sizes)[:-1]]
    h = _rmsnorm(x, g_pre_mix)
    z = h @ w_in
    q_m, k_m, v_m, o_m, ig, fg, q_a, k_a, v_a, gate_a, gate_b = jnp.split(z, idx, axis=-1)

    ifb = b_if.astype(f32)
    qm = q_m.reshape(B, T, N_MLSTM_HEADS, MLSTM_DK).astype(f32)
    km = k_m.reshape(B, T, N_MLSTM_HEADS, MLSTM_DK).astype(f32) * (MLSTM_DK ** -0.5)
    vm = v_m.reshape(B, T, N_MLSTM_HEADS, MLSTM_DV).astype(f32)
    igt = ig.astype(f32) + ifb[:N_MLSTM_HEADS]
    lft = jax.nn.log_sigmoid(fg.astype(f32) + ifb[N_MLSTM_HEADS:])
    hm, C, n, m = _mlstm(qm, km, vm, igt, lft, C0.astype(f32), n0.astype(f32), m0.astype(f32), chunk)
    hm = (jax.nn.sigmoid(o_m.astype(f32)) * hm.reshape(B, T, MLSTM_V)).astype(x.dtype)

    qa = _rope(q_a.reshape(B, T, N_Q_HEADS, HEAD_DIM), pos)
    ka = _rope(k_a.reshape(B, T, N_KV_HEADS, HEAD_DIM), pos)
    va = v_a.reshape(B, T, N_KV_HEADS, HEAD_DIM)
    snk = sinks.reshape(N_KV_HEADS, GROUP)
    if k_past is None:
        ha = _swa_prompt(qa, ka, va, snk)
        new_k, new_v = ka[:, T - WINDOW:], va[:, T - WINDOW:]
    else:
        ha, new_k, new_v = _swa_sample(qa, ka, va, k_past, v_past, pos, snk)

    mix = (jax.nn.sigmoid(gate_a) * (hm @ w_branch_a) + jax.nn.sigmoid(gate_b) * (ha @ w_branch_b)) @ w_out
    x = x + _rmsnorm(mix, g_post_mix)

    f, conv_new = _conv_ffn(_rmsnorm(x, g_pre_ffn), conv_past, w_up, w_conv, b_conv, w_down)
    x = x + _rmsnorm(f, g_post_ffn)
    dt = x.dtype
    return x, C.astype(dt), n.astype(dt), m.astype(dt), new_k, new_v, conv_new


def setup_inputs(seed: int = 0) -> dict:
    key = jax.random.key(seed)
    ks = jax.random.split(key, 24)
    f32 = jnp.float32
    L = DEPTH

    def nrm(k, shape, s):
        return jax.random.normal(k, shape, f32) * s

    return {
        "x_prompt": nrm(ks[0], (BATCH, SEQ, D_MODEL), 1.0),
        "x_sample": nrm(ks[1], (DEC_BATCH, DEC_SEQ, D_MODEL), 1.0),
        "state_mlstm_c": nrm(ks[2], (L, DEC_BATCH, N_MLSTM_HEADS, MLSTM_DV, MLSTM_DK), 0.1),
        "state_mlstm_n": nrm(ks[3], (L, DEC_BATCH, N_MLSTM_HEADS, MLSTM_DK), 0.1),
        "state_mlstm_m": nrm(ks[4], (L, DEC_BATCH, N_MLSTM_HEADS), 0.5),
        "cache_swa_k": nrm(ks[5], (L, DEC_BATCH, WINDOW, N_KV_HEADS, HEAD_DIM), 1.0),
        "cache_swa_v": nrm(ks[6], (L, DEC_BATCH, WINDOW, N_KV_HEADS, HEAD_DIM), 1.0),
        "state_ffn_conv": nrm(ks[7], (L, DEC_BATCH, CONV_WIDTH - 1, 2 * D_FF), 1.0),
        "g_pre_mix": 1.0 + nrm(ks[8], (L, D_MODEL), 0.05),
        "w_in": nrm(ks[9], (L, D_MODEL, D_IN), D_MODEL ** -0.5),
        "b_if": jnp.concatenate([-1.0 + nrm(ks[10], (L, N_MLSTM_HEADS), 0.1),
                                 3.0 + nrm(ks[11], (L, N_MLSTM_HEADS), 0.1)], axis=-1),
        "attn_sinks": nrm(ks[12], (L, N_Q_HEADS), 0.5),
        "w_branch_a": nrm(ks[13], (L, MLSTM_V, D_MODEL), MLSTM_V ** -0.5),
        "w_branch_b": nrm(ks[14], (L, ATT_Q, D_MODEL), ATT_Q ** -0.5),
        "w_out": nrm(ks[15], (L, D_MODEL, D_MODEL), D_MODEL ** -0.5),
        "g_post_mix": 1.0 + nrm(ks[16], (L, D_MODEL), 0.05),
        "g_pre_ffn": 1.0 + nrm(ks[17], (L, D_MODEL), 0.05),
        "w_up": nrm(ks[18], (L, D_MODEL, 2 * D_FF), D_MODEL ** -0.5),
        "w_conv": nrm(ks[19], (L, CONV_WIDTH, 2 * D_FF), CONV_WIDTH ** -0.5),
        "b_conv": nrm(ks[20], (L, 2 * D_FF), 0.02),
        "w_down": nrm(ks[21], (L, D_FF, D_MODEL), D_FF ** -0.5),
        "g_post_ffn": 1.0 + nrm(ks[22], (L, D_MODEL), 0.05),
    }


def reference(x_prompt, x_sample, state_mlstm_c, state_mlstm_n, state_mlstm_m,
              cache_swa_k, cache_swa_v, state_ffn_conv,
              g_pre_mix, w_in, b_if, attn_sinks, w_branch_a, w_branch_b, w_out, g_post_mix,
              g_pre_ffn, w_up, w_conv, b_conv, w_down, g_post_ffn):
    Bp, Tp, _ = x_prompt.shape
    Bs, Ts, _ = x_sample.shape
    pos_p = jnp.arange(Tp)
    pos_s = PAST_LEN + jnp.arange(Ts)
    dt = x_prompt.dtype
    yp, ys = x_prompt, x_sample
    cp, np_, mp, kp, vp, fp = [], [], [], [], [], []
    cs, ns, ms, kss, vs, fs = [], [], [], [], [], []
    for l in range(DEPTH):
        lw = (g_pre_mix[l], w_in[l], b_if[l], attn_sinks[l], w_branch_a[l], w_branch_b[l], w_out[l],
              g_post_mix[l], g_pre_ffn[l], w_up[l], w_conv[l], b_conv[l], w_down[l], g_post_ffn[l])
        C0 = jnp.zeros((Bp, N_MLSTM_HEADS, MLSTM_DV, MLSTM_DK), dt)
        n0 = jnp.zeros((Bp, N_MLSTM_HEADS, MLSTM_DK), dt)
        m0 = jnp.zeros((Bp, N_MLSTM_HEADS), dt)
        conv0 = jnp.zeros((Bp, CONV_WIDTH - 1, 2 * D_FF), dt)
        yp, c_, n_, m_, k_, v_, f_ = _layer(yp, pos_p, C0, n0, m0, None, None, conv0, MLSTM_CHUNK, *lw)
        cp.append(c_); np_.append(n_); mp.append(m_); kp.append(k_); vp.append(v_); fp.append(f_)
        ys, c_, n_, m_, k_, v_, f_ = _layer(ys, pos_s, state_mlstm_c[l], state_mlstm_n[l], state_mlstm_m[l],
                                            cache_swa_k[l], cache_swa_v[l], state_ffn_conv[l], Ts, *lw)
        cs.append(c_); ns.append(n_); ms.append(m_); kss.append(k_); vs.append(v_); fs.append(f_)
    return (yp, ys,
            jnp.stack(cp), jnp.stack(np_), jnp.stack(mp), jnp.stack(kp), jnp.stack(vp), jnp.stack(fp),
            jnp.stack(cs), jnp.stack(ns), jnp.stack(ms), jnp.stack(kss), jnp.stack(vs), jnp.stack(fs))
```

```python
import functools

import jax
import jax.numpy as jnp
from jax import lax
from jax.experimental import pallas as pl
from jax.experimental.pallas import tpu as pltpu

F32 = jnp.float32
BF16 = jnp.bfloat16

N_MH = 4
DK = 256
DV = 512
CHUNK = 64
HD = 64
N_Q = 32
N_KV = 4
GROUP = N_Q // N_KV
WINDOW = 128
ROT = 16
THETA = 500000.0
PAST_LEN = 16384
EPS = 1e-6
D_MODEL = 2048
QK_W = N_MH * DK
V_W = N_MH * DV
ATT_Q = N_Q * HD
ATT_KV = N_KV * HD

_Z_QM = 0
_Z_KM = _Z_QM + QK_W
_Z_VM = _Z_KM + QK_W
_Z_OM = _Z_VM + V_W
_Z_QA = _Z_OM + V_W
_Z_GA = _Z_QA + ATT_Q
_Z_GB = _Z_GA + D_MODEL
_Z_KA = _Z_GB + D_MODEL
_Z_VA = _Z_KA + ATT_KV
_Z_IF = _Z_VA + ATT_KV
_PROJ_TN = 768
_Z_W = 13056
LANES = 128

_VMEM_LIMIT = 56 * 1024 * 1024


def _cparams(sem):
    return pltpu.CompilerParams(dimension_semantics=sem, vmem_limit_bytes=_VMEM_LIMIT)


def _rms(x, g):
    return x * lax.rsqrt(jnp.mean(x * x, axis=-1, keepdims=True) + EPS) * g


def _sigmoid(x):
    return 1.0 / (1.0 + jnp.exp(-x))


def _log_sigmoid(x):
    return jnp.minimum(x, 0.0) - jnp.log1p(jnp.exp(-jnp.abs(x)))


def _dot(a, b):
    return jnp.dot(a, b, preferred_element_type=F32)


def _dot_nt(a, b):
    return lax.dot_general(a, b, (((1,), (1,)), ((), ())), preferred_element_type=F32)


def _dot_tn(a, b):
    return lax.dot_general(a, b, (((0,), (0,)), ((), ())), preferred_element_type=F32)


def _proj_kernel(x_ref, g_ref, w_ref, z_ref, h_ref):
    @pl.when(pl.program_id(1) == 0)
    def _():
        h_ref[...] = _rms(x_ref[...], g_ref[...]).astype(BF16)

    z_ref[...] = _dot(h_ref[...], w_ref[...])


def _project(x2d, g, w_all, tm):
    rows, d = x2d.shape
    n = w_all.shape[1]
    tn = _PROJ_TN
    return pl.pallas_call(
        _proj_kernel,
        grid=(rows // tm, n // tn),
        in_specs=[
            pl.BlockSpec((tm, d), lambda i, j: (i, 0)),
            pl.BlockSpec((1, d), lambda i, j: (0, 0)),
            pl.BlockSpec((d, tn), lambda i, j: (0, j)),
        ],
        out_specs=pl.BlockSpec((tm, tn), lambda i, j: (i, j)),
        out_shape=jax.ShapeDtypeStruct((rows, n), F32),
        scratch_shapes=[pltpu.VMEM((tm, d), BF16)],
        compiler_params=_cparams(("parallel", "arbitrary")),
        name="proj",
    )(x2d, g.reshape(1, d), w_all)


def _mlstm_prompt_kernel(q_ref, k_ref, v_ref, o_ref, g_ref, bias_ref,
                         hm_ref, c_out, n_out, m_out, ct_s, n_s, m_s):
    c = pl.program_id(1)
    last = pl.num_programs(1) - 1
    L = CHUNK

    @pl.when(c == 0)
    def _():
        ct_s[...] = jnp.zeros_like(ct_s)
        n_s[...] = jnp.zeros_like(n_s)
        m_s[...] = jnp.zeros_like(m_s)

    gates = g_ref[...] + bias_ref[...]
    row = lax.broadcasted_iota(jnp.int32, (L, LANES), 0)
    bcum = _log_sigmoid(gates)
    for d in (1, 2, 4, 8, 16, 32):
        bcum = bcum + jnp.where(row >= d, pltpu.roll(bcum, d, axis=0), 0.0)

    ri = lax.broadcasted_iota(jnp.int32, (L, L), 0)
    ci = lax.broadcasted_iota(jnp.int32, (L, L), 1)
    eye = ri == ci
    causal = ci <= ri

    for h in range(N_MH):
        q = q_ref[:, h * DK:(h + 1) * DK]
        k = k_ref[:, h * DK:(h + 1) * DK] * (DK ** -0.5)
        v = v_ref[:, h * DV:(h + 1) * DV]
        o = o_ref[:, h * DV:(h + 1) * DV]
        b_col = bcum[:, N_MH + h:N_MH + h + 1]
        ig_col = gates[:, h:h + 1]
        a_col = ig_col - b_col
        a_lane = jnp.sum(jnp.where(eye, a_col, 0.0), axis=0, keepdims=True)
        dmat = jnp.where(causal, b_col + a_lane, -jnp.inf)
        m_prev = m_s[h:h + 1, 0:1]
        inter = b_col + m_prev
        m_t = jnp.maximum(jnp.max(dmat, axis=1, keepdims=True), inter)
        qb = q.astype(BF16)
        kb = k.astype(BF16)
        vb = v.astype(BF16)
        s = _dot_nt(qb, kb) * jnp.exp(dmat - m_t)
        w_inter = jnp.exp(inter - m_t)
        ct = ct_s[h]
        num = _dot(s.astype(BF16), vb) + w_inter * _dot(qb, ct.astype(BF16))
        n_row = n_s[h:h + 1, :]
        den = jnp.sum(s, axis=1, keepdims=True) + w_inter * jnp.sum(q * n_row, axis=1, keepdims=True)
        hout = num / jnp.maximum(jnp.abs(den), jnp.exp(-m_t))
        hm_ref[:, h * DV:(h + 1) * DV] = (_sigmoid(o) * hout).astype(BF16)

        m_new = m_t[L - 1:L, :]
        b_last = b_col[L - 1:L, :]
        w_last = jnp.exp(b_last - b_col + ig_col - m_new)
        decay = jnp.exp(b_last + m_prev - m_new)
        kw = k * w_last
        ct_new = decay * ct + _dot_tn(kw.astype(BF16), vb)
        n_new = decay * n_row + jnp.sum(kw, axis=0, keepdims=True)
        ct_s[h] = ct_new
        n_s[h:h + 1, :] = n_new
        m_s[h:h + 1, :] = jnp.broadcast_to(m_new, (1, LANES))

        @pl.when(c == last)
        def _():
            c_out[0, h] = ct_new.T
            n_out[0, h:h + 1, :] = n_new

    @pl.when(c == last)
    def _():
        m_out[0] = m_s[...]


def _mlstm_prompt(z, bias_row, batch, seq):
    rows = z.shape[0]
    nc = seq // CHUNK

    def zspec(width, off):
        blk = off // width
        return pl.BlockSpec((CHUNK, width), lambda b, c: (b * nc + c, blk))

    return pl.pallas_call(
        _mlstm_prompt_kernel,
        grid=(batch, nc),
        in_specs=[
            zspec(QK_W, _Z_QM), zspec(QK_W, _Z_KM), zspec(V_W, _Z_VM), zspec(V_W, _Z_OM),
            zspec(LANES, _Z_IF),
            pl.BlockSpec((1, LANES), lambda b, c: (0, 0)),
        ],
        out_specs=[
            pl.BlockSpec((CHUNK, V_W), lambda b, c: (b * nc + c, 0)),
            pl.BlockSpec((1, N_MH, DV, DK), lambda b, c: (b, 0, 0, 0)),
            pl.BlockSpec((1, N_MH, DK), lambda b, c: (b, 0, 0)),
            pl.BlockSpec((1, 8, LANES), lambda b, c: (b, 0, 0)),
        ],
        out_shape=[
            jax.ShapeDtypeStruct((rows, V_W), BF16),
            jax.ShapeDtypeStruct((batch, N_MH, DV, DK), F32),
            jax.ShapeDtypeStruct((batch, N_MH, DK), F32),
            jax.ShapeDtypeStruct((batch, 8, LANES), F32),
        ],
        scratch_shapes=[
            pltpu.VMEM((N_MH, DK, DV), F32),
            pltpu.VMEM((8, DK), F32),
            pltpu.VMEM((8, LANES), F32),
        ],
        compiler_params=_cparams(("parallel", "arbitrary")),
        name="mlstm_prompt",
    )(z, z, z, z, z, bias_row)


def _mlstm_sample_kernel(q_ref, k_ref, v_ref, o_ref, g_ref, bias_ref, c_ref, n_ref, m_ref,
                         hm_ref, c_out, n_out, m_out):
    gates = g_ref[0] + bias_ref[...]
    lf_all = _log_sigmoid(gates)
    m_all = m_ref[0]
    ri = lax.broadcasted_iota(jnp.int32, (DV, DV), 0)
    ci = lax.broadcasted_iota(jnp.int32, (DV, DV), 1)
    eye = ri == ci
    m_new_all = jnp.zeros((1, LANES), F32)
    lane = lax.broadcasted_iota(jnp.int32, (1, LANES), 1)

    for h in range(N_MH):
        q = q_ref[0, :, h * DK:(h + 1) * DK]
        k = k_ref[0, :, h * DK:(h + 1) * DK] * (DK ** -0.5)
        v = v_ref[0, :, h * DV:(h + 1) * DV]
        o = o_ref[0, :, h * DV:(h + 1) * DV]
        ig = gates[:, h:h + 1]
        lf = lf_all[:, N_MH + h:N_MH + h + 1]
        m_prev = m_all[:, h:h + 1]
        inter = lf + m_prev
        m_t = jnp.maximum(ig, inter)
        s = jnp.sum(q * k, axis=1, keepdims=True) * jnp.exp(ig - m_t)
        w_inter = jnp.exp(inter - m_t)
        cmat = c_ref[0, h]
        q8 = jnp.broadcast_to(q, (8, DK)).astype(BF16)
        cq = _dot_nt(q8, cmat.astype(BF16))[0:1, :]
        n_row = n_ref[0, h:h + 1, :]
        num = s * v + w_inter * cq
        den = s + w_inter * jnp.sum(n_row * q, axis=1, keepdims=True)
        hout = num / jnp.maximum(jnp.abs(den), jnp.exp(-m_t))
        hm_ref[0, :, h * DV:(h + 1) * DV] = (_sigmoid(o) * hout).astype(BF16)

        w_last = jnp.exp(ig - m_t)
        decay = jnp.exp(inter - m_t)
        vw = v * w_last
        vw_col = jnp.sum(jnp.where(eye, vw, 0.0), axis=1, keepdims=True)
        c_out[0, h] = decay * cmat + vw_col * k
        n_out[0, h:h + 1, :] = decay * n_row + w_last * k
        m_new_all = jnp.where(lane == h, m_t, m_new_all)

    m_out[0] = m_new_all


def _mlstm_sample(z3, bias_row, c0, n0, m0_pad):
    nb = z3.shape[0]

    def zspec(width, off):
        blk = off // width
        return pl.BlockSpec((1, 1, width), lambda b: (b, 0, blk))

    return pl.pallas_call(
        _mlstm_sample_kernel,
        grid=(nb,),
        in_specs=[
            zspec(QK_W, _Z_QM), zspec(QK_W, _Z_KM), zspec(V_W, _Z_VM), zspec(V_W, _Z_OM),
            zspec(LANES, _Z_IF),
            pl.BlockSpec((1, LANES), lambda b: (0, 0)),
            pl.BlockSpec((1, N_MH, DV, DK), lambda b: (b, 0, 0, 0)),
            pl.BlockSpec((1, N_MH, DK), lambda b: (b, 0, 0)),
            pl.BlockSpec((1, 1, LANES), lambda b: (b, 0, 0)),
        ],
        out_specs=[
            pl.BlockSpec((1, 1, V_W), lambda b: (b, 0, 0)),
            pl.BlockSpec((1, N_MH, DV, DK), lambda b: (b, 0, 0, 0)),
            pl.BlockSpec((1, N_MH, DK), lambda b: (b, 0, 0)),
            pl.BlockSpec((1, 1, LANES), lambda b: (b, 0, 0)),
        ],
        out_shape=[
            jax.ShapeDtypeStruct((nb, 1, V_W), BF16),
            jax.ShapeDtypeStruct((nb, N_MH, DV, DK), F32),
            jax.ShapeDtypeStruct((nb, N_MH, DK), F32),
            jax.ShapeDtypeStruct((nb, 1, LANES), F32),
        ],
        compiler_params=_cparams(("parallel",)),
        name="mlstm_sample",
    )(z3, z3, z3, z3, z3, bias_row, c0, n0, m0_pad)


def _rope_tables(pos):
    half = ROT // 2
    inv = THETA ** (-jnp.arange(half, dtype=F32) * 2.0 / ROT)
    ang = pos.astype(F32)[:, None] * inv[None, :]
    cos = jnp.cos(ang)
    sin = jnp.sin(ang)
    t = pos.shape[0]
    cos_t = jnp.concatenate([cos, cos, jnp.ones((t, HD - ROT), F32)], axis=1)
    sin_a = jnp.concatenate([-sin, jnp.zeros((t, HD - half), F32)], axis=1)
    sin_b = jnp.concatenate([jnp.zeros((t, half), F32), sin, jnp.zeros((t, HD - ROT), F32)], axis=1)
    rep = LANES // HD
    return tuple(jnp.tile(a, (1, rep)) for a in (cos_t, sin_a, sin_b))


def _rope(x, cos_t, sin_a, sin_b):
    w = x.shape[1]
    rep = w // LANES
    half = ROT // 2
    if rep > 1:
        cos_t = jnp.concatenate([cos_t] * rep, axis=1)
        sin_a = jnp.concatenate([sin_a] * rep, axis=1)
        sin_b = jnp.concatenate([sin_b] * rep, axis=1)
    up = pltpu.roll(x, w - half, axis=1)
    down = pltpu.roll(x, half, axis=1)
    return x * cos_t + up * sin_a + down * sin_b


def _swa_prompt_kernel(sink_ref, q_ref, kc_ref, kp_ref, vc_ref, vp_ref,
                       cc_ref, sac_ref, sbc_ref, cp_ref, sap_ref, sbp_ref,
                       ha_ref, krot_ref):
    i = pl.program_id(1)
    w = WINDOW
    cc, sac, sbc = cc_ref[...], sac_ref[...], sbc_ref[...]
    q = _rope(q_ref[...], cc, sac, sbc) * (HD ** -0.5)
    kc = _rope(kc_ref[...], cc, sac, sbc)
    kp = _rope(kp_ref[...], cp_ref[...], sap_ref[...], sbp_ref[...])
    krot_ref[...] = kc
    kk = jnp.concatenate([kp, kc], axis=0).astype(BF16)
    vv = jnp.concatenate([vp_ref[...], vc_ref[...]], axis=0).astype(BF16)
    a_idx = lax.broadcasted_iota(jnp.int32, (w, 2 * w), 0)
    s_idx = lax.broadcasted_iota(jnp.int32, (w, 2 * w), 1)
    valid = (s_idx >= a_idx) & (s_idx <= a_idx + w) & ((i > 0) | (s_idx >= w))
    outs = []
    for g in range(N_KV):
        kg = kk[:, g * HD:(g + 1) * HD]
        vg = vv[:, g * HD:(g + 1) * HD]
        for hh in range(GROUP):
            h = g * GROUP + hh
            qh = q[:, h * HD:(h + 1) * HD].astype(BF16)
            s = jnp.where(valid, _dot_nt(qh, kg), -jnp.inf)
            sink = sink_ref[h]
            m = jnp.maximum(jnp.max(s, axis=1, keepdims=True), sink)
            e = jnp.exp(s - m)
            denom = jnp.sum(e, axis=1, keepdims=True) + jnp.exp(sink - m)
            p = e * (1.0 / denom)
            outs.append(_dot(p.astype(BF16), vg))
    ha_ref[...] = jnp.concatenate(outs, axis=1).astype(BF16)


def _swa_prompt(z, sinks, tables, batch, seq):
    rows = z.shape[0]
    nb = seq // WINDOW
    w = WINDOW

    def cur(width, off):
        blk = off // width
        return pl.BlockSpec((w, width), lambda b, i: (b * nb + i, blk))

    def prev(width, off):
        blk = off // width
        return pl.BlockSpec((w, width), lambda b, i: (b * nb + jnp.maximum(i - 1, 0), blk))

    tab_cur = pl.BlockSpec((w, LANES), lambda b, i: (i, 0))
    tab_prev = pl.BlockSpec((w, LANES), lambda b, i: (jnp.maximum(i - 1, 0), 0))
    return pl.pallas_call(
        _swa_prompt_kernel,
        grid=(batch, nb),
        in_specs=[
            pl.BlockSpec(memory_space=pltpu.SMEM),
            cur(ATT_Q, _Z_QA), cur(ATT_KV, _Z_KA), prev(ATT_KV, _Z_KA),
            cur(ATT_KV, _Z_VA), prev(ATT_KV, _Z_VA),
            tab_cur, tab_cur, tab_cur, tab_prev, tab_prev, tab_prev,
        ],
        out_specs=[
            pl.BlockSpec((w, ATT_Q), lambda b, i: (b * nb + i, 0)),
            pl.BlockSpec((w, ATT_KV), lambda b, i: (b * nb + i, 0)),
        ],
        out_shape=[
            jax.ShapeDtypeStruct((rows, ATT_Q), BF16),
            jax.ShapeDtypeStruct((rows, ATT_KV), F32),
        ],
        compiler_params=_cparams(("parallel", "parallel")),
        name="swa_prompt",
    )(sinks, z, z, z, z, z, *tables, *tables)


def _rope_sample_kernel(q_ref, k_ref, c_ref, sa_ref, sb_ref, qo_ref, ko_ref):
    n = q_ref.shape[0]
    cos_t = jnp.broadcast_to(c_ref[...], (n, LANES))
    sin_a = jnp.broadcast_to(sa_ref[...], (n, LANES))
    sin_b = jnp.broadcast_to(sb_ref[...], (n, LANES))
    qo_ref[...] = _rope(q_ref[...], cos_t, sin_a, sin_b) * (HD ** -0.5)
    ko_ref[...] = _rope(k_ref[...], cos_t, sin_a, sin_b)


def _rope_sample(z, tables):
    nb = z.shape[0]
    tab = pl.BlockSpec((1, LANES), lambda i: (0, 0))
    return pl.pallas_call(
        _rope_sample_kernel,
        grid=(1,),
        in_specs=[
            pl.BlockSpec((nb, ATT_Q), lambda i: (0, _Z_QA // ATT_Q)),
            pl.BlockSpec((nb, ATT_KV), lambda i: (0, _Z_KA // ATT_KV)),
            tab, tab, tab,
        ],
        out_specs=[
            pl.BlockSpec((nb, ATT_Q), lambda i: (0, 0)),
            pl.BlockSpec((nb, ATT_KV), lambda i: (0, 0)),
        ],
        out_shape=[
            jax.ShapeDtypeStruct((nb, ATT_Q), F32),
            jax.ShapeDtypeStruct((nb, ATT_KV), F32),
        ],
        compiler_params=_cparams(("arbitrary",)),
        name="rope_sample",
    )(z, z, *tables)


def _swa_sample_kernel(q_ref, kn_ref, vn_ref, kp_ref, vp_ref, sink_ref,
                       o_ref, ko_ref, vo_ref, *, bb):
    w = WINDOW
    hrow = lax.broadcasted_iota(jnp.int32, (N_Q, ATT_KV), 0) // GROUP
    glane = lax.broadcasted_iota(jnp.int32, (N_Q, ATT_KV), 1) // HD
    own = hrow == glane
    srow = lax.broadcasted_iota(jnp.int32, (w, ATT_KV), 0)
    sink = sink_ref[...]
    for b in range(bb):
        q = q_ref[b]
        qm = jnp.where(own, jnp.concatenate([q] * N_KV, axis=1), 0.0)
        k_past = kp_ref[b]
        v_past = vp_ref[b]
        k_new = kn_ref[b]
        v_new = vn_ref[b]
        s_past = _dot_nt(qm.astype(BF16), k_past.astype(BF16))
        s_new = jnp.sum(qm * k_new, axis=1, keepdims=True)
        m = jnp.maximum(jnp.maximum(jnp.max(s_past, axis=1, keepdims=True), s_new), sink)
        e_past = jnp.exp(s_past - m)
        e_new = jnp.exp(s_new - m)
        denom = jnp.sum(e_past, axis=1, keepdims=True) + e_new + jnp.exp(sink - m)
        r = 1.0 / denom
        pv = _dot((e_past * r).astype(BF16), v_past.astype(BF16)) + (e_new * r) * v_new
        pv = jnp.where(own, pv, 0.0)
        o = pv[:, 0:HD]
        for g in range(1, N_KV):
            o = o + pv[:, g * HD:(g + 1) * HD]
        o_ref[b] = o
        ko_ref[b] = jnp.where(srow == w - 1, k_new, pltpu.roll(k_past, w - 1, axis=0))
        vo_ref[b] = jnp.where(srow == w - 1, v_new, pltpu.roll(v_past, w - 1, axis=0))


def _swa_sample(q3, k_new3, v_new3, k_past, v_past, sink_col, bb):
    nb = q3.shape[0]
    w = WINDOW
    return pl.pallas_call(
        functools.partial(_swa_sample_kernel, bb=bb),
        grid=(nb // bb,),
        in_specs=[
            pl.BlockSpec((bb, N_Q, HD), lambda i: (i, 0, 0)),
            pl.BlockSpec((bb, 1, ATT_KV), lambda i: (i, 0, 0)),
            pl.BlockSpec((bb, 1, ATT_KV), lambda i: (i, 0, _Z_VA // ATT_KV)),
            pl.BlockSpec((bb, w, ATT_KV), lambda i: (i, 0, 0)),
            pl.BlockSpec((bb, w, ATT_KV), lambda i: (i, 0, 0)),
            pl.BlockSpec((N_Q, 1), lambda i: (0, 0)),
        ],
        out_specs=[
            pl.BlockSpec((bb, N_Q, HD), lambda i: (i, 0, 0)),
            pl.BlockSpec((bb, w, ATT_KV), lambda i: (i, 0, 0)),
            pl.BlockSpec((bb, w, ATT_KV), lambda i: (i, 0, 0)),
        ],
        out_shape=[
            jax.ShapeDtypeStruct((nb, N_Q, HD), F32),
            jax.ShapeDtypeStruct((nb, w, ATT_KV), F32),
            jax.ShapeDtypeStruct((nb, w, ATT_KV), F32),
        ],
        compiler_params=_cparams(("parallel",)),
        name="swa_sample",
    )(q3, k_new3, v_new3, k_past, v_past, sink_col)


def _merge1_kernel(hm_ref, ha_ref, ga_ref, gb_ref, wa_ref, wb_ref, t_ref):
    a = _dot(hm_ref[...], wa_ref[...])
    b = _dot(ha_ref[...], wb_ref[...])
    t_ref[...] = (_sigmoid(ga_ref[...]) * a + _sigmoid(gb_ref[...]) * b).astype(BF16)


def _merge1(hm, ha, z, wa, wb, tm, tn):
    rows, d = hm.shape
    n = wa.shape[1]
    return pl.pallas_call(
        _merge1_kernel,
        grid=(rows // tm, n // tn),
        in_specs=[
            pl.BlockSpec((tm, d), lambda i, j: (i, 0)),
            pl.BlockSpec((tm, d), lambda i, j: (i, 0)),
            pl.BlockSpec((tm, tn), lambda i, j: (i, _Z_GA // tn + j)),
            pl.BlockSpec((tm, tn), lambda i, j: (i, _Z_GB // tn + j)),
            pl.BlockSpec((d, tn), lambda i, j: (0, j)),
            pl.BlockSpec((d, tn), lambda i, j: (0, j)),
        ],
        out_specs=pl.BlockSpec((tm, tn), lambda i, j: (i, j)),
        out_shape=jax.ShapeDtypeStruct((rows, n), BF16),
        compiler_params=_cparams(("parallel", "arbitrary")),
        name="merge1",
    )(hm, ha, z, z, wa, wb)


def _merge2_kernel(t_ref, w_ref, x_ref, g1_ref, g2_ref, x1_ref, h2_ref):
    mix = _dot(t_ref[...], w_ref[...])
    x1 = x_ref[...] + _rms(mix, g1_ref[...])
    x1_ref[...] = x1
    h2_ref[...] = _rms(x1, g2_ref[...]).astype(BF16)


def _merge2(t, w_out, x2d, g1, g2, tm):
    rows, d = x2d.shape
    row = lambda i: (i, 0)
    const = lambda i: (0, 0)
    return pl.pallas_call(
        _merge2_kernel,
        grid=(rows // tm,),
        in_specs=[
            pl.BlockSpec((tm, d), row),
            pl.BlockSpec((d, d), const),
            pl.BlockSpec((tm, d), row),
            pl.BlockSpec((1, d), const),
            pl.BlockSpec((1, d), const),
        ],
        out_specs=[pl.BlockSpec((tm, d), row), pl.BlockSpec((tm, d), row)],
        out_shape=[jax.ShapeDtypeStruct((rows, d), F32), jax.ShapeDtypeStruct((rows, d), BF16)],
        compiler_params=_cparams(("parallel",)),
        name="merge2",
    )(t, w_out, x2d, g1.reshape(1, d), g2.reshape(1, d))


def _gelu_tanh(x):
    return 0.5 * x * (1.0 + jnp.tanh(0.7978845608028654 * (x + 0.044715 * (x * x * x))))


def _conv_taps(u, u1, u2, cw_ref, cb_ref):
    return cb_ref[...] + u2 * cw_ref[0:1, :] + u1 * cw_ref[1:2, :] + u * cw_ref[2:3, :]


def _ffn_tail(j, nf, y, wd_ref, x1_ref, g_ref, y_ref, acc_ref):
    contrib = _dot(y, wd_ref[...])

    @pl.when(j == 0)
    def _():
        acc_ref[...] = contrib

    @pl.when(j > 0)
    def _():
        acc_ref[...] += contrib

    @pl.when(j == nf - 1)
    def _():
        y_ref[...] = x1_ref[...] + _rms(acc_ref[...], g_ref[...])


def _ffn_prompt_kernel(h_ref, wg_ref, wv_ref, cwg_ref, cwv_ref, cbg_ref, cbv_ref, wd_ref,
                       x1_ref, g_ref, y_ref, sg_ref, sv_ref, acc_ref, carry_ref,
                       *, tiles_per_seq):
    i = pl.program_id(0)
    j = pl.program_id(1)
    nf = pl.num_programs(1)
    tm = h_ref.shape[0]

    @pl.when((i == 0) & (j == 0))
    def _():
        carry_ref[...] = jnp.zeros_like(carry_ref)

    seq_start = (i % tiles_per_seq) == 0
    h = h_ref[...]
    row = lax.broadcasted_iota(jnp.int32, (tm, wg_ref.shape[1]), 0)

    def branch(w_ref, cw_ref, cb_ref, slot, state_ref):
        u = _dot(h, w_ref[...])
        prev = jnp.where(seq_start, 0.0, carry_ref[slot])
        p2 = prev[6:7, :]
        p1 = prev[7:8, :]
        u1 = jnp.where(row == 0, p1, pltpu.roll(u, 1, axis=0))
        u2 = jnp.where(row == 0, p2, jnp.where(row == 1, p1, pltpu.roll(u, 2, axis=0)))
        tail = u[tm - 8:tm, :]
        carry_ref[slot] = tail
        state_ref[0] = tail
        return _conv_taps(u, u1, u2, cw_ref, cb_ref)

    cg = branch(wg_ref, cwg_ref, cbg_ref, 2 * j, sg_ref)
    cv = branch(wv_ref, cwv_ref, cbv_ref, 2 * j + 1, sv_ref)
    y = (_gelu_tanh(cg) * cv).astype(BF16)
    _ffn_tail(j, nf, y, wd_ref, x1_ref, g_ref, y_ref, acc_ref)


def _ffn_sample_kernel(h_ref, wg_ref, wv_ref, cwg_ref, cwv_ref, cbg_ref, cbv_ref, wd_ref,
                       x1_ref, g_ref, p2g_ref, p2v_ref, p1g_ref, p1v_ref,
                       y_ref, ug_ref, uv_ref, acc_ref):
    j = pl.program_id(1)
    nf = pl.num_programs(1)
    h = h_ref[...]
    ug = _dot(h, wg_ref[...])
    uv = _dot(h, wv_ref[...])
    ug_ref[...] = ug
    uv_ref[...] = uv
    cg = _conv_taps(ug, p1g_ref[...], p2g_ref[...], cwg_ref, cbg_ref)
    cv = _conv_taps(uv, p1v_ref[...], p2v_ref[...], cwv_ref, cbv_ref)
    y = (_gelu_tanh(cg) * cv).astype(BF16)
    _ffn_tail(j, nf, y, wd_ref, x1_ref, g_ref, y_ref, acc_ref)


def _ffn(h2, x1, w_up, w_conv, b_conv2d, w_down, g, tm, tf, seq=None, past=None):
    rows, d = h2.shape
    f = w_down.shape[0]
    nf = f // tf
    row = lambda i, j: (i, 0)
    gcol = lambda i, j: (0, j)
    vcol = lambda i, j: (0, nf + j)
    in_specs = [
        pl.BlockSpec((tm, d), row),
        pl.BlockSpec((d, tf), gcol), pl.BlockSpec((d, tf), vcol),
        pl.BlockSpec((3, tf), gcol), pl.BlockSpec((3, tf), vcol),
        pl.BlockSpec((1, tf), gcol), pl.BlockSpec((1, tf), vcol),
        pl.BlockSpec((tf, d), lambda i, j: (j, 0)),
        pl.BlockSpec((tm, d), row),
        pl.BlockSpec((1, d), lambda i, j: (0, 0)),
    ]
    args = [h2, w_up, w_up, w_conv, w_conv, b_conv2d, b_conv2d, w_down, x1, g.reshape(1, d)]
    scratch = [pltpu.VMEM((tm, d), F32)]
    if past is None:
        batch = rows // seq
        tiles_per_seq = seq // tm
        kern = functools.partial(_ffn_prompt_kernel, tiles_per_seq=tiles_per_seq)
        state_spec = pl.BlockSpec((1, 8, tf), lambda i, j: (i // tiles_per_seq, 0, j))
        out_specs = [pl.BlockSpec((tm, d), row), state_spec, state_spec]
        out_shape = [jax.ShapeDtypeStruct((rows, d), F32),
                     jax.ShapeDtypeStruct((batch, 8, f), F32),
                     jax.ShapeDtypeStruct((batch, 8, f), F32)]
        scratch.append(pltpu.VMEM((2 * nf, 8, tf), F32))
        name = "ffn_prompt"
    else:
        p2, p1 = past
        kern = _ffn_sample_kernel
        ucol = pl.BlockSpec((tm, tf), lambda i, j: (i, j))
        in_specs += [
            pl.BlockSpec((tm, tf), lambda i, j: (i, j)), pl.BlockSpec((tm, tf), lambda i, j: (i, nf + j)),
            pl.BlockSpec((tm, tf), lambda i, j: (i, j)), pl.BlockSpec((tm, tf), lambda i, j: (i, nf + j)),
        ]
        args += [p2, p2, p1, p1]
        out_specs = [pl.BlockSpec((tm, d), row), ucol, ucol]
        out_shape = [jax.ShapeDtypeStruct((rows, d), F32),
                     jax.ShapeDtypeStruct((rows, f), F32),
                     jax.ShapeDtypeStruct((rows, f), F32)]
        name = "ffn_sample"
    return pl.pallas_call(
        kern,
        grid=(rows // tm, nf),
        in_specs=in_specs,
        out_specs=out_specs,
        out_shape=out_shape,
        scratch_shapes=scratch,
        compiler_params=_cparams(("arbitrary", "arbitrary")),
        name=name,
    )(*args)


def _regroup_w_in(w_in):
    d = w_in.shape[0]
    o_if = QK_W * 2 + V_W * 2
    o_qa = o_if + 2 * N_MH
    o_ka = o_qa + ATT_Q
    o_ga = o_ka + 2 * ATT_KV
    parts = [
        w_in[:, :o_if],
        w_in[:, o_qa:o_ka],
        w_in[:, o_ga:],
        w_in[:, o_ka:o_ga],
        w_in[:, o_if:o_qa],
    ]
    used = sum(p.shape[1] for p in parts)
    parts.append(jnp.zeros((d, _Z_W - used), w_in.dtype))
    return jnp.concatenate(parts, axis=1).astype(BF16)


def _pick(rows, pref):
    t = min(rows, pref)
    while rows % t:
        t //= 2
    return t


def kernel(x_prompt, x_sample, state_mlstm_c, state_mlstm_n, state_mlstm_m, cache_swa_k, cache_swa_v,
           state_ffn_conv, g_pre_mix, w_in, b_if, attn_sinks, w_branch_a, w_branch_b, w_out, g_post_mix,
           g_pre_ffn, w_up, w_conv, b_conv, w_down, g_post_ffn):
    bp, tp, d = x_prompt.shape
    bs, ts, _ = x_sample.shape
    depth = w_in.shape[0]
    assert depth == 1 and ts == 1 and d == D_MODEL
    f = w_down.shape[1]
    tf = 512
    assert f % tf == 0

    l = 0
    w_all = _regroup_w_in(w_in[l])
    wa = w_branch_a[l].astype(BF16)
    wb = w_branch_b[l].astype(BF16)
    wo = w_out[l].astype(BF16)
    wu = w_up[l].astype(BF16)
    wd = w_down[l].astype(BF16)
    bias_row = jnp.concatenate([b_if[l].astype(F32), jnp.zeros((LANES - 2 * N_MH,), F32)]).reshape(1, LANES)
    sinks = attn_sinks[l].astype(F32)
    b_conv2d = b_conv[l].reshape(1, 2 * f)

    xp = x_prompt.reshape(bp * tp, d)
    zp = _project(xp, g_pre_mix[l], w_all, _pick(bp * tp, 512))
    hm_p, c_p, n_p, m_p = _mlstm_prompt(zp, bias_row, bp, tp)
    tabs_p = _rope_tables(jnp.arange(tp))
    ha_p, krot_p = _swa_prompt(zp, sinks, tabs_p, bp, tp)
    t_p = _merge1(hm_p, ha_p, zp, wa, wb, _pick(bp * tp, 512), 512)
    x1_p, h2_p = _merge2(t_p, wo, xp, g_post_mix[l], g_pre_ffn[l], _pick(bp * tp, 256))
    y_p, sg_p, sv_p = _ffn(h2_p, x1_p, wu, w_conv[l], b_conv2d, wd, g_post_ffn[l],
                           _pick(tp, 512), tf, seq=tp)

    y_prompt = y_p.reshape(bp, tp, d)
    c_prompt = c_p[None]
    n_prompt = n_p[None]
    m_prompt = m_p[:, :N_MH, 0][None]
    k_prompt = krot_p.reshape(bp, tp, N_KV, HD)[:, tp - WINDOW:][None]
    v_prompt = zp[:, _Z_VA:_Z_VA + ATT_KV].reshape(bp, tp, N_KV, HD)[:, tp - WINDOW:][None]
    conv_prompt = jnp.concatenate([sg_p[:, 6:8, :], sv_p[:, 6:8, :]], axis=-1)[None]

    xs = x_sample.reshape(bs, d)
    zs = _project(xs, g_pre_mix[l], w_all, bs)
    zs3 = zs.reshape(bs, 1, _Z_W)
    m0_pad = jnp.pad(state_mlstm_m[l].astype(F32), ((0, 0), (0, LANES - N_MH))).reshape(bs, 1, LANES)
    hm_s3, c_s, n_s, m_s3 = _mlstm_sample(zs3, bias_row, state_mlstm_c[l], state_mlstm_n[l], m0_pad)
    tabs_s = _rope_tables(PAST_LEN + jnp.arange(ts))
    q_rot, k_rot = _rope_sample(zs, tabs_s)
    o_s3, k_s, v_s = _swa_sample(
        q_rot.reshape(bs, N_Q, HD), k_rot.reshape(bs, 1, ATT_KV), zs3,
        cache_swa_k[l].reshape(bs, WINDOW, ATT_KV), cache_swa_v[l].reshape(bs, WINDOW, ATT_KV),
        sinks.reshape(N_Q, 1), _pick(bs, 8))
    ha_s = o_s3.reshape(bs, ATT_Q).astype(BF16)
    t_s = _merge1(hm_s3.reshape(bs, V_W), ha_s, zs, wa, wb, bs, 512)
    x1_s, h2_s = _merge2(t_s, wo, xs, g_post_mix[l], g_pre_ffn[l], bs)
    past = state_ffn_conv[l].astype(F32)
    y_s, ug_s, uv_s = _ffn(h2_s, x1_s, wu, w_conv[l], b_conv2d, wd, g_post_ffn[l],
                           bs, tf, past=(past[:, 0, :], past[:, 1, :]))

    y_sample = y_s.reshape(bs, ts, d)
    c_sample = c_s[None]
    n_sample = n_s[None]
    m_sample = m_s3[:, 0, :N_MH][None]
    k_sample = k_s.reshape(bs, WINDOW, N_KV, HD)[None]
    v_sample = v_s.reshape(bs, WINDOW, N_KV, HD)[None]
    conv_sample = jnp.stack([past[:, 1, :], jnp.concatenate([ug_s, uv_s], axis=-1)], axis=1)[None]

    return (y_prompt, y_sample,
            c_prompt, n_prompt, m_prompt, k_prompt, v_prompt, conv_prompt,
            c_sample, n_sample, m_sample, k_sample, v_sample, conv_sample)
```

```python
import functools

import jax
import jax.numpy as jnp
from jax import lax
from jax.experimental import pallas as pl
from jax.experimental.pallas import tpu as pltpu

F32 = jnp.float32
BF16 = jnp.bfloat16

N_MH = 4
DK = 256
DV = 512
CHUNK = 64
HD = 64
N_Q = 32
N_KV = 4
GROUP = N_Q // N_KV
WINDOW = 128
ROT = 16
THETA = 500000.0
PAST_LEN = 16384
EPS = 1e-6
D_MODEL = 2048
QK_W = N_MH * DK
V_W = N_MH * DV
ATT_Q = N_Q * HD
ATT_KV = N_KV * HD

_Z_QM = 0
_Z_KM = _Z_QM + QK_W
_Z_VM = _Z_KM + QK_W
_Z_OM = _Z_VM + V_W
_Z_QA = _Z_OM + V_W
_Z_GA = _Z_QA + ATT_Q
_Z_GB = _Z_GA + D_MODEL
_Z_KA = _Z_GB + D_MODEL
_Z_VA = _Z_KA + ATT_KV
_Z_IF = _Z_VA + ATT_KV
LANES = 128
_PROJ_TN = 1024
_Z_W = -(-(_Z_IF + LANES) // _PROJ_TN) * _PROJ_TN
assert _Z_QA % _PROJ_TN == 0 and _Z_GA % _PROJ_TN == 0 and _Z_KA % _PROJ_TN == 0

_VMEM_LIMIT = 56 * 1024 * 1024


def _cparams(sem):
    return pltpu.CompilerParams(dimension_semantics=sem, vmem_limit_bytes=_VMEM_LIMIT)


def _rms(x, g):
    return x * lax.rsqrt(jnp.mean(x * x, axis=-1, keepdims=True) + EPS) * g


def _sigmoid(x):
    return 1.0 / (1.0 + jnp.exp(-x))


def _log_sigmoid(x):
    return jnp.minimum(x, 0.0) - jnp.log1p(jnp.exp(-jnp.abs(x)))


def _dot(a, b):
    return jnp.dot(a, b, preferred_element_type=F32)


def _dot_nt(a, b):
    return lax.dot_general(a, b, (((1,), (1,)), ((), ())), preferred_element_type=F32)


def _dot_tn(a, b):
    return lax.dot_general(a, b, (((0,), (0,)), ((), ())), preferred_element_type=F32)


def _proj_kernel(x_ref, g_ref, w_ref, cos_ref, sa_ref, sb_ref, z_ref, h_ref):
    j = pl.program_id(1)
    jq0 = _Z_QA // _PROJ_TN
    jq1 = _Z_GA // _PROJ_TN
    jk = _Z_KA // _PROJ_TN

    @pl.when(j == 0)
    def _():
        h_ref[...] = _rms(x_ref[...], g_ref[...]).astype(BF16)

    is_q = (j >= jq0) & (j < jq1)
    is_k = j == jk

    @pl.when(is_q)
    def _():
        acc = _dot(h_ref[...], w_ref[...])
        z_ref[...] = _rope(acc, cos_ref[...], sa_ref[...], sb_ref[...]) * (HD ** -0.5)

    @pl.when(is_k)
    def _():
        acc = _dot(h_ref[...], w_ref[...])
        z_ref[...] = acc
        z_ref[:, :ATT_KV] = _rope(acc[:, :ATT_KV], cos_ref[...], sa_ref[...], sb_ref[...])

    @pl.when(jnp.logical_not(is_q | is_k))
    def _():
        z_ref[...] = _dot(h_ref[...], w_ref[...])


def _project(x2d, g, w_all, tables, tm):
    rows, d = x2d.shape
    n = w_all.shape[1]
    tn = _PROJ_TN
    tab_tiles = tables[0].shape[0] // tm
    tab = pl.BlockSpec((tm, LANES), lambda i, j: (i % tab_tiles, 0))
    return pl.pallas_call(
        _proj_kernel,
        grid=(rows // tm, n // tn),
        in_specs=[
            pl.BlockSpec((tm, d), lambda i, j: (i, 0)),
            pl.BlockSpec((1, d), lambda i, j: (0, 0)),
            pl.BlockSpec((d, tn), lambda i, j: (0, j)),
            tab, tab, tab,
        ],
        out_specs=pl.BlockSpec((tm, tn), lambda i, j: (i, j)),
        out_shape=jax.ShapeDtypeStruct((rows, n), F32),
        scratch_shapes=[pltpu.VMEM((tm, d), BF16)],
        compiler_params=_cparams(("parallel", "arbitrary")),
        name="proj",
    )(x2d, g.reshape(1, d), w_all, *tables)


def _mlstm_prompt_kernel(q_ref, k_ref, v_ref, o_ref, g_ref, bias_ref,
                         hm_ref, c_out, n_out, m_out, ct_s, n_s, m_s):
    c = pl.program_id(1)
    last = pl.num_programs(1) - 1
    L = CHUNK

    @pl.when(c == 0)
    def _():
        ct_s[...] = jnp.zeros_like(ct_s)
        n_s[...] = jnp.zeros_like(n_s)
        m_s[...] = jnp.zeros_like(m_s)

    gates = g_ref[...] + bias_ref[...]
    row = lax.broadcasted_iota(jnp.int32, (L, LANES), 0)
    bcum = _log_sigmoid(gates)
    for d in (1, 2, 4, 8, 16, 32):
        bcum = bcum + jnp.where(row >= d, pltpu.roll(bcum, d, axis=0), 0.0)

    ri = lax.broadcasted_iota(jnp.int32, (L, L), 0)
    ci = lax.broadcasted_iota(jnp.int32, (L, L), 1)
    eye = ri == ci
    causal = ci <= ri

    for h in range(N_MH):
        q = q_ref[:, h * DK:(h + 1) * DK]
        k = k_ref[:, h * DK:(h + 1) * DK] * (DK ** -0.5)
        v = v_ref[:, h * DV:(h + 1) * DV]
        o = o_ref[:, h * DV:(h + 1) * DV]
        b_col = bcum[:, N_MH + h:N_MH + h + 1]
        ig_col = gates[:, h:h + 1]
        a_col = ig_col - b_col
        a_lane = jnp.sum(jnp.where(eye, a_col, 0.0), axis=0, keepdims=True)
        dmat = jnp.where(causal, b_col + a_lane, -jnp.inf)
        m_prev = m_s[h:h + 1, 0:1]
        inter = b_col + m_prev
        m_t = jnp.maximum(jnp.max(dmat, axis=1, keepdims=True), inter)
        qb = q.astype(BF16)
        kb = k.astype(BF16)
        vb = v.astype(BF16)
        s = _dot_nt(qb, kb) * jnp.exp(dmat - m_t)
        w_inter = jnp.exp(inter - m_t)
        ct = ct_s[h]
        num = _dot(s.astype(BF16), vb) + w_inter * _dot(qb, ct.astype(BF16))
        n_row = n_s[h:h + 1, :]
        den = jnp.sum(s, axis=1, keepdims=True) + w_inter * jnp.sum(q * n_row, axis=1, keepdims=True)
        hout = num / jnp.maximum(jnp.abs(den), jnp.exp(-m_t))
        hm_ref[:, h * DV:(h + 1) * DV] = (_sigmoid(o) * hout).astype(BF16)

        m_new = m_t[L - 1:L, :]
        b_last = b_col[L - 1:L, :]
        w_last = jnp.exp(b_last - b_col + ig_col - m_new)
        decay = jnp.exp(b_last + m_prev - m_new)
        kw = k * w_last
        ct_new = decay * ct + _dot_tn(kw.astype(BF16), vb)
        n_new = decay * n_row + jnp.sum(kw, axis=0, keepdims=True)
        ct_s[h] = ct_new
        n_s[h:h + 1, :] = n_new
        m_s[h:h + 1, :] = jnp.broadcast_to(m_new, (1, LANES))

        @pl.when(c == last)
        def _():
            c_out[0, h] = ct_new.T
            n_out[0, h:h + 1, :] = n_new

    @pl.when(c == last)
    def _():
        m_out[0] = m_s[...]


def _mlstm_prompt(z, bias_row, batch, seq):
    rows = z.shape[0]
    nc = seq // CHUNK

    def zspec(width, off):
        blk = off // width
        return pl.BlockSpec((CHUNK, width), lambda b, c: (b * nc + c, blk))

    return pl.pallas_call(
        _mlstm_prompt_kernel,
        grid=(batch, nc),
        in_specs=[
            zspec(QK_W, _Z_QM), zspec(QK_W, _Z_KM), zspec(V_W, _Z_VM), zspec(V_W, _Z_OM),
            zspec(LANES, _Z_IF),
            pl.BlockSpec((1, LANES), lambda b, c: (0, 0)),
        ],
        out_specs=[
            pl.BlockSpec((CHUNK, V_W), lambda b, c: (b * nc + c, 0)),
            pl.BlockSpec((1, N_MH, DV, DK), lambda b, c: (b, 0, 0, 0)),
            pl.BlockSpec((1, N_MH, DK), lambda b, c: (b, 0, 0)),
            pl.BlockSpec((1, 8, LANES), lambda b, c: (b, 0, 0)),
        ],
        out_shape=[
            jax.ShapeDtypeStruct((rows, V_W), BF16),
            jax.ShapeDtypeStruct((batch, N_MH, DV, DK), F32),
            jax.ShapeDtypeStruct((batch, N_MH, DK), F32),
            jax.ShapeDtypeStruct((batch, 8, LANES), F32),
        ],
        scratch_shapes=[
            pltpu.VMEM((N_MH, DK, DV), F32),
            pltpu.VMEM((8, DK), F32),
            pltpu.VMEM((8, LANES), F32),
        ],
        compiler_params=_cparams(("parallel", "arbitrary")),
        name="mlstm_prompt",
    )(z, z, z, z, z, bias_row)


def _mlstm_sample_kernel(q_ref, k_ref, v_ref, o_ref, g_ref, bias_ref, c_ref, n_ref, m_ref,
                         hm_ref, c_out, n_out, m_out):
    gates = g_ref[0] + bias_ref[...]
    lf_all = _log_sigmoid(gates)
    m_all = m_ref[0]
    ri = lax.broadcasted_iota(jnp.int32, (DV, DV), 0)
    ci = lax.broadcasted_iota(jnp.int32, (DV, DV), 1)
    eye = ri == ci
    m_new_all = jnp.zeros((1, LANES), F32)
    lane = lax.broadcasted_iota(jnp.int32, (1, LANES), 1)

    for h in range(N_MH):
        q = q_ref[0, :, h * DK:(h + 1) * DK]
        k = k_ref[0, :, h * DK:(h + 1) * DK] * (DK ** -0.5)
        v = v_ref[0, :, h * DV:(h + 1) * DV]
        o = o_ref[0, :, h * DV:(h + 1) * DV]
        ig = gates[:, h:h + 1]
        lf = lf_all[:, N_MH + h:N_MH + h + 1]
        m_prev = m_all[:, h:h + 1]
        inter = lf + m_prev
        m_t = jnp.maximum(ig, inter)
        s = jnp.sum(q * k, axis=1, keepdims=True) * jnp.exp(ig - m_t)
        w_inter = jnp.exp(inter - m_t)
        cmat = c_ref[0, 0, h]
        q8 = jnp.broadcast_to(q, (8, DK)).astype(BF16)
        cq = _dot_nt(q8, cmat.astype(BF16))[0:1, :]
        n_row = n_ref[0, 0, h:h + 1, :]
        num = s * v + w_inter * cq
        den = s + w_inter * jnp.sum(n_row * q, axis=1, keepdims=True)
        hout = num / jnp.maximum(jnp.abs(den), jnp.exp(-m_t))
        hm_ref[0, :, h * DV:(h + 1) * DV] = (_sigmoid(o) * hout).astype(BF16)

        w_last = jnp.exp(ig - m_t)
        decay = jnp.exp(inter - m_t)
        vw = v * w_last
        vw_col = jnp.sum(jnp.where(eye, vw, 0.0), axis=1, keepdims=True)
        c_out[0, 0, h] = decay * cmat + vw_col * k
        n_out[0, 0, h:h + 1, :] = decay * n_row + w_last * k
        m_new_all = jnp.where(lane == h, m_t, m_new_all)

    m_out[0] = m_new_all


def _mlstm_sample(z3, bias_row, c0, n0, m0_pad, layer):
    nb = z3.shape[0]

    def zspec(width, off):
        blk = off // width
        return pl.BlockSpec((1, 1, width), lambda b: (b, 0, blk))

    c_in = pl.BlockSpec((1, 1, N_MH, DV, DK), lambda b: (layer, b, 0, 0, 0))
    n_in = pl.BlockSpec((1, 1, N_MH, DK), lambda b: (layer, b, 0, 0))
    c_blk = pl.BlockSpec((1, 1, N_MH, DV, DK), lambda b: (0, b, 0, 0, 0))
    n_blk = pl.BlockSpec((1, 1, N_MH, DK), lambda b: (0, b, 0, 0))

    return pl.pallas_call(
        _mlstm_sample_kernel,
        grid=(nb,),
        in_specs=[
            zspec(QK_W, _Z_QM), zspec(QK_W, _Z_KM), zspec(V_W, _Z_VM), zspec(V_W, _Z_OM),
            zspec(LANES, _Z_IF),
            pl.BlockSpec((1, LANES), lambda b: (0, 0)),
            c_in,
            n_in,
            pl.BlockSpec((1, 1, LANES), lambda b: (b, 0, 0)),
        ],
        out_specs=[
            pl.BlockSpec((1, 1, V_W), lambda b: (b, 0, 0)),
            c_blk,
            n_blk,
            pl.BlockSpec((1, 1, LANES), lambda b: (b, 0, 0)),
        ],
        out_shape=[
            jax.ShapeDtypeStruct((nb, 1, V_W), BF16),
            jax.ShapeDtypeStruct((1, nb, N_MH, DV, DK), F32),
            jax.ShapeDtypeStruct((1, nb, N_MH, DK), F32),
            jax.ShapeDtypeStruct((nb, 1, LANES), F32),
        ],
        compiler_params=_cparams(("parallel",)),
        name="mlstm_sample",
    )(z3, z3, z3, z3, z3, bias_row, c0, n0, m0_pad)


def _rope_tables(pos):
    half = ROT // 2
    inv = THETA ** (-jnp.arange(half, dtype=F32) * 2.0 / ROT)
    ang = pos.astype(F32)[:, None] * inv[None, :]
    cos = jnp.cos(ang)
    sin = jnp.sin(ang)
    t = pos.shape[0]
    cos_t = jnp.concatenate([cos, cos, jnp.ones((t, HD - ROT), F32)], axis=1)
    sin_a = jnp.concatenate([-sin, jnp.zeros((t, HD - half), F32)], axis=1)
    sin_b = jnp.concatenate([jnp.zeros((t, half), F32), sin, jnp.zeros((t, HD - ROT), F32)], axis=1)
    rep = LANES // HD
    return tuple(jnp.tile(a, (1, rep)) for a in (cos_t, sin_a, sin_b))


def _rope(x, cos_t, sin_a, sin_b):
    w = x.shape[1]
    rep = w // LANES
    half = ROT // 2
    if rep > 1:
        cos_t = jnp.concatenate([cos_t] * rep, axis=1)
        sin_a = jnp.concatenate([sin_a] * rep, axis=1)
        sin_b = jnp.concatenate([sin_b] * rep, axis=1)
    up = pltpu.roll(x, w - half, axis=1)
    down = pltpu.roll(x, half, axis=1)
    return x * cos_t + up * sin_a + down * sin_b


def _swa_prompt_kernel(sink_ref, q_ref, kc_ref, kp_ref, vc_ref, vp_ref, ha_ref):
    i = pl.program_id(1)
    w = WINDOW
    kk = jnp.concatenate([kp_ref[...], kc_ref[...]], axis=0).astype(BF16)
    vv = jnp.concatenate([vp_ref[...], vc_ref[...]], axis=0).astype(BF16)
    a_idx = lax.broadcasted_iota(jnp.int32, (w, 2 * w), 0)
    s_idx = lax.broadcasted_iota(jnp.int32, (w, 2 * w), 1)
    valid = (s_idx >= a_idx) & (s_idx <= a_idx + w) & ((i > 0) | (s_idx >= w))
    outs = []
    for g in range(N_KV):
        kg = kk[:, g * HD:(g + 1) * HD]
        vg = vv[:, g * HD:(g + 1) * HD]
        qg = jnp.concatenate(
            [q_ref[:, (g * GROUP + hh) * HD:(g * GROUP + hh + 1) * HD] for hh in range(GROUP)], axis=0)
        s = _dot_nt(qg.astype(BF16), kg).reshape(GROUP, w, 2 * w)
        s = jnp.where(valid[None], s, -jnp.inf)
        sink = sink_ref[g * GROUP:(g + 1) * GROUP]
        m = jnp.maximum(jnp.max(s, axis=2, keepdims=True), sink)
        e = jnp.exp(s - m)
        denom = jnp.sum(e, axis=2, keepdims=True) + jnp.exp(sink - m)
        o = _dot(e.reshape(GROUP * w, 2 * w).astype(BF16), vg).reshape(GROUP, w, HD) * (1.0 / denom)
        outs += [o[hh] for hh in range(GROUP)]
    ha_ref[...] = jnp.concatenate(outs, axis=1).astype(BF16)


def _swa_prompt(z, sinks3, batch, seq):
    rows = z.shape[0]
    nb = seq // WINDOW
    w = WINDOW

    def cur(width, off):
        blk = off // width
        return pl.BlockSpec((w, width), lambda b, i: (b * nb + i, blk))

    def prev(width, off):
        blk = off // width
        return pl.BlockSpec((w, width), lambda b, i: (b * nb + jnp.maximum(i - 1, 0), blk))

    return pl.pallas_call(
        _swa_prompt_kernel,
        grid=(batch, nb),
        in_specs=[
            pl.BlockSpec((N_Q, 1, 1), lambda b, i: (0, 0, 0)),
            cur(ATT_Q, _Z_QA), cur(ATT_KV, _Z_KA), prev(ATT_KV, _Z_KA),
            cur(ATT_KV, _Z_VA), prev(ATT_KV, _Z_VA),
        ],
        out_specs=pl.BlockSpec((w, ATT_Q), lambda b, i: (b * nb + i, 0)),
        out_shape=jax.ShapeDtypeStruct((rows, ATT_Q), BF16),
        compiler_params=_cparams(("parallel", "parallel")),
        name="swa_prompt",
    )(sinks3, z, z, z, z, z)


def _swa_sample_kernel(q_ref, kn_ref, vn_ref, kp_ref, vp_ref, sink_ref,
                       o_ref, ko_ref, vo_ref, *, bb):
    w = WINDOW
    hrow = lax.broadcasted_iota(jnp.int32, (N_Q, ATT_KV), 0) // GROUP
    glane = lax.broadcasted_iota(jnp.int32, (N_Q, ATT_KV), 1) // HD
    own = hrow == glane
    srow = lax.broadcasted_iota(jnp.int32, (w, ATT_KV), 0)
    sink = sink_ref[...]
    for b in range(bb):
        q = q_ref[b]
        qm = jnp.where(own, jnp.concatenate([q] * N_KV, axis=1), 0.0)
        k_past = kp_ref[b]
        v_past = vp_ref[b]
        k_new = kn_ref[b]
        v_new = vn_ref[b]
        s_past = _dot_nt(qm.astype(BF16), k_past.astype(BF16))
        s_new = jnp.sum(qm * k_new, axis=1, keepdims=True)
        m = jnp.maximum(jnp.maximum(jnp.max(s_past, axis=1, keepdims=True), s_new), sink)
        e_past = jnp.exp(s_past - m)
        e_new = jnp.exp(s_new - m)
        denom = jnp.sum(e_past, axis=1, keepdims=True) + e_new + jnp.exp(sink - m)
        r = 1.0 / denom
        pv = _dot((e_past * r).astype(BF16), v_past.astype(BF16)) + (e_new * r) * v_new
        pv = jnp.where(own, pv, 0.0)
        o = pv[:, 0:HD]
        for g in range(1, N_KV):
            o = o + pv[:, g * HD:(g + 1) * HD]
        o_ref[b] = o
        ko_ref[b] = jnp.where(srow == w - 1, k_new, pltpu.roll(k_past, w - 1, axis=0))
        vo_ref[b] = jnp.where(srow == w - 1, v_new, pltpu.roll(v_past, w - 1, axis=0))


def _swa_sample(q3, k_new3, v_new3, k_past, v_past, sink_col, bb):
    nb = q3.shape[0]
    w = WINDOW
    return pl.pallas_call(
        functools.partial(_swa_sample_kernel, bb=bb),
        grid=(nb // bb,),
        in_specs=[
            pl.BlockSpec((bb, N_Q, HD), lambda i: (i, 0, 0)),
            pl.BlockSpec((bb, 1, ATT_KV), lambda i: (i, 0, _Z_KA // ATT_KV)),
            pl.BlockSpec((bb, 1, ATT_KV), lambda i: (i, 0, _Z_VA // ATT_KV)),
            pl.BlockSpec((bb, w, ATT_KV), lambda i: (i, 0, 0)),
            pl.BlockSpec((bb, w, ATT_KV), lambda i: (i, 0, 0)),
            pl.BlockSpec((N_Q, 1), lambda i: (0, 0)),
        ],
        out_specs=[
            pl.BlockSpec((bb, N_Q, HD), lambda i: (i, 0, 0)),
            pl.BlockSpec((bb, w, ATT_KV), lambda i: (i, 0, 0)),
            pl.BlockSpec((bb, w, ATT_KV), lambda i: (i, 0, 0)),
        ],
        out_shape=[
            jax.ShapeDtypeStruct((nb, N_Q, HD), F32),
            jax.ShapeDtypeStruct((nb, w, ATT_KV), F32),
            jax.ShapeDtypeStruct((nb, w, ATT_KV), F32),
        ],
        compiler_params=_cparams(("parallel",)),
        name="swa_sample",
    )(q3, k_new3, v_new3, k_past, v_past, sink_col)


def _merge1_kernel(hm_ref, ha_ref, ga_ref, gb_ref, wa_ref, wb_ref, t_ref):
    a = _dot(hm_ref[...], wa_ref[...])
    b = _dot(ha_ref[...], wb_ref[...])
    t_ref[...] = (_sigmoid(ga_ref[...]) * a + _sigmoid(gb_ref[...]) * b).astype(BF16)


def _merge1(hm, ha, z, wa, wb, tm, tn):
    rows, d = hm.shape
    n = wa.shape[1]
    return pl.pallas_call(
        _merge1_kernel,
        grid=(rows // tm, n // tn),
        in_specs=[
            pl.BlockSpec((tm, d), lambda i, j: (i, 0)),
            pl.BlockSpec((tm, d), lambda i, j: (i, 0)),
            pl.BlockSpec((tm, tn), lambda i, j: (i, _Z_GA // tn + j)),
            pl.BlockSpec((tm, tn), lambda i, j: (i, _Z_GB // tn + j)),
            pl.BlockSpec((d, tn), lambda i, j: (0, j)),
            pl.BlockSpec((d, tn), lambda i, j: (0, j)),
        ],
        out_specs=pl.BlockSpec((tm, tn), lambda i, j: (i, j)),
        out_shape=jax.ShapeDtypeStruct((rows, n), BF16),
        compiler_params=_cparams(("parallel", "arbitrary")),
        name="merge1",
    )(hm, ha, z, z, wa, wb)


def _merge2_kernel(t_ref, w_ref, x_ref, g1_ref, g2_ref, x1_ref, h2_ref):
    mix = _dot(t_ref[...], w_ref[...])
    x1 = x_ref[...] + _rms(mix, g1_ref[...])
    x1_ref[...] = x1
    h2_ref[...] = _rms(x1, g2_ref[...]).astype(BF16)


def _merge2(t, w_out, x2d, g1, g2, tm):
    rows, d = x2d.shape
    row = lambda i: (i, 0)
    const = lambda i: (0, 0)
    return pl.pallas_call(
        _merge2_kernel,
        grid=(rows // tm,),
        in_specs=[
            pl.BlockSpec((tm, d), row),
            pl.BlockSpec((d, d), const, pipeline_mode=pl.Buffered(1)),
            pl.BlockSpec((tm, d), row),
            pl.BlockSpec((1, d), const),
            pl.BlockSpec((1, d), const),
        ],
        out_specs=[pl.BlockSpec((tm, d), row), pl.BlockSpec((tm, d), row)],
        out_shape=[jax.ShapeDtypeStruct((rows, d), F32), jax.ShapeDtypeStruct((rows, d), BF16)],
        compiler_params=_cparams(("parallel",)),
        name="merge2",
    )(t, w_out, x2d, g1.reshape(1, d), g2.reshape(1, d))


def _gelu_tanh(x):
    return 0.5 * x * (1.0 + jnp.tanh(0.7978845608028654 * (x + 0.044715 * (x * x * x))))


def _conv_taps(u, u1, u2, cw_ref, cb_ref):
    return cb_ref[...] + u2 * cw_ref[0:1, :] + u1 * cw_ref[1:2, :] + u * cw_ref[2:3, :]


def _ffn_tail(j, nf, y, wd_ref, x1_ref, g_ref, y_ref, acc_ref):
    contrib = _dot(y, wd_ref[...])

    @pl.when(j == 0)
    def _():
        acc_ref[...] = contrib

    @pl.when(j > 0)
    def _():
        acc_ref[...] += contrib

    @pl.when(j == nf - 1)
    def _():
        y_ref[...] = x1_ref[...] + _rms(acc_ref[...], g_ref[...])


def _ffn_prompt_kernel(h_ref, wg_ref, wv_ref, cwg_ref, cwv_ref, cbg_ref, cbv_ref, wd_ref,
                       x1_ref, g_ref, y_ref, sg_ref, sv_ref, acc_ref, carry_ref,
                       *, tiles_per_seq):
    i = pl.program_id(0)
    j = pl.program_id(1)
    nf = pl.num_programs(1)
    tm = h_ref.shape[0]

    @pl.when((i == 0) & (j == 0))
    def _():
        carry_ref[...] = jnp.zeros_like(carry_ref)

    seq_start = (i % tiles_per_seq) == 0
    h = h_ref[...]
    row = lax.broadcasted_iota(jnp.int32, (tm, wg_ref.shape[1]), 0)

    def branch(w_ref, cw_ref, cb_ref, slot, state_ref):
        u = _dot(h, w_ref[...])
        prev = jnp.where(seq_start, 0.0, carry_ref[slot])
        p2 = prev[6:7, :]
        p1 = prev[7:8, :]
        u1 = jnp.where(row == 0, p1, pltpu.roll(u, 1, axis=0))
        u2 = jnp.where(row == 0, p2, jnp.where(row == 1, p1, pltpu.roll(u, 2, axis=0)))
        tail = u[tm - 8:tm, :]
        carry_ref[slot] = tail
        state_ref[0] = tail
        return _conv_taps(u, u1, u2, cw_ref, cb_ref)

    cg = branch(wg_ref, cwg_ref, cbg_ref, 2 * j, sg_ref)
    cv = branch(wv_ref, cwv_ref, cbv_ref, 2 * j + 1, sv_ref)
    y = (_gelu_tanh(cg) * cv).astype(BF16)
    _ffn_tail(j, nf, y, wd_ref, x1_ref, g_ref, y_ref, acc_ref)


def _ffn_sample_kernel(h_ref, wg_ref, wv_ref, cwg_ref, cwv_ref, cbg_ref, cbv_ref, wd_ref,
                       x1_ref, g_ref, p2g_ref, p2v_ref, p1g_ref, p1v_ref,
                       y_ref, ug_ref, uv_ref, acc_ref):
    j = pl.program_id(1)
    nf = pl.num_programs(1)
    h = h_ref[...]
    ug = _dot(h, wg_ref[...])
    uv = _dot(h, wv_ref[...])
    ug_ref[...] = ug
    uv_ref[...] = uv
    cg = _conv_taps(ug, p1g_ref[...], p2g_ref[...], cwg_ref, cbg_ref)
    cv = _conv_taps(uv, p1v_ref[...], p2v_ref[...], cwv_ref, cbv_ref)
    y = (_gelu_tanh(cg) * cv).astype(BF16)
    _ffn_tail(j, nf, y, wd_ref, x1_ref, g_ref, y_ref, acc_ref)


def _ffn(h2, x1, w_up, w_conv, b_conv2d, w_down, g, tm, tf, seq=None, past=None):
    rows, d = h2.shape
    f = w_down.shape[0]
    nf = f // tf
    row = lambda i, j: (i, 0)
    gcol = lambda i, j: (0, j)
    vcol = lambda i, j: (0, nf + j)
    in_specs = [
        pl.BlockSpec((tm, d), row),
        pl.BlockSpec((d, tf), gcol), pl.BlockSpec((d, tf), vcol),
        pl.BlockSpec((3, tf), gcol), pl.BlockSpec((3, tf), vcol),
        pl.BlockSpec((1, tf), gcol), pl.BlockSpec((1, tf), vcol),
        pl.BlockSpec((tf, d), lambda i, j: (j, 0)),
        pl.BlockSpec((tm, d), row),
        pl.BlockSpec((1, d), lambda i, j: (0, 0)),
    ]
    args = [h2, w_up, w_up, w_conv, w_conv, b_conv2d, b_conv2d, w_down, x1, g.reshape(1, d)]
    scratch = [pltpu.VMEM((tm, d), F32)]
    if past is None:
        tiles_per_seq = seq // tm
        kern = functools.partial(_ffn_prompt_kernel, tiles_per_seq=tiles_per_seq)
        state_spec = pl.BlockSpec((1, 8, tf), lambda i, j: (i, 0, j))
        out_specs = [pl.BlockSpec((tm, d), row), state_spec, state_spec]
        out_shape = [jax.ShapeDtypeStruct((rows, d), F32),
                     jax.ShapeDtypeStruct((rows // tm, 8, f), F32),
                     jax.ShapeDtypeStruct((rows // tm, 8, f), F32)]
        scratch.append(pltpu.VMEM((2 * nf, 8, tf), F32))
        name = "ffn_prompt"
    else:
        p2, p1 = past
        kern = _ffn_sample_kernel
        ucol = pl.BlockSpec((tm, tf), lambda i, j: (i, j))
        in_specs += [
            pl.BlockSpec((tm, tf), lambda i, j: (i, j)), pl.BlockSpec((tm, tf), lambda i, j: (i, nf + j)),
            pl.BlockSpec((tm, tf), lambda i, j: (i, j)), pl.BlockSpec((tm, tf), lambda i, j: (i, nf + j)),
        ]
        args += [p2, p2, p1, p1]
        out_specs = [pl.BlockSpec((tm, d), row), ucol, ucol]
        out_shape = [jax.ShapeDtypeStruct((rows, d), F32),
                     jax.ShapeDtypeStruct((rows, f), F32),
                     jax.ShapeDtypeStruct((rows, f), F32)]
        name = "ffn_sample"
    return pl.pallas_call(
        kern,
        grid=(rows // tm, nf),
        in_specs=in_specs,
        out_specs=out_specs,
        out_shape=out_shape,
        scratch_shapes=scratch,
        compiler_params=_cparams(("arbitrary", "arbitrary")),
        name=name,
    )(*args)


def _regroup_w_in(w_in):
    d = w_in.shape[0]
    o_if = QK_W * 2 + V_W * 2
    o_qa = o_if + 2 * N_MH
    o_ka = o_qa + ATT_Q
    o_ga = o_ka + 2 * ATT_KV
    parts = [
        w_in[:, :o_if],
        w_in[:, o_qa:o_ka],
        w_in[:, o_ga:],
        w_in[:, o_ka:o_ga],
        w_in[:, o_if:o_qa],
    ]
    used = sum(p.shape[1] for p in parts)
    parts.append(jnp.zeros((d, _Z_W - used), w_in.dtype))
    return jnp.concatenate(parts, axis=1).astype(BF16)


def _pick(rows, pref):
    t = min(rows, pref)
    while rows % t:
        t //= 2
    return t


def kernel(x_prompt, x_sample, state_mlstm_c, state_mlstm_n, state_mlstm_m, cache_swa_k, cache_swa_v,
           state_ffn_conv, g_pre_mix, w_in, b_if, attn_sinks, w_branch_a, w_branch_b, w_out, g_post_mix,
           g_pre_ffn, w_up, w_conv, b_conv, w_down, g_post_ffn):
    bp, tp, d = x_prompt.shape
    bs, ts, _ = x_sample.shape
    depth = w_in.shape[0]
    assert depth == 1 and ts == 1 and d == D_MODEL
    f = w_down.shape[1]
    tf = 512
    assert f % tf == 0

    l = 0
    w_all = _regroup_w_in(w_in[l])
    wa = w_branch_a[l].astype(BF16)
    wb = w_branch_b[l].astype(BF16)
    wo = w_out[l].astype(BF16)
    wu = w_up[l].astype(BF16)
    wd = w_down[l].astype(BF16)
    bias_row = jnp.concatenate([b_if[l].astype(F32), jnp.zeros((LANES - 2 * N_MH,), F32)]).reshape(1, LANES)
    sinks = attn_sinks[l].astype(F32)
    b_conv2d = b_conv[l].reshape(1, 2 * f)

    xp = x_prompt.reshape(bp * tp, d)
    zp = _project(xp, g_pre_mix[l], w_all, _rope_tables(jnp.arange(tp)), _pick(tp, 1024))
    hm_p, c_p, n_p, m_p = _mlstm_prompt(zp, bias_row, bp, tp)
    ha_p = _swa_prompt(zp, sinks.reshape(N_Q, 1, 1), bp, tp)
    t_p = _merge1(hm_p, ha_p, zp, wa, wb, _pick(bp * tp, 1024), 512)
    x1_p, h2_p = _merge2(t_p, wo, xp, g_post_mix[l], g_pre_ffn[l], _pick(bp * tp, 512))
    tm_ffn = _pick(tp, 512)
    y_p, sg_p, sv_p = _ffn(h2_p, x1_p, wu, w_conv[l], b_conv2d, wd, g_post_ffn[l],
                           tm_ffn, tf, seq=tp)
    seq_end = slice(tp // tm_ffn - 1, None, tp // tm_ffn)

    y_prompt = y_p.reshape(bp, tp, d)
    c_prompt = c_p[None]
    n_prompt = n_p[None]
    m_prompt = m_p[:, :N_MH, 0][None]
    kv_tail = zp.reshape(bp, tp, _Z_W)[:, tp - WINDOW:, _Z_KA:_Z_KA + 2 * ATT_KV]
    k_prompt = kv_tail[..., :ATT_KV].reshape(bp, WINDOW, N_KV, HD)[None]
    v_prompt = kv_tail[..., ATT_KV:].reshape(bp, WINDOW, N_KV, HD)[None]
    conv_prompt = jnp.concatenate([sg_p[seq_end, 6:8, :], sv_p[seq_end, 6:8, :]], axis=-1)[None]

    xs = x_sample.reshape(bs, d)
    tabs_s = tuple(jnp.broadcast_to(a, (bs, LANES)) for a in _rope_tables(PAST_LEN + jnp.arange(ts)))
    zs = _project(xs, g_pre_mix[l], w_all, tabs_s, bs)
    zs3 = zs.reshape(bs, 1, _Z_W)
    m0_pad = jnp.pad(state_mlstm_m[l].astype(F32), ((0, 0), (0, LANES - N_MH))).reshape(bs, 1, LANES)
    hm_s3, c_sample, n_sample, m_s3 = _mlstm_sample(zs3, bias_row, state_mlstm_c, state_mlstm_n, m0_pad, l)
    o_s3, k_s, v_s = _swa_sample(
        zs[:, _Z_QA:_Z_QA + ATT_Q].reshape(bs, N_Q, HD), zs3, zs3,
        cache_swa_k[l].reshape(bs, WINDOW, ATT_KV), cache_swa_v[l].reshape(bs, WINDOW, ATT_KV),
        sinks.reshape(N_Q, 1), _pick(bs, 8))
    ha_s = o_s3.reshape(bs, ATT_Q).astype(BF16)
    t_s = _merge1(hm_s3.reshape(bs, V_W), ha_s, zs, wa, wb, bs, 512)
    x1_s, h2_s = _merge2(t_s, wo, xs, g_post_mix[l], g_pre_ffn[l], bs)
    past = state_ffn_conv[l].astype(F32)
    y_s, ug_s, uv_s = _ffn(h2_s, x1_s, wu, w_conv[l], b_conv2d, wd, g_post_ffn[l],
                           bs, tf, past=(past[:, 0, :], past[:, 1, :]))

    y_sample = y_s.reshape(bs, ts, d)
    m_sample = m_s3[:, 0, :N_MH][None]
    k_sample = k_s.reshape(bs, WINDOW, N_KV, HD)[None]
    v_sample = v_s.reshape(bs, WINDOW, N_KV, HD)[None]
    conv_sample = jnp.stack([past[:, 1, :], jnp.concatenate([ug_s, uv_s], axis=-1)], axis=1)[None]

    return (y_prompt, y_sample,
            c_prompt, n_prompt, m_prompt, k_prompt, v_prompt, conv_prompt,
            c_sample, n_sample, m_sample, k_sample, v_sample, conv_sample)
```

```python
import functools

import jax
import jax.numpy as jnp
from jax import lax
from jax.experimental import pallas as pl
from jax.experimental.pallas import tpu as pltpu

F32 = jnp.float32
BF16 = jnp.bfloat16

N_MH = 4
DK = 256
DV = 512
CHUNK = 64
HD = 64
N_Q = 32
N_KV = 4
GROUP = N_Q // N_KV
WINDOW = 128
ROT = 16
THETA = 500000.0
PAST_LEN = 16384
EPS = 1e-6
D_MODEL = 2048
QK_W = N_MH * DK
V_W = N_MH * DV
ATT_Q = N_Q * HD
ATT_KV = N_KV * HD

_Z_QM = 0
_Z_KM = _Z_QM + QK_W
_Z_VM = _Z_KM + QK_W
_Z_OM = _Z_VM + V_W
_Z_QA = _Z_OM + V_W
_Z_GA = _Z_QA + ATT_Q
_Z_GB = _Z_GA + D_MODEL
_Z_KA = _Z_GB + D_MODEL
_Z_VA = _Z_KA + ATT_KV
_Z_IF = _Z_VA + ATT_KV
LANES = 128
_PROJ_TN = 1024
_Z_W = -(-(_Z_IF + LANES) // _PROJ_TN) * _PROJ_TN
assert _Z_QA % _PROJ_TN == 0 and _Z_GA % _PROJ_TN == 0 and _Z_KA % _PROJ_TN == 0

_VMEM_LIMIT = 56 * 1024 * 1024


def _cparams(sem):
    return pltpu.CompilerParams(dimension_semantics=sem, vmem_limit_bytes=_VMEM_LIMIT)


def _rms(x, g):
    return x * lax.rsqrt(jnp.mean(x * x, axis=-1, keepdims=True) + EPS) * g


def _sigmoid(x):
    return 1.0 / (1.0 + jnp.exp(-x))


def _log_sigmoid(x):
    return jnp.minimum(x, 0.0) - jnp.log1p(jnp.exp(-jnp.abs(x)))


def _dot(a, b):
    return jnp.dot(a, b, preferred_element_type=F32)


def _dot_nt(a, b):
    return lax.dot_general(a, b, (((1,), (1,)), ((), ())), preferred_element_type=F32)


def _dot_tn(a, b):
    return lax.dot_general(a, b, (((0,), (0,)), ((), ())), preferred_element_type=F32)


def _proj_kernel(x_ref, g_ref, w_ref, cos_ref, sa_ref, sb_ref, z_ref, h_ref):
    j = pl.program_id(1)
    jq0 = _Z_QA // _PROJ_TN
    jq1 = _Z_GA // _PROJ_TN
    jk = _Z_KA // _PROJ_TN

    @pl.when(j == 0)
    def _():
        h_ref[...] = _rms(x_ref[...], g_ref[...]).astype(BF16)

    is_q = (j >= jq0) & (j < jq1)
    is_k = j == jk

    @pl.when(is_q)
    def _():
        acc = _dot(h_ref[...], w_ref[...])
        z_ref[...] = _rope(acc, cos_ref[...], sa_ref[...], sb_ref[...]) * (HD ** -0.5)

    @pl.when(is_k)
    def _():
        acc = _dot(h_ref[...], w_ref[...])
        z_ref[...] = acc
        z_ref[:, :ATT_KV] = _rope(acc[:, :ATT_KV], cos_ref[...], sa_ref[...], sb_ref[...])

    @pl.when(jnp.logical_not(is_q | is_k))
    def _():
        z_ref[...] = _dot(h_ref[...], w_ref[...])


def _project(x2d, g, w_all, tables, tm):
    rows, d = x2d.shape
    n = w_all.shape[1]
    tn = _PROJ_TN
    tab_tiles = tables[0].shape[0] // tm
    tab = pl.BlockSpec((tm, LANES), lambda i, j: (i % tab_tiles, 0))
    return pl.pallas_call(
        _proj_kernel,
        grid=(rows // tm, n // tn),
        in_specs=[
            pl.BlockSpec((tm, d), lambda i, j: (i, 0)),
            pl.BlockSpec((1, d), lambda i, j: (0, 0)),
            pl.BlockSpec((d, tn), lambda i, j: (0, j)),
            tab, tab, tab,
        ],
        out_specs=pl.BlockSpec((tm, tn), lambda i, j: (i, j)),
        out_shape=jax.ShapeDtypeStruct((rows, n), F32),
        scratch_shapes=[pltpu.VMEM((tm, d), BF16)],
        compiler_params=_cparams(("parallel", "arbitrary")),
        name="proj",
    )(x2d, g.reshape(1, d), w_all, *tables)


def _mlstm_prompt_kernel(q_ref, k_ref, v_ref, o_ref, g_ref, bias_ref,
                         hm_ref, c_out, n_out, m_out, ct_s, n_s, m_s):
    c = pl.program_id(1)
    last = pl.num_programs(1) - 1
    L = CHUNK
    nseq = q_ref.shape[0]

    @pl.when(c == 0)
    def _():
        ct_s[...] = jnp.zeros_like(ct_s)
        n_s[...] = jnp.zeros_like(n_s)
        m_s[...] = jnp.zeros_like(m_s)

    row = lax.broadcasted_iota(jnp.int32, (L, LANES), 0)
    ri = lax.broadcasted_iota(jnp.int32, (L, L), 0)
    ci = lax.broadcasted_iota(jnp.int32, (L, L), 1)
    eye = ri == ci
    causal = ci <= ri

    for bi in range(nseq):
        gates = g_ref[bi] + bias_ref[...]
        bcum = _log_sigmoid(gates)
        for d in (1, 2, 4, 8, 16, 32):
            bcum = bcum + jnp.where(row >= d, pltpu.roll(bcum, d, axis=0), 0.0)

        for h in range(N_MH):
            q = q_ref[bi, :, h * DK:(h + 1) * DK]
            k = k_ref[bi, :, h * DK:(h + 1) * DK] * (DK ** -0.5)
            v = v_ref[bi, :, h * DV:(h + 1) * DV]
            o = o_ref[bi, :, h * DV:(h + 1) * DV]
            b_col = bcum[:, N_MH + h:N_MH + h + 1]
            ig_col = gates[:, h:h + 1]
            a_col = ig_col - b_col
            a_lane = jnp.sum(jnp.where(eye, a_col, 0.0), axis=0, keepdims=True)
            dmat = jnp.where(causal, b_col + a_lane, -jnp.inf)
            m_prev = m_s[bi, h:h + 1, 0:1]
            inter = b_col + m_prev
            m_t = jnp.maximum(jnp.max(dmat, axis=1, keepdims=True), inter)
            qb = q.astype(BF16)
            kb = k.astype(BF16)
            vb = v.astype(BF16)
            s = _dot_nt(qb, kb) * jnp.exp(dmat - m_t)
            w_inter = jnp.exp(inter - m_t)
            ct = ct_s[bi * N_MH + h]
            num = _dot(s.astype(BF16), vb) + w_inter * _dot(qb, ct.astype(BF16))
            n_row = n_s[bi, h:h + 1, :]
            den = jnp.sum(s, axis=1, keepdims=True) + w_inter * jnp.sum(q * n_row, axis=1, keepdims=True)
            hout = num / jnp.maximum(jnp.abs(den), jnp.exp(-m_t))
            hm_ref[bi, :, h * DV:(h + 1) * DV] = (_sigmoid(o) * hout).astype(BF16)

            m_new = m_t[L - 1:L, :]
            b_last = b_col[L - 1:L, :]
            w_last = jnp.exp(b_last - b_col + ig_col - m_new)
            decay = jnp.exp(b_last + m_prev - m_new)
            kw = k * w_last
            ct_new = decay * ct + _dot_tn(kw.astype(BF16), vb)
            n_new = decay * n_row + jnp.sum(kw, axis=0, keepdims=True)
            ct_s[bi * N_MH + h] = ct_new
            n_s[bi, h:h + 1, :] = n_new
            m_s[bi, h:h + 1, :] = jnp.broadcast_to(m_new, (1, LANES))

    @pl.when(c == last)
    def _():
        for bi in range(nseq):
            for h in range(N_MH):
                c_out[bi, h] = ct_s[bi * N_MH + h].T
        n_out[...] = n_s[:, :N_MH, :]
        m_out[...] = m_s[...]


def _mlstm_prompt(z, bias_row, batch, seq, bb):
    nc = seq // CHUNK
    z3 = z.reshape(batch, seq, z.shape[1])

    def zspec(width, off):
        blk = off // width
        return pl.BlockSpec((bb, CHUNK, width), lambda b, c: (b, c, blk))

    return pl.pallas_call(
        _mlstm_prompt_kernel,
        grid=(batch // bb, nc),
        in_specs=[
            zspec(QK_W, _Z_QM), zspec(QK_W, _Z_KM), zspec(V_W, _Z_VM), zspec(V_W, _Z_OM),
            zspec(LANES, _Z_IF),
            pl.BlockSpec((1, LANES), lambda b, c: (0, 0)),
        ],
        out_specs=[
            pl.BlockSpec((bb, CHUNK, V_W), lambda b, c: (b, c, 0)),
            pl.BlockSpec((bb, N_MH, DV, DK), lambda b, c: (b, 0, 0, 0)),
            pl.BlockSpec((bb, N_MH, DK), lambda b, c: (b, 0, 0)),
            pl.BlockSpec((bb, 8, LANES), lambda b, c: (b, 0, 0)),
        ],
        out_shape=[
            jax.ShapeDtypeStruct((batch, seq, V_W), BF16),
            jax.ShapeDtypeStruct((batch, N_MH, DV, DK), F32),
            jax.ShapeDtypeStruct((batch, N_MH, DK), F32),
            jax.ShapeDtypeStruct((batch, 8, LANES), F32),
        ],
        scratch_shapes=[
            pltpu.VMEM((bb * N_MH, DK, DV), F32),
            pltpu.VMEM((bb, 8, DK), F32),
            pltpu.VMEM((bb, 8, LANES), F32),
        ],
        compiler_params=_cparams(("parallel", "arbitrary")),
        name="mlstm_prompt",
    )(z3, z3, z3, z3, z3, bias_row)


def _mlstm_sample_kernel(q_ref, k_ref, v_ref, o_ref, g_ref, bias_ref, c_ref, n_ref, m_ref,
                         hm_ref, c_out, n_out, m_out):
    gates = g_ref[0] + bias_ref[...]
    lf_all = _log_sigmoid(gates)
    m_all = m_ref[0]
    ri = lax.broadcasted_iota(jnp.int32, (DV, DV), 0)
    ci = lax.broadcasted_iota(jnp.int32, (DV, DV), 1)
    eye = ri == ci
    m_new_all = jnp.zeros((1, LANES), F32)
    lane = lax.broadcasted_iota(jnp.int32, (1, LANES), 1)

    for h in range(N_MH):
        q = q_ref[0, :, h * DK:(h + 1) * DK]
        k = k_ref[0, :, h * DK:(h + 1) * DK] * (DK ** -0.5)
        v = v_ref[0, :, h * DV:(h + 1) * DV]
        o = o_ref[0, :, h * DV:(h + 1) * DV]
        ig = gates[:, h:h + 1]
        lf = lf_all[:, N_MH + h:N_MH + h + 1]
        m_prev = m_all[:, h:h + 1]
        inter = lf + m_prev
        m_t = jnp.maximum(ig, inter)
        s = jnp.sum(q * k, axis=1, keepdims=True) * jnp.exp(ig - m_t)
        w_inter = jnp.exp(inter - m_t)
        cmat = c_ref[0, 0, h]
        q8 = jnp.broadcast_to(q, (8, DK)).astype(BF16)
        cq = _dot_nt(q8, cmat.astype(BF16))[0:1, :]
        n_row = n_ref[0, 0, h:h + 1, :]
        num = s * v + w_inter * cq
        den = s + w_inter * jnp.sum(n_row * q, axis=1, keepdims=True)
        hout = num / jnp.maximum(jnp.abs(den), jnp.exp(-m_t))
        hm_ref[0, :, h * DV:(h + 1) * DV] = (_sigmoid(o) * hout).astype(BF16)

        w_last = jnp.exp(ig - m_t)
        decay = jnp.exp(inter - m_t)
        vw = v * w_last
        vw_col = jnp.sum(jnp.where(eye, vw, 0.0), axis=1, keepdims=True)
        c_out[0, 0, h] = decay * cmat + vw_col * k
        n_out[0, 0, h:h + 1, :] = decay * n_row + w_last * k
        m_new_all = jnp.where(lane == h, m_t, m_new_all)

    m_out[0] = m_new_all


def _mlstm_sample(z3, bias_row, c0, n0, m0_pad, layer):
    nb = z3.shape[0]

    def zspec(width, off):
        blk = off // width
        return pl.BlockSpec((1, 1, width), lambda b: (b, 0, blk))

    c_in = pl.BlockSpec((1, 1, N_MH, DV, DK), lambda b: (layer, b, 0, 0, 0))
    n_in = pl.BlockSpec((1, 1, N_MH, DK), lambda b: (layer, b, 0, 0))
    c_blk = pl.BlockSpec((1, 1, N_MH, DV, DK), lambda b: (0, b, 0, 0, 0))
    n_blk = pl.BlockSpec((1, 1, N_MH, DK), lambda b: (0, b, 0, 0))

    return pl.pallas_call(
        _mlstm_sample_kernel,
        grid=(nb,),
        in_specs=[
            zspec(QK_W, _Z_QM), zspec(QK_W, _Z_KM), zspec(V_W, _Z_VM), zspec(V_W, _Z_OM),
            zspec(LANES, _Z_IF),
            pl.BlockSpec((1, LANES), lambda b: (0, 0)),
            c_in,
            n_in,
            pl.BlockSpec((1, 1, LANES), lambda b: (b, 0, 0)),
        ],
        out_specs=[
            pl.BlockSpec((1, 1, V_W), lambda b: (b, 0, 0)),
            c_blk,
            n_blk,
            pl.BlockSpec((1, 1, LANES), lambda b: (b, 0, 0)),
        ],
        out_shape=[
            jax.ShapeDtypeStruct((nb, 1, V_W), BF16),
            jax.ShapeDtypeStruct((1, nb, N_MH, DV, DK), F32),
            jax.ShapeDtypeStruct((1, nb, N_MH, DK), F32),
            jax.ShapeDtypeStruct((nb, 1, LANES), F32),
        ],
        compiler_params=_cparams(("parallel",)),
        name="mlstm_sample",
    )(z3, z3, z3, z3, z3, bias_row, c0, n0, m0_pad)


def _rope_tables(pos):
    half = ROT // 2
    inv = THETA ** (-jnp.arange(half, dtype=F32) * 2.0 / ROT)
    ang = pos.astype(F32)[:, None] * inv[None, :]
    cos = jnp.cos(ang)
    sin = jnp.sin(ang)
    t = pos.shape[0]
    cos_t = jnp.concatenate([cos, cos, jnp.ones((t, HD - ROT), F32)], axis=1)
    sin_a = jnp.concatenate([-sin, jnp.zeros((t, HD - half), F32)], axis=1)
    sin_b = jnp.concatenate([jnp.zeros((t, half), F32), sin, jnp.zeros((t, HD - ROT), F32)], axis=1)
    rep = LANES // HD
    return tuple(jnp.tile(a, (1, rep)) for a in (cos_t, sin_a, sin_b))


def _rope(x, cos_t, sin_a, sin_b):
    w = x.shape[1]
    rep = w // LANES
    half = ROT // 2
    if rep > 1:
        cos_t = jnp.concatenate([cos_t] * rep, axis=1)
        sin_a = jnp.concatenate([sin_a] * rep, axis=1)
        sin_b = jnp.concatenate([sin_b] * rep, axis=1)
    up = pltpu.roll(x, w - half, axis=1)
    down = pltpu.roll(x, half, axis=1)
    return x * cos_t + up * sin_a + down * sin_b


def _swa_prompt_kernel(sink_ref, q_ref, kc_ref, kp_ref, vc_ref, vp_ref, ha_ref):
    i = pl.program_id(1)
    w = WINDOW
    kk = jnp.concatenate([kp_ref[...], kc_ref[...]], axis=0).astype(BF16)
    vv = jnp.concatenate([vp_ref[...], vc_ref[...]], axis=0).astype(BF16)
    a_idx = lax.broadcasted_iota(jnp.int32, (w, 2 * w), 0)
    s_idx = lax.broadcasted_iota(jnp.int32, (w, 2 * w), 1)
    valid = (s_idx >= a_idx) & (s_idx <= a_idx + w) & ((i > 0) | (s_idx >= w))
    outs = []
    for g in range(N_KV):
        kg = kk[:, g * HD:(g + 1) * HD]
        vg = vv[:, g * HD:(g + 1) * HD]
        qg = jnp.concatenate(
            [q_ref[:, (g * GROUP + hh) * HD:(g * GROUP + hh + 1) * HD] for hh in range(GROUP)], axis=0)
        s = _dot_nt(qg.astype(BF16), kg).reshape(GROUP, w, 2 * w)
        s = jnp.where(valid[None], s, -jnp.inf)
        sink = sink_ref[g * GROUP:(g + 1) * GROUP]
        m = jnp.maximum(jnp.max(s, axis=2, keepdims=True), sink)
        e = jnp.exp(s - m)
        denom = jnp.sum(e, axis=2, keepdims=True) + jnp.exp(sink - m)
        o = _dot(e.reshape(GROUP * w, 2 * w).astype(BF16), vg).reshape(GROUP, w, HD) * (1.0 / denom)
        outs += [o[hh] for hh in range(GROUP)]
    ha_ref[...] = jnp.concatenate(outs, axis=1).astype(BF16)


def _swa_prompt(z, sinks3, batch, seq):
    rows = z.shape[0]
    nb = seq // WINDOW
    w = WINDOW

    def cur(width, off):
        blk = off // width
        return pl.BlockSpec((w, width), lambda b, i: (b * nb + i, blk))

    def prev(width, off):
        blk = off // width
        return pl.BlockSpec((w, width), lambda b, i: (b * nb + jnp.maximum(i - 1, 0), blk))

    return pl.pallas_call(
        _swa_prompt_kernel,
        grid=(batch, nb),
        in_specs=[
            pl.BlockSpec((N_Q, 1, 1), lambda b, i: (0, 0, 0)),
            cur(ATT_Q, _Z_QA), cur(ATT_KV, _Z_KA), prev(ATT_KV, _Z_KA),
            cur(ATT_KV, _Z_VA), prev(ATT_KV, _Z_VA),
        ],
        out_specs=pl.BlockSpec((w, ATT_Q), lambda b, i: (b * nb + i, 0)),
        out_shape=jax.ShapeDtypeStruct((rows, ATT_Q), BF16),
        compiler_params=_cparams(("parallel", "parallel")),
        name="swa_prompt",
    )(sinks3, z, z, z, z, z)


def _swa_sample_kernel(q_ref, kn_ref, vn_ref, kp_ref, vp_ref, sink_ref,
                       o_ref, ko_ref, vo_ref, *, bb):
    w = WINDOW
    hrow = lax.broadcasted_iota(jnp.int32, (N_Q, ATT_KV), 0) // GROUP
    glane = lax.broadcasted_iota(jnp.int32, (N_Q, ATT_KV), 1) // HD
    own = hrow == glane
    srow = lax.broadcasted_iota(jnp.int32, (w, ATT_KV), 0)
    sink = sink_ref[...]
    for b in range(bb):
        q = q_ref[b]
        qm = jnp.where(own, jnp.concatenate([q] * N_KV, axis=1), 0.0)
        k_past = kp_ref[b]
        v_past = vp_ref[b]
        k_new = kn_ref[b]
        v_new = vn_ref[b]
        s_past = _dot_nt(qm.astype(BF16), k_past.astype(BF16))
        s_new = jnp.sum(qm * k_new, axis=1, keepdims=True)
        m = jnp.maximum(jnp.maximum(jnp.max(s_past, axis=1, keepdims=True), s_new), sink)
        e_past = jnp.exp(s_past - m)
        e_new = jnp.exp(s_new - m)
        denom = jnp.sum(e_past, axis=1, keepdims=True) + e_new + jnp.exp(sink - m)
        r = 1.0 / denom
        pv = _dot((e_past * r).astype(BF16), v_past.astype(BF16)) + (e_new * r) * v_new
        pv = jnp.where(own, pv, 0.0)
        o = pv[:, 0:HD]
        for g in range(1, N_KV):
            o = o + pv[:, g * HD:(g + 1) * HD]
        o_ref[b] = o
        ko_ref[b] = jnp.where(srow == w - 1, k_new, pltpu.roll(k_past, w - 1, axis=0))
        vo_ref[b] = jnp.where(srow == w - 1, v_new, pltpu.roll(v_past, w - 1, axis=0))


def _swa_sample(q3, k_new3, v_new3, k_past, v_past, sink_col, bb):
    nb = q3.shape[0]
    w = WINDOW
    return pl.pallas_call(
        functools.partial(_swa_sample_kernel, bb=bb),
        grid=(nb // bb,),
        in_specs=[
            pl.BlockSpec((bb, N_Q, HD), lambda i: (i, 0, 0)),
            pl.BlockSpec((bb, 1, ATT_KV), lambda i: (i, 0, _Z_KA // ATT_KV)),
            pl.BlockSpec((bb, 1, ATT_KV), lambda i: (i, 0, _Z_VA // ATT_KV)),
            pl.BlockSpec((bb, w, ATT_KV), lambda i: (i, 0, 0)),
            pl.BlockSpec((bb, w, ATT_KV), lambda i: (i, 0, 0)),
            pl.BlockSpec((N_Q, 1), lambda i: (0, 0)),
        ],
        out_specs=[
            pl.BlockSpec((bb, N_Q, HD), lambda i: (i, 0, 0)),
            pl.BlockSpec((bb, w, ATT_KV), lambda i: (i, 0, 0)),
            pl.BlockSpec((bb, w, ATT_KV), lambda i: (i, 0, 0)),
        ],
        out_shape=[
            jax.ShapeDtypeStruct((nb, N_Q, HD), F32),
            jax.ShapeDtypeStruct((nb, w, ATT_KV), F32),
            jax.ShapeDtypeStruct((nb, w, ATT_KV), F32),
        ],
        compiler_params=_cparams(("parallel",)),
        name="swa_sample",
    )(q3, k_new3, v_new3, k_past, v_past, sink_col)


def _merge1_kernel(hm_ref, ha_ref, ga_ref, gb_ref, wa_ref, wb_ref, t_ref):
    a = _dot(hm_ref[...], wa_ref[...])
    b = _dot(ha_ref[...], wb_ref[...])
    t_ref[...] = (_sigmoid(ga_ref[...]) * a + _sigmoid(gb_ref[...]) * b).astype(BF16)


def _merge1(hm, ha, z, wa, wb, tm, tn):
    rows, d = hm.shape
    n = wa.shape[1]
    return pl.pallas_call(
        _merge1_kernel,
        grid=(rows // tm, n // tn),
        in_specs=[
            pl.BlockSpec((tm, d), lambda i, j: (i, 0)),
            pl.BlockSpec((tm, d), lambda i, j: (i, 0)),
            pl.BlockSpec((tm, tn), lambda i, j: (i, _Z_GA // tn + j)),
            pl.BlockSpec((tm, tn), lambda i, j: (i, _Z_GB // tn + j)),
            pl.BlockSpec((d, tn), lambda i, j: (0, j)),
            pl.BlockSpec((d, tn), lambda i, j: (0, j)),
        ],
        out_specs=pl.BlockSpec((tm, tn), lambda i, j: (i, j)),
        out_shape=jax.ShapeDtypeStruct((rows, n), BF16),
        compiler_params=_cparams(("parallel", "arbitrary")),
        name="merge1",
    )(hm, ha, z, z, wa, wb)


def _merge2_kernel(t_ref, w_ref, x_ref, g1_ref, g2_ref, x1_ref, h2_ref):
    mix = _dot(t_ref[...], w_ref[...])
    x1 = x_ref[...] + _rms(mix, g1_ref[...])
    x1_ref[...] = x1
    h2_ref[...] = _rms(x1, g2_ref[...]).astype(BF16)


def _merge2(t, w_out, x2d, g1, g2, tm):
    rows, d = x2d.shape
    row = lambda i: (i, 0)
    const = lambda i: (0, 0)
    return pl.pallas_call(
        _merge2_kernel,
        grid=(rows // tm,),
        in_specs=[
            pl.BlockSpec((tm, d), row),
            pl.BlockSpec((d, d), const, pipeline_mode=pl.Buffered(1)),
            pl.BlockSpec((tm, d), row),
            pl.BlockSpec((1, d), const),
            pl.BlockSpec((1, d), const),
        ],
        out_specs=[pl.BlockSpec((tm, d), row), pl.BlockSpec((tm, d), row)],
        out_shape=[jax.ShapeDtypeStruct((rows, d), F32), jax.ShapeDtypeStruct((rows, d), BF16)],
        compiler_params=_cparams(("parallel",)),
        name="merge2",
    )(t, w_out, x2d, g1.reshape(1, d), g2.reshape(1, d))


def _gelu_tanh(x):
    return 0.5 * x * (1.0 + jnp.tanh(0.7978845608028654 * (x + 0.044715 * (x * x * x))))


def _conv_taps(u, u1, u2, cw_ref, cb_ref):
    return cb_ref[...] + u2 * cw_ref[0:1, :] + u1 * cw_ref[1:2, :] + u * cw_ref[2:3, :]


def _ffn_tail(j, nf, y, wd_ref, x1_ref, g_ref, y_ref, acc_ref):
    @pl.when(j == 0)
    def _():
        acc_ref[...] = _dot(y, wd_ref[...])

    @pl.when(j > 0)
    def _():
        acc_ref[...] += _dot(y, wd_ref[...])

    @pl.when(j == nf - 1)
    def _():
        y_ref[...] = x1_ref[...] + _rms(acc_ref[...], g_ref[...])


def _ffn_prompt_kernel(h_ref, wg_ref, wv_ref, cwg_ref, cwv_ref, cbg_ref, cbv_ref, wd_ref,
                       x1_ref, g_ref, y_ref, sg_ref, sv_ref, acc_ref, carry_ref,
                       *, tiles_per_seq):
    i = pl.program_id(0)
    j = pl.program_id(1)
    nf = pl.num_programs(1)
    tm = h_ref.shape[0]

    @pl.when((i == 0) & (j == 0))
    def _():
        carry_ref[...] = jnp.zeros_like(carry_ref)

    seq_start = (i % tiles_per_seq) == 0
    h = h_ref[...]

    def branch(w_ref, cw_ref, cb_ref, slot, state_ref):
        u = _dot(h, w_ref[...])
        prev = jnp.where(seq_start, 0.0, carry_ref[slot])
        uu = jnp.concatenate([prev, u], axis=0)
        u1 = uu[7:7 + tm, :]
        u2 = uu[6:6 + tm, :]
        tail = u[tm - 8:tm, :]
        carry_ref[slot] = tail
        state_ref[0] = tail
        return _conv_taps(u, u1, u2, cw_ref, cb_ref)

    cg = branch(wg_ref, cwg_ref, cbg_ref, 2 * j, sg_ref)
    cv = branch(wv_ref, cwv_ref, cbv_ref, 2 * j + 1, sv_ref)
    y = (_gelu_tanh(cg) * cv).astype(BF16)
    _ffn_tail(j, nf, y, wd_ref, x1_ref, g_ref, y_ref, acc_ref)


def _ffn_sample_kernel(h_ref, wg_ref, wv_ref, cwg_ref, cwv_ref, cbg_ref, cbv_ref, wd_ref,
                       x1_ref, g_ref, p2g_ref, p2v_ref, p1g_ref, p1v_ref,
                       y_ref, ug_ref, uv_ref, acc_ref):
    j = pl.program_id(1)
    nf = pl.num_programs(1)
    h = h_ref[...]
    ug = _dot(h, wg_ref[...])
    uv = _dot(h, wv_ref[...])
    ug_ref[...] = ug
    uv_ref[...] = uv
    cg = _conv_taps(ug, p1g_ref[...], p2g_ref[...], cwg_ref, cbg_ref)
    cv = _conv_taps(uv, p1v_ref[...], p2v_ref[...], cwv_ref, cbv_ref)
    y = (_gelu_tanh(cg) * cv).astype(BF16)
    _ffn_tail(j, nf, y, wd_ref, x1_ref, g_ref, y_ref, acc_ref)


def _ffn(h2, x1, w_up, w_conv, b_conv2d, w_down, g, tm, tf, seq=None, past=None):
    rows, d = h2.shape
    f = w_down.shape[0]
    nf = f // tf
    row = lambda i, j: (i, 0)
    gcol = lambda i, j: (0, j)
    vcol = lambda i, j: (0, nf + j)
    in_specs = [
        pl.BlockSpec((tm, d), row),
        pl.BlockSpec((d, tf), gcol), pl.BlockSpec((d, tf), vcol),
        pl.BlockSpec((3, tf), gcol), pl.BlockSpec((3, tf), vcol),
        pl.BlockSpec((1, tf), gcol), pl.BlockSpec((1, tf), vcol),
        pl.BlockSpec((tf, d), lambda i, j: (j, 0)),
        pl.BlockSpec((tm, d), row),
        pl.BlockSpec((1, d), lambda i, j: (0, 0)),
    ]
    args = [h2, w_up, w_up, w_conv, w_conv, b_conv2d, b_conv2d, w_down, x1, g.reshape(1, d)]
    scratch = [pltpu.VMEM((tm, d), F32)]
    if past is None:
        tiles_per_seq = seq // tm
        kern = functools.partial(_ffn_prompt_kernel, tiles_per_seq=tiles_per_seq)
        state_spec = pl.BlockSpec((1, 8, tf), lambda i, j: (i, 0, j))
        out_specs = [pl.BlockSpec((tm, d), row), state_spec, state_spec]
        out_shape = [jax.ShapeDtypeStruct((rows, d), F32),
                     jax.ShapeDtypeStruct((rows // tm, 8, f), F32),
                     jax.ShapeDtypeStruct((rows // tm, 8, f), F32)]
        scratch.append(pltpu.VMEM((2 * nf, 8, tf), F32))
        name = "ffn_prompt"
    else:
        p2, p1 = past
        kern = _ffn_sample_kernel
        ucol = pl.BlockSpec((tm, tf), lambda i, j: (i, j))
        in_specs += [
            pl.BlockSpec((tm, tf), lambda i, j: (i, j)), pl.BlockSpec((tm, tf), lambda i, j: (i, nf + j)),
            pl.BlockSpec((tm, tf), lambda i, j: (i, j)), pl.BlockSpec((tm, tf), lambda i, j: (i, nf + j)),
        ]
        args += [p2, p2, p1, p1]
        out_specs = [pl.BlockSpec((tm, d), row), ucol, ucol]
        out_shape = [jax.ShapeDtypeStruct((rows, d), F32),
                     jax.ShapeDtypeStruct((rows, f), F32),
                     jax.ShapeDtypeStruct((rows, f), F32)]
        name = "ffn_sample"
    return pl.pallas_call(
        kern,
        grid=(rows // tm, nf),
        in_specs=in_specs,
        out_specs=out_specs,
        out_shape=out_shape,
        scratch_shapes=scratch,
        compiler_params=_cparams(("arbitrary", "arbitrary")),
        name=name,
    )(*args)


def _regroup_w_in(w_in):
    d = w_in.shape[0]
    o_if = QK_W * 2 + V_W * 2
    o_qa = o_if + 2 * N_MH
    o_ka = o_qa + ATT_Q
    o_ga = o_ka + 2 * ATT_KV
    parts = [
        w_in[:, :o_if],
        w_in[:, o_qa:o_ka],
        w_in[:, o_ga:],
        w_in[:, o_ka:o_ga],
        w_in[:, o_if:o_qa],
    ]
    used = sum(p.shape[1] for p in parts)
    parts.append(jnp.zeros((d, _Z_W - used), w_in.dtype))
    return jnp.concatenate(parts, axis=1).astype(BF16)


def _pick(rows, pref):
    t = min(rows, pref)
    while rows % t:
        t //= 2
    return t


def kernel(x_prompt, x_sample, state_mlstm_c, state_mlstm_n, state_mlstm_m, cache_swa_k, cache_swa_v,
           state_ffn_conv, g_pre_mix, w_in, b_if, attn_sinks, w_branch_a, w_branch_b, w_out, g_post_mix,
           g_pre_ffn, w_up, w_conv, b_conv, w_down, g_post_ffn):
    bp, tp, d = x_prompt.shape
    bs, ts, _ = x_sample.shape
    depth = w_in.shape[0]
    assert depth == 1 and ts == 1 and d == D_MODEL
    f = w_down.shape[1]
    tf = 512
    assert f % tf == 0

    l = 0
    w_all = _regroup_w_in(w_in[l])
    wa = w_branch_a[l].astype(BF16)
    wb = w_branch_b[l].astype(BF16)
    wo = w_out[l].astype(BF16)
    wu = w_up[l].astype(BF16)
    wd = w_down[l].astype(BF16)
    bias_row = jnp.concatenate([b_if[l].astype(F32), jnp.zeros((LANES - 2 * N_MH,), F32)]).reshape(1, LANES)
    sinks = attn_sinks[l].astype(F32)
    b_conv2d = b_conv[l].reshape(1, 2 * f)

    xp = x_prompt.reshape(bp * tp, d)
    zp = _project(xp, g_pre_mix[l], w_all, _rope_tables(jnp.arange(tp)), _pick(tp, 1024))
    hm_p3, c_p, n_p, m_p = _mlstm_prompt(zp, bias_row, bp, tp, _pick(bp, 4))
    hm_p = hm_p3.reshape(bp * tp, V_W)
    ha_p = _swa_prompt(zp, sinks.reshape(N_Q, 1, 1), bp, tp)
    t_p = _merge1(hm_p, ha_p, zp, wa, wb, _pick(bp * tp, 1024), 512)
    x1_p, h2_p = _merge2(t_p, wo, xp, g_post_mix[l], g_pre_ffn[l], _pick(bp * tp, 512))
    tm_ffn = _pick(tp, 512)
    y_p, sg_p, sv_p = _ffn(h2_p, x1_p, wu, w_conv[l], b_conv2d, wd, g_post_ffn[l],
                           tm_ffn, tf, seq=tp)
    seq_end = slice(tp // tm_ffn - 1, None, tp // tm_ffn)

    y_prompt = y_p.reshape(bp, tp, d)
    c_prompt = c_p[None]
    n_prompt = n_p[None]
    m_prompt = m_p[:, :N_MH, 0][None]
    kv_tail = zp.reshape(bp, tp, _Z_W)[:, tp - WINDOW:, _Z_KA:_Z_KA + 2 * ATT_KV]
    k_prompt = kv_tail[..., :ATT_KV].reshape(bp, WINDOW, N_KV, HD)[None]
    v_prompt = kv_tail[..., ATT_KV:].reshape(bp, WINDOW, N_KV, HD)[None]
    conv_prompt = jnp.concatenate([sg_p[seq_end, 6:8, :], sv_p[seq_end, 6:8, :]], axis=-1)[None]

    xs = x_sample.reshape(bs, d)
    tabs_s = tuple(jnp.broadcast_to(a, (bs, LANES)) for a in _rope_tables(PAST_LEN + jnp.arange(ts)))
    zs = _project(xs, g_pre_mix[l], w_all, tabs_s, bs)
    zs3 = zs.reshape(bs, 1, _Z_W)
    m0_pad = jnp.pad(state_mlstm_m[l].astype(F32), ((0, 0), (0, LANES - N_MH))).reshape(bs, 1, LANES)
    hm_s3, c_sample, n_sample, m_s3 = _mlstm_sample(zs3, bias_row, state_mlstm_c, state_mlstm_n, m0_pad, l)
    o_s3, k_s, v_s = _swa_sample(
        zs[:, _Z_QA:_Z_QA + ATT_Q].reshape(bs, N_Q, HD), zs3, zs3,
        cache_swa_k[l].reshape(bs, WINDOW, ATT_KV), cache_swa_v[l].reshape(bs, WINDOW, ATT_KV),
        sinks.reshape(N_Q, 1), _pick(bs, 8))
    ha_s = o_s3.reshape(bs, ATT_Q).astype(BF16)
    t_s = _merge1(hm_s3.reshape(bs, V_W), ha_s, zs, wa, wb, bs, 512)
    x1_s, h2_s = _merge2(t_s, wo, xs, g_post_mix[l], g_pre_ffn[l], bs)
    past = state_ffn_conv[l].astype(F32)
    y_s, ug_s, uv_s = _ffn(h2_s, x1_s, wu, w_conv[l], b_conv2d, wd, g_post_ffn[l],
                           bs, tf, past=(past[:, 0, :], past[:, 1, :]))

    y_sample = y_s.reshape(bs, ts, d)
    m_sample = m_s3[:, 0, :N_MH][None]
    k_sample = k_s.reshape(bs, WINDOW, N_KV, HD)[None]
    v_sample = v_s.reshape(bs, WINDOW, N_KV, HD)[None]
    conv_sample = jnp.stack([past[:, 1, :], jnp.concatenate([ug_s, uv_s], axis=-1)], axis=1)[None]

    return (y_prompt, y_sample,
            c_prompt, n_prompt, m_prompt, k_prompt, v_prompt, conv_prompt,
            c_sample, n_sample, m_sample, k_sample, v_sample, conv_sample)
```

```python
import functools

import jax
import jax.numpy as jnp
from jax import lax
from jax.experimental import pallas as pl
from jax.experimental.pallas import tpu as pltpu

F32 = jnp.float32
BF16 = jnp.bfloat16

N_MH = 4
DK = 256
DV = 512
CHUNK = 64
HD = 64
N_Q = 32
N_KV = 4
GROUP = N_Q // N_KV
WINDOW = 128
ROT = 16
THETA = 500000.0
PAST_LEN = 16384
EPS = 1e-6
D_MODEL = 2048
QK_W = N_MH * DK
V_W = N_MH * DV
ATT_Q = N_Q * HD
ATT_KV = N_KV * HD

_Z_QM = 0
_Z_KM = _Z_QM + QK_W
_Z_VM = _Z_KM + QK_W
_Z_OM = _Z_VM + V_W
_Z_QA = _Z_OM + V_W
_Z_GA = _Z_QA + ATT_Q
_Z_GB = _Z_GA + D_MODEL
_Z_KA = _Z_GB + D_MODEL
_Z_VA = _Z_KA + ATT_KV
_Z_IF = _Z_VA + ATT_KV
LANES = 128
_PROJ_TN = 1024
_Z_W = -(-(_Z_IF + LANES) // _PROJ_TN) * _PROJ_TN
assert _Z_QA % _PROJ_TN == 0 and _Z_GA % _PROJ_TN == 0 and _Z_KA % _PROJ_TN == 0

_VMEM_LIMIT = 56 * 1024 * 1024


def _cparams(sem):
    return pltpu.CompilerParams(dimension_semantics=sem, vmem_limit_bytes=_VMEM_LIMIT)


def _rms(x, g):
    return x * lax.rsqrt(jnp.mean(x * x, axis=-1, keepdims=True) + EPS) * g


def _sigmoid(x):
    return 1.0 / (1.0 + jnp.exp(-x))


def _log_sigmoid(x):
    return jnp.minimum(x, 0.0) - jnp.log1p(jnp.exp(-jnp.abs(x)))


def _dot(a, b):
    return jnp.dot(a, b, preferred_element_type=F32)


def _dot_nt(a, b):
    return lax.dot_general(a, b, (((1,), (1,)), ((), ())), preferred_element_type=F32)


def _dot_tn(a, b):
    return lax.dot_general(a, b, (((0,), (0,)), ((), ())), preferred_element_type=F32)


def _proj_kernel(x_ref, g_ref, w_ref, cos_ref, sa_ref, sb_ref, z_ref, h_ref):
    j = pl.program_id(1)
    jq0 = _Z_QA // _PROJ_TN
    jq1 = _Z_GA // _PROJ_TN
    jk = _Z_KA // _PROJ_TN

    @pl.when(j == 0)
    def _():
        h_ref[...] = _rms(x_ref[...], g_ref[...]).astype(BF16)

    is_q = (j >= jq0) & (j < jq1)
    is_k = j == jk

    @pl.when(is_q)
    def _():
        acc = _dot(h_ref[...], w_ref[...])
        z_ref[...] = _rope(acc, cos_ref[...], sa_ref[...], sb_ref[...]) * (HD ** -0.5)

    @pl.when(is_k)
    def _():
        acc = _dot(h_ref[...], w_ref[...])
        z_ref[...] = acc
        z_ref[:, :ATT_KV] = _rope(acc[:, :ATT_KV], cos_ref[...], sa_ref[...], sb_ref[...])

    @pl.when(jnp.logical_not(is_q | is_k))
    def _():
        z_ref[...] = _dot(h_ref[...], w_ref[...])


def _project(x2d, g, w_all, tables, tm):
    rows, d = x2d.shape
    n = w_all.shape[1]
    tn = _PROJ_TN
    tab_tiles = tables[0].shape[0] // tm
    tab = pl.BlockSpec((tm, LANES), lambda i, j: (i % tab_tiles, 0))
    return pl.pallas_call(
        _proj_kernel,
        grid=(rows // tm, n // tn),
        in_specs=[
            pl.BlockSpec((tm, d), lambda i, j: (i, 0)),
            pl.BlockSpec((1, d), lambda i, j: (0, 0)),
            pl.BlockSpec((d, tn), lambda i, j: (0, j)),
            tab, tab, tab,
        ],
        out_specs=pl.BlockSpec((tm, tn), lambda i, j: (i, j)),
        out_shape=jax.ShapeDtypeStruct((rows, n), F32),
        scratch_shapes=[pltpu.VMEM((tm, d), BF16)],
        compiler_params=_cparams(("parallel", "arbitrary")),
        name="proj",
    )(x2d, g.reshape(1, d), w_all, *tables)


def _mlstm_prompt_kernel(q_ref, k_ref, v_ref, o_ref, g_ref, bias_ref,
                         hm_ref, c_out, n_out, m_out, ct_s, n_s, m_s):
    c = pl.program_id(1)
    last = pl.num_programs(1) - 1
    L = CHUNK
    nseq = q_ref.shape[0]

    @pl.when(c == 0)
    def _():
        ct_s[...] = jnp.zeros_like(ct_s)
        n_s[...] = jnp.zeros_like(n_s)
        m_s[...] = jnp.zeros_like(m_s)

    row = lax.broadcasted_iota(jnp.int32, (L, LANES), 0)
    ri = lax.broadcasted_iota(jnp.int32, (L, L), 0)
    ci = lax.broadcasted_iota(jnp.int32, (L, L), 1)
    eye = ri == ci
    causal = ci <= ri

    for bi in range(nseq):
        gates = g_ref[bi] + bias_ref[...]
        bcum = _log_sigmoid(gates)
        for d in (1, 2, 4, 8, 16, 32):
            bcum = bcum + jnp.where(row >= d, pltpu.roll(bcum, d, axis=0), 0.0)

        for h in range(N_MH):
            q = q_ref[bi, :, h * DK:(h + 1) * DK]
            k = k_ref[bi, :, h * DK:(h + 1) * DK] * (DK ** -0.5)
            v = v_ref[bi, :, h * DV:(h + 1) * DV]
            o = o_ref[bi, :, h * DV:(h + 1) * DV]
            b_col = bcum[:, N_MH + h:N_MH + h + 1]
            ig_col = gates[:, h:h + 1]
            a_col = ig_col - b_col
            a_lane = jnp.sum(jnp.where(eye, a_col, 0.0), axis=0, keepdims=True)
            dmat = jnp.where(causal, b_col + a_lane, -jnp.inf)
            m_prev = m_s[bi, h:h + 1, 0:1]
            inter = b_col + m_prev
            m_t = jnp.maximum(jnp.max(dmat, axis=1, keepdims=True), inter)
            qb = q.astype(BF16)
            kb = k.astype(BF16)
            vb = v.astype(BF16)
            s = _dot_nt(qb, kb) * jnp.exp(dmat - m_t)
            w_inter = jnp.exp(inter - m_t)
            ct = ct_s[bi * N_MH + h]
            num = _dot(s.astype(BF16), vb) + w_inter * _dot(qb, ct.astype(BF16))
            n_row = n_s[bi, h:h + 1, :]
            den = jnp.sum(s, axis=1, keepdims=True) + w_inter * jnp.sum(q * n_row, axis=1, keepdims=True)
            hout = num / jnp.maximum(jnp.abs(den), jnp.exp(-m_t))
            hm_ref[bi, :, h * DV:(h + 1) * DV] = (_sigmoid(o) * hout).astype(BF16)

            m_new = m_t[L - 1:L, :]
            b_last = b_col[L - 1:L, :]
            w_last = jnp.exp(b_last - b_col + ig_col - m_new)
            decay = jnp.exp(b_last + m_prev - m_new)
            kw = k * w_last
            ct_new = decay * ct + _dot_tn(kw.astype(BF16), vb)
            n_new = decay * n_row + jnp.sum(kw, axis=0, keepdims=True)
            ct_s[bi * N_MH + h] = ct_new
            n_s[bi, h:h + 1, :] = n_new
            m_s[bi, h:h + 1, :] = jnp.broadcast_to(m_new, (1, LANES))

    @pl.when(c == last)
    def _():
        for bi in range(nseq):
            for h in range(N_MH):
                c_out[bi, h] = ct_s[bi * N_MH + h].T
        n_out[...] = n_s[:, :N_MH, :]
        m_out[...] = m_s[...]


def _mlstm_prompt(z, bias_row, batch, seq, bb):
    nc = seq // CHUNK
    z3 = z.reshape(batch, seq, z.shape[1])

    def zspec(width, off):
        blk = off // width
        return pl.BlockSpec((bb, CHUNK, width), lambda b, c: (b, c, blk))

    return pl.pallas_call(
        _mlstm_prompt_kernel,
        grid=(batch // bb, nc),
        in_specs=[
            zspec(QK_W, _Z_QM), zspec(QK_W, _Z_KM), zspec(V_W, _Z_VM), zspec(V_W, _Z_OM),
            zspec(LANES, _Z_IF),
            pl.BlockSpec((1, LANES), lambda b, c: (0, 0)),
        ],
        out_specs=[
            pl.BlockSpec((bb, CHUNK, V_W), lambda b, c: (b, c, 0)),
            pl.BlockSpec((bb, N_MH, DV, DK), lambda b, c: (b, 0, 0, 0)),
            pl.BlockSpec((bb, N_MH, DK), lambda b, c: (b, 0, 0)),
            pl.BlockSpec((bb, 8, LANES), lambda b, c: (b, 0, 0)),
        ],
        out_shape=[
            jax.ShapeDtypeStruct((batch, seq, V_W), BF16),
            jax.ShapeDtypeStruct((batch, N_MH, DV, DK), F32),
            jax.ShapeDtypeStruct((batch, N_MH, DK), F32),
            jax.ShapeDtypeStruct((batch, 8, LANES), F32),
        ],
        scratch_shapes=[
            pltpu.VMEM((bb * N_MH, DK, DV), F32),
            pltpu.VMEM((bb, 8, DK), F32),
            pltpu.VMEM((bb, 8, LANES), F32),
        ],
        compiler_params=_cparams(("parallel", "arbitrary")),
        name="mlstm_prompt",
    )(z3, z3, z3, z3, z3, bias_row)


def _mlstm_sample_kernel(q_ref, k_ref, v_ref, o_ref, g_ref, bias_ref, c_ref, n_ref, m_ref,
                         hm_ref, c_out, n_out, m_out):
    gates = g_ref[0] + bias_ref[...]
    lf_all = _log_sigmoid(gates)
    m_all = m_ref[0]
    ri = lax.broadcasted_iota(jnp.int32, (DV, DV), 0)
    ci = lax.broadcasted_iota(jnp.int32, (DV, DV), 1)
    eye = ri == ci
    m_new_all = jnp.zeros((1, LANES), F32)
    lane = lax.broadcasted_iota(jnp.int32, (1, LANES), 1)

    for h in range(N_MH):
        q = q_ref[0, :, h * DK:(h + 1) * DK]
        k = k_ref[0, :, h * DK:(h + 1) * DK] * (DK ** -0.5)
        v = v_ref[0, :, h * DV:(h + 1) * DV]
        o = o_ref[0, :, h * DV:(h + 1) * DV]
        ig = gates[:, h:h + 1]
        lf = lf_all[:, N_MH + h:N_MH + h + 1]
        m_prev = m_all[:, h:h + 1]
        inter = lf + m_prev
        m_t = jnp.maximum(ig, inter)
        s = jnp.sum(q * k, axis=1, keepdims=True) * jnp.exp(ig - m_t)
        w_inter = jnp.exp(inter - m_t)
        cmat = c_ref[0, 0, h]
        q8 = jnp.broadcast_to(q, (8, DK)).astype(BF16)
        cq = _dot_nt(q8, cmat.astype(BF16))[0:1, :]
        n_row = n_ref[0, 0, h:h + 1, :]
        num = s * v + w_inter * cq
        den = s + w_inter * jnp.sum(n_row * q, axis=1, keepdims=True)
        hout = num / jnp.maximum(jnp.abs(den), jnp.exp(-m_t))
        hm_ref[0, :, h * DV:(h + 1) * DV] = (_sigmoid(o) * hout).astype(BF16)

        w_last = jnp.exp(ig - m_t)
        decay = jnp.exp(inter - m_t)
        vw = v * w_last
        vw_col = jnp.sum(jnp.where(eye, vw, 0.0), axis=1, keepdims=True)
        c_out[0, 0, h] = decay * cmat + vw_col * k
        n_out[0, 0, h:h + 1, :] = decay * n_row + w_last * k
        m_new_all = jnp.where(lane == h, m_t, m_new_all)

    m_out[0] = m_new_all


def _mlstm_sample(z3, bias_row, c0, n0, m0_pad, layer):
    nb = z3.shape[0]

    def zspec(width, off):
        blk = off // width
        return pl.BlockSpec((1, 1, width), lambda b: (b, 0, blk))

    c_in = pl.BlockSpec((1, 1, N_MH, DV, DK), lambda b: (layer, b, 0, 0, 0))
    n_in = pl.BlockSpec((1, 1, N_MH, DK), lambda b: (layer, b, 0, 0))
    c_blk = pl.BlockSpec((1, 1, N_MH, DV, DK), lambda b: (0, b, 0, 0, 0))
    n_blk = pl.BlockSpec((1, 1, N_MH, DK), lambda b: (0, b, 0, 0))

    return pl.pallas_call(
        _mlstm_sample_kernel,
        grid=(nb,),
        in_specs=[
            zspec(QK_W, _Z_QM), zspec(QK_W, _Z_KM), zspec(V_W, _Z_VM), zspec(V_W, _Z_OM),
            zspec(LANES, _Z_IF),
            pl.BlockSpec((1, LANES), lambda b: (0, 0)),
            c_in,
            n_in,
            pl.BlockSpec((1, 1, LANES), lambda b: (b, 0, 0)),
        ],
        out_specs=[
            pl.BlockSpec((1, 1, V_W), lambda b: (b, 0, 0)),
            c_blk,
            n_blk,
            pl.BlockSpec((1, 1, LANES), lambda b: (b, 0, 0)),
        ],
        out_shape=[
            jax.ShapeDtypeStruct((nb, 1, V_W), BF16),
            jax.ShapeDtypeStruct((1, nb, N_MH, DV, DK), F32),
            jax.ShapeDtypeStruct((1, nb, N_MH, DK), F32),
            jax.ShapeDtypeStruct((nb, 1, LANES), F32),
        ],
        compiler_params=_cparams(("parallel",)),
        name="mlstm_sample",
    )(z3, z3, z3, z3, z3, bias_row, c0, n0, m0_pad)


def _rope_tables(pos):
    half = ROT // 2
    inv = THETA ** (-jnp.arange(half, dtype=F32) * 2.0 / ROT)
    ang = pos.astype(F32)[:, None] * inv[None, :]
    cos = jnp.cos(ang)
    sin = jnp.sin(ang)
    t = pos.shape[0]
    cos_t = jnp.concatenate([cos, cos, jnp.ones((t, HD - ROT), F32)], axis=1)
    sin_a = jnp.concatenate([-sin, jnp.zeros((t, HD - half), F32)], axis=1)
    sin_b = jnp.concatenate([jnp.zeros((t, half), F32), sin, jnp.zeros((t, HD - ROT), F32)], axis=1)
    rep = LANES // HD
    return tuple(jnp.tile(a, (1, rep)) for a in (cos_t, sin_a, sin_b))


def _rope(x, cos_t, sin_a, sin_b):
    w = x.shape[1]
    rep = w // LANES
    half = ROT // 2
    if rep > 1:
        cos_t = jnp.concatenate([cos_t] * rep, axis=1)
        sin_a = jnp.concatenate([sin_a] * rep, axis=1)
        sin_b = jnp.concatenate([sin_b] * rep, axis=1)
    up = pltpu.roll(x, w - half, axis=1)
    down = pltpu.roll(x, half, axis=1)
    return x * cos_t + up * sin_a + down * sin_b


def _swa_prompt_kernel(sink_ref, q_ref, kc_ref, kp_ref, vc_ref, vp_ref, ha_ref):
    i = pl.program_id(1)
    w = WINDOW
    kk = jnp.concatenate([kp_ref[...], kc_ref[...]], axis=0)
    vv = jnp.concatenate([vp_ref[...], vc_ref[...]], axis=0)
    vv_t = vv.T.astype(BF16)
    s_idx = lax.broadcasted_iota(jnp.int32, (2 * w, w), 0)
    a_idx = lax.broadcasted_iota(jnp.int32, (2 * w, w), 1)
    valid = (s_idx >= a_idx) & (s_idx <= a_idx + w) & ((i > 0) | (s_idx >= w))
    lane = lax.broadcasted_iota(jnp.int32, (2 * w, LANES), 1)
    pair_w = 2 * HD
    for half in range(N_KV // 2):
        kh = kk[:, half * pair_w:(half + 1) * pair_w]
        kh_sw = pltpu.roll(kh, HD, axis=1)
        for gg in range(2):
            g = 2 * half + gg
            k_lo = jnp.where(lane < HD, kh if gg == 0 else kh_sw, 0.0).astype(BF16)
            k_hi = jnp.where(lane >= HD, kh_sw if gg == 0 else kh, 0.0).astype(BF16)
            vg_t = vv_t[g * HD:(g + 1) * HD, :]
            sts = []
            for a in range(GROUP // 2):
                col = (g * GROUP + 2 * a) * HD
                qpair = q_ref[:, col:col + pair_w].astype(BF16)
                sts += [_dot_nt(k_lo, qpair), _dot_nt(k_hi, qpair)]
            es, rs = [], []
            for hh in range(GROUP):
                sink = sink_ref[g * GROUP + hh]
                st = jnp.where(valid, sts[hh], -jnp.inf)
                m = jnp.maximum(jnp.max(st, axis=0, keepdims=True), sink)
                e = jnp.exp(st - m)
                es.append(e.astype(BF16))
                rs.append(1.0 / (jnp.sum(e, axis=0, keepdims=True) + jnp.exp(sink - m)))
            ots = [_dot(vg_t, es[hh]) * rs[hh] for hh in range(GROUP)]
            for a in range(GROUP // 2):
                col = (g * GROUP + 2 * a) * HD
                ha_ref[:, col:col + pair_w] = jnp.concatenate(ots[2 * a:2 * a + 2], axis=0).T.astype(BF16)


def _swa_prompt(z, sinks3, batch, seq):
    rows = z.shape[0]
    nb = seq // WINDOW
    w = WINDOW

    def cur(width, off):
        blk = off // width
        return pl.BlockSpec((w, width), lambda b, i: (b * nb + i, blk))

    def prev(width, off):
        blk = off // width
        return pl.BlockSpec((w, width), lambda b, i: (b * nb + jnp.maximum(i - 1, 0), blk))

    return pl.pallas_call(
        _swa_prompt_kernel,
        grid=(batch, nb),
        in_specs=[
            pl.BlockSpec(memory_space=pltpu.SMEM),
            cur(ATT_Q, _Z_QA), cur(ATT_KV, _Z_KA), prev(ATT_KV, _Z_KA),
            cur(ATT_KV, _Z_VA), prev(ATT_KV, _Z_VA),
        ],
        out_specs=pl.BlockSpec((w, ATT_Q), lambda b, i: (b * nb + i, 0)),
        out_shape=jax.ShapeDtypeStruct((rows, ATT_Q), BF16),
        compiler_params=_cparams(("parallel", "parallel")),
        name="swa_prompt",
    )(sinks3, z, z, z, z, z)


def _swa_sample_kernel(q_ref, kn_ref, vn_ref, kp_ref, vp_ref, sink_ref,
                       o_ref, ko_ref, vo_ref, *, bb):
    w = WINDOW
    hrow = lax.broadcasted_iota(jnp.int32, (N_Q, ATT_KV), 0) // GROUP
    glane = lax.broadcasted_iota(jnp.int32, (N_Q, ATT_KV), 1) // HD
    own = hrow == glane
    srow = lax.broadcasted_iota(jnp.int32, (w, ATT_KV), 0)
    sink = sink_ref[...]
    for b in range(bb):
        q = q_ref[b]
        qm = jnp.where(own, jnp.concatenate([q] * N_KV, axis=1), 0.0)
        k_past = kp_ref[b]
        v_past = vp_ref[b]
        k_new = kn_ref[b]
        v_new = vn_ref[b]
        s_past = _dot_nt(qm.astype(BF16), k_past.astype(BF16))
        s_new = jnp.sum(qm * k_new, axis=1, keepdims=True)
        m = jnp.maximum(jnp.maximum(jnp.max(s_past, axis=1, keepdims=True), s_new), sink)
        e_past = jnp.exp(s_past - m)
        e_new = jnp.exp(s_new - m)
        denom = jnp.sum(e_past, axis=1, keepdims=True) + e_new + jnp.exp(sink - m)
        r = 1.0 / denom
        pv = _dot((e_past * r).astype(BF16), v_past.astype(BF16)) + (e_new * r) * v_new
        pv = jnp.where(own, pv, 0.0)
        o = pv[:, 0:HD]
        for g in range(1, N_KV):
            o = o + pv[:, g * HD:(g + 1) * HD]
        o_ref[b] = o
        ko_ref[b] = jnp.where(srow == w - 1, k_new, pltpu.roll(k_past, w - 1, axis=0))
        vo_ref[b] = jnp.where(srow == w - 1, v_new, pltpu.roll(v_past, w - 1, axis=0))


def _swa_sample(q3, k_new3, v_new3, k_past, v_past, sink_col, bb):
    nb = q3.shape[0]
    w = WINDOW
    return pl.pallas_call(
        functools.partial(_swa_sample_kernel, bb=bb),
        grid=(nb // bb,),
        in_specs=[
            pl.BlockSpec((bb, N_Q, HD), lambda i: (i, 0, 0)),
            pl.BlockSpec((bb, 1, ATT_KV), lambda i: (i, 0, _Z_KA // ATT_KV)),
            pl.BlockSpec((bb, 1, ATT_KV), lambda i: (i, 0, _Z_VA // ATT_KV)),
            pl.BlockSpec((bb, w, ATT_KV), lambda i: (i, 0, 0)),
            pl.BlockSpec((bb, w, ATT_KV), lambda i: (i, 0, 0)),
            pl.BlockSpec((N_Q, 1), lambda i: (0, 0)),
        ],
        out_specs=[
            pl.BlockSpec((bb, N_Q, HD), lambda i: (i, 0, 0)),
            pl.BlockSpec((bb, w, ATT_KV), lambda i: (i, 0, 0)),
            pl.BlockSpec((bb, w, ATT_KV), lambda i: (i, 0, 0)),
        ],
        out_shape=[
            jax.ShapeDtypeStruct((nb, N_Q, HD), F32),
            jax.ShapeDtypeStruct((nb, w, ATT_KV), F32),
            jax.ShapeDtypeStruct((nb, w, ATT_KV), F32),
        ],
        compiler_params=_cparams(("parallel",)),
        name="swa_sample",
    )(q3, k_new3, v_new3, k_past, v_past, sink_col)


def _merge1_kernel(hm_ref, ha_ref, ga_ref, gb_ref, wa_ref, wb_ref, t_ref):
    a = _dot(hm_ref[...], wa_ref[...])
    b = _dot(ha_ref[...], wb_ref[...])
    t_ref[...] = (_sigmoid(ga_ref[...]) * a + _sigmoid(gb_ref[...]) * b).astype(BF16)


def _merge1(hm, ha, z, wa, wb, tm, tn):
    rows, d = hm.shape
    n = wa.shape[1]
    return pl.pallas_call(
        _merge1_kernel,
        grid=(rows // tm, n // tn),
        in_specs=[
            pl.BlockSpec((tm, d), lambda i, j: (i, 0)),
            pl.BlockSpec((tm, d), lambda i, j: (i, 0)),
            pl.BlockSpec((tm, tn), lambda i, j: (i, _Z_GA // tn + j)),
            pl.BlockSpec((tm, tn), lambda i, j: (i, _Z_GB // tn + j)),
            pl.BlockSpec((d, tn), lambda i, j: (0, j)),
            pl.BlockSpec((d, tn), lambda i, j: (0, j)),
        ],
        out_specs=pl.BlockSpec((tm, tn), lambda i, j: (i, j)),
        out_shape=jax.ShapeDtypeStruct((rows, n), BF16),
        compiler_params=_cparams(("parallel", "arbitrary")),
        name="merge1",
    )(hm, ha, z, z, wa, wb)


def _merge2_kernel(t_ref, w_ref, x_ref, g1_ref, g2_ref, x1_ref, h2_ref):
    mix = _dot(t_ref[...], w_ref[...])
    x1 = x_ref[...] + _rms(mix, g1_ref[...])
    x1_ref[...] = x1
    h2_ref[...] = _rms(x1, g2_ref[...]).astype(BF16)


def _merge2(t, w_out, x2d, g1, g2, tm):
    rows, d = x2d.shape
    row = lambda i: (i, 0)
    const = lambda i: (0, 0)
    return pl.pallas_call(
        _merge2_kernel,
        grid=(rows // tm,),
        in_specs=[
            pl.BlockSpec((tm, d), row),
            pl.BlockSpec((d, d), const, pipeline_mode=pl.Buffered(1)),
            pl.BlockSpec((tm, d), row),
            pl.BlockSpec((1, d), const),
            pl.BlockSpec((1, d), const),
        ],
        out_specs=[pl.BlockSpec((tm, d), row), pl.BlockSpec((tm, d), row)],
        out_shape=[jax.ShapeDtypeStruct((rows, d), F32), jax.ShapeDtypeStruct((rows, d), BF16)],
        compiler_params=_cparams(("parallel",)),
        name="merge2",
    )(t, w_out, x2d, g1.reshape(1, d), g2.reshape(1, d))


def _gelu_tanh(x):
    return 0.5 * x * (1.0 + jnp.tanh(0.7978845608028654 * (x + 0.044715 * (x * x * x))))


def _conv_taps(u, u1, u2, cw_ref, cb_ref):
    return cb_ref[...] + u2 * cw_ref[0:1, :] + u1 * cw_ref[1:2, :] + u * cw_ref[2:3, :]


def _ffn_tail(j, nf, y, wd_ref, x1_ref, g_ref, y_ref, acc_ref):
    @pl.when(j == 0)
    def _():
        acc_ref[...] = _dot(y, wd_ref[...])

    @pl.when(j > 0)
    def _():
        acc_ref[...] += _dot(y, wd_ref[...])

    @pl.when(j == nf - 1)
    def _():
        y_ref[...] = x1_ref[...] + _rms(acc_ref[...], g_ref[...])


def _ffn_prompt_kernel(h_ref, wg_ref, wv_ref, cwg_ref, cwv_ref, cbg_ref, cbv_ref, wd_ref,
                       x1_ref, g_ref, y_ref, sg_ref, sv_ref, acc_ref, carry_ref,
                       *, tiles_per_seq):
    i = pl.program_id(0)
    j = pl.program_id(1)
    nf = pl.num_programs(1)
    tm = h_ref.shape[0]

    @pl.when((i == 0) & (j == 0))
    def _():
        carry_ref[...] = jnp.zeros_like(carry_ref)

    seq_start = (i % tiles_per_seq) == 0
    h = h_ref[...]

    def branch(w_ref, cw_ref, cb_ref, slot, state_ref):
        u = _dot(h, w_ref[...])
        prev = jnp.where(seq_start, 0.0, carry_ref[slot])
        uu = jnp.concatenate([prev, u], axis=0)
        u1 = uu[7:7 + tm, :]
        u2 = uu[6:6 + tm, :]
        tail = u[tm - 8:tm, :]
        carry_ref[slot] = tail
        state_ref[0] = tail
        return _conv_taps(u, u1, u2, cw_ref, cb_ref)

    cg = branch(wg_ref, cwg_ref, cbg_ref, 2 * j, sg_ref)
    cv = branch(wv_ref, cwv_ref, cbv_ref, 2 * j + 1, sv_ref)
    y = (_gelu_tanh(cg) * cv).astype(BF16)
    _ffn_tail(j, nf, y, wd_ref, x1_ref, g_ref, y_ref, acc_ref)


def _ffn_sample_kernel(h_ref, wg_ref, wv_ref, cwg_ref, cwv_ref, cbg_ref, cbv_ref, wd_ref,
                       x1_ref, g_ref, p2g_ref, p2v_ref, p1g_ref, p1v_ref,
                       y_ref, ug_ref, uv_ref, acc_ref):
    j = pl.program_id(1)
    nf = pl.num_programs(1)
    h = h_ref[...]
    ug = _dot(h, wg_ref[...])
    uv = _dot(h, wv_ref[...])
    ug_ref[...] = ug
    uv_ref[...] = uv
    cg = _conv_taps(ug, p1g_ref[...], p2g_ref[...], cwg_ref, cbg_ref)
    cv = _conv_taps(uv, p1v_ref[...], p2v_ref[...], cwv_ref, cbv_ref)
    y = (_gelu_tanh(cg) * cv).astype(BF16)
    _ffn_tail(j, nf, y, wd_ref, x1_ref, g_ref, y_ref, acc_ref)


def _ffn(h2, x1, w_up, w_conv, b_conv2d, w_down, g, tm, tf, seq=None, past=None):
    rows, d = h2.shape
    f = w_down.shape[0]
    nf = f // tf
    row = lambda i, j: (i, 0)
    gcol = lambda i, j: (0, j)
    vcol = lambda i, j: (0, nf + j)
    in_specs = [
        pl.BlockSpec((tm, d), row),
        pl.BlockSpec((d, tf), gcol), pl.BlockSpec((d, tf), vcol),
        pl.BlockSpec((3, tf), gcol), pl.BlockSpec((3, tf), vcol),
        pl.BlockSpec((1, tf), gcol), pl.BlockSpec((1, tf), vcol),
        pl.BlockSpec((tf, d), lambda i, j: (j, 0)),
        pl.BlockSpec((tm, d), row),
        pl.BlockSpec((1, d), lambda i, j: (0, 0)),
    ]
    args = [h2, w_up, w_up, w_conv, w_conv, b_conv2d, b_conv2d, w_down, x1, g.reshape(1, d)]
    scratch = [pltpu.VMEM((tm, d), F32)]
    if past is None:
        tiles_per_seq = seq // tm
        kern = functools.partial(_ffn_prompt_kernel, tiles_per_seq=tiles_per_seq)
        state_spec = pl.BlockSpec((1, 8, tf), lambda i, j: (i, 0, j))
        out_specs = [pl.BlockSpec((tm, d), row), state_spec, state_spec]
        out_shape = [jax.ShapeDtypeStruct((rows, d), F32),
                     jax.ShapeDtypeStruct((rows // tm, 8, f), F32),
                     jax.ShapeDtypeStruct((rows // tm, 8, f), F32)]
        scratch.append(pltpu.VMEM((2 * nf, 8, tf), F32))
        name = "ffn_prompt"
    else:
        p2, p1 = past
        kern = _ffn_sample_kernel
        ucol = pl.BlockSpec((tm, tf), lambda i, j: (i, j))
        in_specs += [
            pl.BlockSpec((tm, tf), lambda i, j: (i, j)), pl.BlockSpec((tm, tf), lambda i, j: (i, nf + j)),
            pl.BlockSpec((tm, tf), lambda i, j: (i, j)), pl.BlockSpec((tm, tf), lambda i, j: (i, nf + j)),
        ]
        args += [p2, p2, p1, p1]
        out_specs = [pl.BlockSpec((tm, d), row), ucol, ucol]
        out_shape = [jax.ShapeDtypeStruct((rows, d), F32),
                     jax.ShapeDtypeStruct((rows, f), F32),
                     jax.ShapeDtypeStruct((rows, f), F32)]
        name = "ffn_sample"
    return pl.pallas_call(
        kern,
        grid=(rows // tm, nf),
        in_specs=in_specs,
        out_specs=out_specs,
        out_shape=out_shape,
        scratch_shapes=scratch,
        compiler_params=_cparams(("arbitrary", "arbitrary")),
        name=name,
    )(*args)


def _regroup_w_in(w_in):
    d = w_in.shape[0]
    o_if = QK_W * 2 + V_W * 2
    o_qa = o_if + 2 * N_MH
    o_ka = o_qa + ATT_Q
    o_ga = o_ka + 2 * ATT_KV
    parts = [
        w_in[:, :o_if],
        w_in[:, o_qa:o_ka],
        w_in[:, o_ga:],
        w_in[:, o_ka:o_ga],
        w_in[:, o_if:o_qa],
    ]
    used = sum(p.shape[1] for p in parts)
    parts.append(jnp.zeros((d, _Z_W - used), w_in.dtype))
    return jnp.concatenate(parts, axis=1).astype(BF16)


def _pick(rows, pref):
    t = min(rows, pref)
    while rows % t:
        t //= 2
    return t


def kernel(x_prompt, x_sample, state_mlstm_c, state_mlstm_n, state_mlstm_m, cache_swa_k, cache_swa_v,
           state_ffn_conv, g_pre_mix, w_in, b_if, attn_sinks, w_branch_a, w_branch_b, w_out, g_post_mix,
           g_pre_ffn, w_up, w_conv, b_conv, w_down, g_post_ffn):
    bp, tp, d = x_prompt.shape
    bs, ts, _ = x_sample.shape
    depth = w_in.shape[0]
    assert depth == 1 and ts == 1 and d == D_MODEL
    f = w_down.shape[1]
    tf = 512
    assert f % tf == 0

    l = 0
    w_all = _regroup_w_in(w_in[l])
    wa = w_branch_a[l].astype(BF16)
    wb = w_branch_b[l].astype(BF16)
    wo = w_out[l].astype(BF16)
    wu = w_up[l].astype(BF16)
    wd = w_down[l].astype(BF16)
    bias_row = jnp.concatenate([b_if[l].astype(F32), jnp.zeros((LANES - 2 * N_MH,), F32)]).reshape(1, LANES)
    sinks = attn_sinks[l].astype(F32)
    b_conv2d = b_conv[l].reshape(1, 2 * f)

    xp = x_prompt.reshape(bp * tp, d)
    zp = _project(xp, g_pre_mix[l], w_all, _rope_tables(jnp.arange(tp)), _pick(tp, 1024))
    hm_p3, c_p, n_p, m_p = _mlstm_prompt(zp, bias_row, bp, tp, _pick(bp, 4))
    hm_p = hm_p3.reshape(bp * tp, V_W)
    ha_p = _swa_prompt(zp, sinks, bp, tp)
    t_p = _merge1(hm_p, ha_p, zp, wa, wb, _pick(bp * tp, 1024), 512)
    x1_p, h2_p = _merge2(t_p, wo, xp, g_post_mix[l], g_pre_ffn[l], _pick(bp * tp, 512))
    tm_ffn = _pick(tp, 512)
    y_p, sg_p, sv_p = _ffn(h2_p, x1_p, wu, w_conv[l], b_conv2d, wd, g_post_ffn[l],
                           tm_ffn, tf, seq=tp)
    seq_end = slice(tp // tm_ffn - 1, None, tp // tm_ffn)

    y_prompt = y_p.reshape(bp, tp, d)
    c_prompt = c_p[None]
    n_prompt = n_p[None]
    m_prompt = m_p[:, :N_MH, 0][None]
    kv_tail = zp.reshape(bp, tp, _Z_W)[:, tp - WINDOW:, _Z_KA:_Z_KA + 2 * ATT_KV]
    k_prompt = kv_tail[..., :ATT_KV].reshape(bp, WINDOW, N_KV, HD)[None]
    v_prompt = kv_tail[..., ATT_KV:].reshape(bp, WINDOW, N_KV, HD)[None]
    conv_prompt = jnp.concatenate([sg_p[seq_end, 6:8, :], sv_p[seq_end, 6:8, :]], axis=-1)[None]

    xs = x_sample.reshape(bs, d)
    tabs_s = tuple(jnp.broadcast_to(a, (bs, LANES)) for a in _rope_tables(PAST_LEN + jnp.arange(ts)))
    zs = _project(xs, g_pre_mix[l], w_all, tabs_s, bs)
    zs3 = zs.reshape(bs, 1, _Z_W)
    m0_pad = jnp.pad(state_mlstm_m[l].astype(F32), ((0, 0), (0, LANES - N_MH))).reshape(bs, 1, LANES)
    hm_s3, c_sample, n_sample, m_s3 = _mlstm_sample(zs3, bias_row, state_mlstm_c, state_mlstm_n, m0_pad, l)
    o_s3, k_s, v_s = _swa_sample(
        zs[:, _Z_QA:_Z_QA + ATT_Q].reshape(bs, N_Q, HD), zs3, zs3,
        cache_swa_k[l].reshape(bs, WINDOW, ATT_KV), cache_swa_v[l].reshape(bs, WINDOW, ATT_KV),
        sinks.reshape(N_Q, 1), _pick(bs, 8))
    ha_s = o_s3.reshape(bs, ATT_Q).astype(BF16)
    t_s = _merge1(hm_s3.reshape(bs, V_W), ha_s, zs, wa, wb, bs, 512)
    x1_s, h2_s = _merge2(t_s, wo, xs, g_post_mix[l], g_pre_ffn[l], bs)
    past = state_ffn_conv[l].astype(F32)
    y_s, ug_s, uv_s = _ffn(h2_s, x1_s, wu, w_conv[l], b_conv2d, wd, g_post_ffn[l],
                           bs, tf, past=(past[:, 0, :], past[:, 1, :]))

    y_sample = y_s.reshape(bs, ts, d)
    m_sample = m_s3[:, 0, :N_MH][None]
    k_sample = k_s.reshape(bs, WINDOW, N_KV, HD)[None]
    v_sample = v_s.reshape(bs, WINDOW, N_KV, HD)[None]
    conv_sample = jnp.stack([past[:, 1, :], jnp.concatenate([ug_s, uv_s], axis=-1)], axis=1)[None]

    return (y_prompt, y_sample,
            c_prompt, n_prompt, m_prompt, k_prompt, v_prompt, conv_prompt,
            c_sample, n_sample, m_sample, k_sample, v_sample, conv_sample)
```

```python
import functools

import jax
import jax.numpy as jnp
from jax import lax
from jax.experimental import pallas as pl
from jax.experimental.pallas import tpu as pltpu

F32 = jnp.float32
BF16 = jnp.bfloat16

N_MH = 4
DK = 256
DV = 512
CHUNK = 64
HD = 64
N_Q = 32
N_KV = 4
GROUP = N_Q // N_KV
WINDOW = 128
ROT = 16
THETA = 500000.0
PAST_LEN = 16384
EPS = 1e-6
D_MODEL = 2048
QK_W = N_MH * DK
V_W = N_MH * DV
ATT_Q = N_Q * HD
ATT_KV = N_KV * HD

_Z_QM = 0
_Z_KM = _Z_QM + QK_W
_Z_VM = _Z_KM + QK_W
_Z_OM = _Z_VM + V_W
_Z_QA = _Z_OM + V_W
_Z_GA = _Z_QA + ATT_Q
_Z_GB = _Z_GA + D_MODEL
_Z_KA = _Z_GB + D_MODEL
_Z_VA = _Z_KA + ATT_KV
_Z_IF = _Z_VA + ATT_KV
LANES = 128
_PROJ_TN = 1024
_Z_W = -(-(_Z_IF + LANES) // _PROJ_TN) * _PROJ_TN
assert _Z_QA % _PROJ_TN == 0 and _Z_GA % _PROJ_TN == 0 and _Z_KA % _PROJ_TN == 0

_VMEM_LIMIT = 56 * 1024 * 1024


def _cparams(sem):
    return pltpu.CompilerParams(dimension_semantics=sem, vmem_limit_bytes=_VMEM_LIMIT)


def _rms(x, g):
    return x * lax.rsqrt(jnp.mean(x * x, axis=-1, keepdims=True) + EPS) * g


def _sigmoid(x):
    return 1.0 / (1.0 + jnp.exp(-x))


def _log_sigmoid(x):
    return jnp.minimum(x, 0.0) - jnp.log1p(jnp.exp(-jnp.abs(x)))


def _dot(a, b):
    return jnp.dot(a, b, preferred_element_type=F32)


def _dot_nt(a, b):
    return lax.dot_general(a, b, (((1,), (1,)), ((), ())), preferred_element_type=F32)


def _dot_tn(a, b):
    return lax.dot_general(a, b, (((0,), (0,)), ((), ())), preferred_element_type=F32)


def _proj_kernel(x_ref, g_ref, wlo_ref, whi_ref, cos_ref, sa_ref, sb_ref, z_ref, h_ref):
    j = pl.program_id(1)
    jq0 = _Z_QA // _PROJ_TN
    jq1 = _Z_GA // _PROJ_TN
    jk = _Z_KA // _PROJ_TN

    @pl.when(j == 0)
    def _():
        h_ref[...] = _rms(x_ref[...], g_ref[...]).astype(BF16)

    is_lo = j < jq0
    is_q = (j >= jq0) & (j < jq1)
    is_k = j == jk

    @pl.when(is_lo)
    def _():
        z_ref[...] = _dot(h_ref[...], wlo_ref[...])

    @pl.when(is_q)
    def _():
        acc = _dot(h_ref[...], whi_ref[...])
        z_ref[...] = _rope(acc, cos_ref[...], sa_ref[...], sb_ref[...]) * (HD ** -0.5)

    @pl.when(is_k)
    def _():
        acc = _dot(h_ref[...], whi_ref[...])
        z_ref[...] = acc
        z_ref[:, :ATT_KV] = _rope(acc[:, :ATT_KV], cos_ref[...], sa_ref[...], sb_ref[...])

    @pl.when(jnp.logical_not(is_lo | is_q | is_k))
    def _():
        z_ref[...] = _dot(h_ref[...], whi_ref[...])


def _project(x2d, g, w_lo, w_hi, tables, tm):
    rows, d = x2d.shape
    tn = _PROJ_TN
    n_lo = w_lo.shape[1] // tn
    n = w_lo.shape[1] + w_hi.shape[1]
    assert n == _Z_W and n_lo == _Z_QA // tn
    tab_tiles = tables[0].shape[0] // tm
    tab = pl.BlockSpec((tm, LANES), lambda i, j: (i % tab_tiles, 0))
    return pl.pallas_call(
        _proj_kernel,
        grid=(rows // tm, n // tn),
        in_specs=[
            pl.BlockSpec((tm, d), lambda i, j: (i, 0)),
            pl.BlockSpec((1, d), lambda i, j: (0, 0)),
            pl.BlockSpec((d, tn), lambda i, j: (0, jnp.minimum(j, n_lo - 1))),
            pl.BlockSpec((d, tn), lambda i, j: (0, jnp.maximum(j - n_lo, 0))),
            tab, tab, tab,
        ],
        out_specs=pl.BlockSpec((tm, tn), lambda i, j: (i, j)),
        out_shape=jax.ShapeDtypeStruct((rows, n), F32),
        scratch_shapes=[pltpu.VMEM((tm, d), BF16)],
        compiler_params=_cparams(("parallel", "arbitrary")),
        name="proj",
    )(x2d, g.reshape(1, d), w_lo, w_hi, *tables)


def _mlstm_prompt_kernel(q_ref, k_ref, v_ref, o_ref, g_ref, bias_ref,
                         hm_ref, c_out, n_out, m_out, ct_s, n_s, m_s):
    c = pl.program_id(1)
    last = pl.num_programs(1) - 1
    L = CHUNK
    nseq = q_ref.shape[0]

    @pl.when(c == 0)
    def _():
        ct_s[...] = jnp.zeros_like(ct_s)
        n_s[...] = jnp.zeros_like(n_s)
        m_s[...] = jnp.zeros_like(m_s)

    row = lax.broadcasted_iota(jnp.int32, (L, LANES), 0)
    ri = lax.broadcasted_iota(jnp.int32, (L, L), 0)
    ci = lax.broadcasted_iota(jnp.int32, (L, L), 1)
    eye = ri == ci
    causal = ci <= ri

    heads = []
    for bi in range(nseq):
        gates = g_ref[bi] + bias_ref[...]
        bcum = _log_sigmoid(gates)
        for d in (1, 2, 4, 8, 16, 32):
            bcum = bcum + jnp.where(row >= d, pltpu.roll(bcum, d, axis=0), 0.0)
        for h in range(N_MH):
            q = q_ref[bi, :, h * DK:(h + 1) * DK]
            k = k_ref[bi, :, h * DK:(h + 1) * DK] * (DK ** -0.5)
            qb = q.astype(BF16)
            ct = ct_s[bi * N_MH + h]
            heads.append((bi, h, gates, bcum, q, k, _dot_nt(qb, k.astype(BF16)), _dot(qb, ct.astype(BF16))))

    for bi, h, gates, bcum, q, k, qk, cq in heads:
        v = v_ref[bi, :, h * DV:(h + 1) * DV]
        o = o_ref[bi, :, h * DV:(h + 1) * DV]
        b_col = bcum[:, N_MH + h:N_MH + h + 1]
        ig_col = gates[:, h:h + 1]
        a_col = ig_col - b_col
        a_lane = jnp.sum(jnp.where(eye, a_col, 0.0), axis=0, keepdims=True)
        dmat = jnp.where(causal, b_col + a_lane, -jnp.inf)
        m_prev = m_s[bi, h:h + 1, 0:1]
        inter = b_col + m_prev
        m_t = jnp.maximum(jnp.max(dmat, axis=1, keepdims=True), inter)
        vb = v.astype(BF16)
        s = qk * jnp.exp(dmat - m_t)
        w_inter = jnp.exp(inter - m_t)
        m_new = m_t[L - 1:L, :]
        b_last = b_col[L - 1:L, :]
        w_last = jnp.exp(b_last - b_col + ig_col - m_new)
        decay = jnp.exp(b_last + m_prev - m_new)
        kw = k * w_last
        ct = ct_s[bi * N_MH + h]
        ct_s[bi * N_MH + h] = decay * ct + _dot_tn(kw.astype(BF16), vb)
        num = _dot(s.astype(BF16), vb) + w_inter * cq
        n_row = n_s[bi, h:h + 1, :]
        den = jnp.sum(s, axis=1, keepdims=True) + w_inter * jnp.sum(q * n_row, axis=1, keepdims=True)
        hout = num / jnp.maximum(jnp.abs(den), jnp.exp(-m_t))
        hm_ref[bi, :, h * DV:(h + 1) * DV] = (_sigmoid(o) * hout).astype(BF16)
        n_s[bi, h:h + 1, :] = decay * n_row + jnp.sum(kw, axis=0, keepdims=True)
        m_s[bi, h:h + 1, :] = jnp.broadcast_to(m_new, (1, LANES))

    @pl.when(c == last)
    def _():
        for bi in range(nseq):
            for h in range(N_MH):
                c_out[bi, h] = ct_s[bi * N_MH + h].T
        n_out[...] = n_s[:, :N_MH, :]
        m_out[...] = m_s[...]


def _mlstm_prompt(z, bias_row, batch, seq, bb):
    nc = seq // CHUNK
    z3 = z.reshape(batch, seq, z.shape[1])

    def zspec(width, off):
        blk = off // width
        return pl.BlockSpec((bb, CHUNK, width), lambda b, c: (b, c, blk))

    return pl.pallas_call(
        _mlstm_prompt_kernel,
        grid=(batch // bb, nc),
        in_specs=[
            zspec(QK_W, _Z_QM), zspec(QK_W, _Z_KM), zspec(V_W, _Z_VM), zspec(V_W, _Z_OM),
            zspec(LANES, _Z_IF),
            pl.BlockSpec((1, LANES), lambda b, c: (0, 0)),
        ],
        out_specs=[
            pl.BlockSpec((bb, CHUNK, V_W), lambda b, c: (b, c, 0)),
            pl.BlockSpec((bb, N_MH, DV, DK), lambda b, c: (b, 0, 0, 0)),
            pl.BlockSpec((bb, N_MH, DK), lambda b, c: (b, 0, 0)),
            pl.BlockSpec((bb, 8, LANES), lambda b, c: (b, 0, 0)),
        ],
        out_shape=[
            jax.ShapeDtypeStruct((batch, seq, V_W), BF16),
            jax.ShapeDtypeStruct((batch, N_MH, DV, DK), F32),
            jax.ShapeDtypeStruct((batch, N_MH, DK), F32),
            jax.ShapeDtypeStruct((batch, 8, LANES), F32),
        ],
        scratch_shapes=[
            pltpu.VMEM((bb * N_MH, DK, DV), F32),
            pltpu.VMEM((bb, 8, DK), F32),
            pltpu.VMEM((bb, 8, LANES), F32),
        ],
        compiler_params=_cparams(("parallel", "arbitrary")),
        name="mlstm_prompt",
    )(z3, z3, z3, z3, z3, bias_row)


def _mlstm_sample_kernel(q_ref, k_ref, v_ref, o_ref, g_ref, bias_ref, c_ref, n_ref, m_ref,
                         hm_ref, c_out, n_out, m_out):
    gates = g_ref[0] + bias_ref[...]
    lf_all = _log_sigmoid(gates)
    m_all = m_ref[0]
    ri = lax.broadcasted_iota(jnp.int32, (DV, DV), 0)
    ci = lax.broadcasted_iota(jnp.int32, (DV, DV), 1)
    eye = ri == ci
    m_new_all = jnp.zeros((1, LANES), F32)
    lane = lax.broadcasted_iota(jnp.int32, (1, LANES), 1)

    for h in range(N_MH):
        q = q_ref[0, :, h * DK:(h + 1) * DK]
        k = k_ref[0, :, h * DK:(h + 1) * DK] * (DK ** -0.5)
        v = v_ref[0, :, h * DV:(h + 1) * DV]
        o = o_ref[0, :, h * DV:(h + 1) * DV]
        ig = gates[:, h:h + 1]
        lf = lf_all[:, N_MH + h:N_MH + h + 1]
        m_prev = m_all[:, h:h + 1]
        inter = lf + m_prev
        m_t = jnp.maximum(ig, inter)
        s = jnp.sum(q * k, axis=1, keepdims=True) * jnp.exp(ig - m_t)
        w_inter = jnp.exp(inter - m_t)
        cmat = c_ref[0, 0, h]
        q8 = jnp.broadcast_to(q, (8, DK)).astype(BF16)
        cq = _dot_nt(q8, cmat.astype(BF16))[0:1, :]
        n_row = n_ref[0, 0, h:h + 1, :]
        num = s * v + w_inter * cq
        den = s + w_inter * jnp.sum(n_row * q, axis=1, keepdims=True)
        hout = num / jnp.maximum(jnp.abs(den), jnp.exp(-m_t))
        hm_ref[0, :, h * DV:(h + 1) * DV] = (_sigmoid(o) * hout).astype(BF16)

        w_last = jnp.exp(ig - m_t)
        decay = jnp.exp(inter - m_t)
        vw = v * w_last
        vw_col = jnp.sum(jnp.where(eye, vw, 0.0), axis=1, keepdims=True)
        c_out[0, 0, h] = decay * cmat + vw_col * k
        n_out[0, 0, h:h + 1, :] = decay * n_row + w_last * k
        m_new_all = jnp.where(lane == h, m_t, m_new_all)

    m_out[0] = m_new_all


def _mlstm_sample(z3, bias_row, c0, n0, m0_pad, layer):
    nb = z3.shape[0]

    def zspec(width, off):
        blk = off // width
        return pl.BlockSpec((1, 1, width), lambda b: (b, 0, blk))

    c_in = pl.BlockSpec((1, 1, N_MH, DV, DK), lambda b: (layer, b, 0, 0, 0))
    n_in = pl.BlockSpec((1, 1, N_MH, DK), lambda b: (layer, b, 0, 0))
    c_blk = pl.BlockSpec((1, 1, N_MH, DV, DK), lambda b: (0, b, 0, 0, 0))
    n_blk = pl.BlockSpec((1, 1, N_MH, DK), lambda b: (0, b, 0, 0))

    return pl.pallas_call(
        _mlstm_sample_kernel,
        grid=(nb,),
        in_specs=[
            zspec(QK_W, _Z_QM), zspec(QK_W, _Z_KM), zspec(V_W, _Z_VM), zspec(V_W, _Z_OM),
            zspec(LANES, _Z_IF),
            pl.BlockSpec((1, LANES), lambda b: (0, 0)),
            c_in,
            n_in,
            pl.BlockSpec((1, 1, LANES), lambda b: (b, 0, 0)),
        ],
        out_specs=[
            pl.BlockSpec((1, 1, V_W), lambda b: (b, 0, 0)),
            c_blk,
            n_blk,
            pl.BlockSpec((1, 1, LANES), lambda b: (b, 0, 0)),
        ],
        out_shape=[
            jax.ShapeDtypeStruct((nb, 1, V_W), BF16),
            jax.ShapeDtypeStruct((1, nb, N_MH, DV, DK), F32),
            jax.ShapeDtypeStruct((1, nb, N_MH, DK), F32),
            jax.ShapeDtypeStruct((nb, 1, LANES), F32),
        ],
        compiler_params=_cparams(("parallel",)),
        name="mlstm_sample",
    )(z3, z3, z3, z3, z3, bias_row, c0, n0, m0_pad)


def _rope_tables(pos):
    half = ROT // 2
    inv = THETA ** (-jnp.arange(half, dtype=F32) * 2.0 / ROT)
    ang = pos.astype(F32)[:, None] * inv[None, :]
    cos = jnp.cos(ang)
    sin = jnp.sin(ang)
    t = pos.shape[0]
    cos_t = jnp.concatenate([cos, cos, jnp.ones((t, HD - ROT), F32)], axis=1)
    sin_a = jnp.concatenate([-sin, jnp.zeros((t, HD - half), F32)], axis=1)
    sin_b = jnp.concatenate([jnp.zeros((t, half), F32), sin, jnp.zeros((t, HD - ROT), F32)], axis=1)
    rep = LANES // HD
    return tuple(jnp.tile(a, (1, rep)) for a in (cos_t, sin_a, sin_b))


def _rope(x, cos_t, sin_a, sin_b):
    w = x.shape[1]
    rep = w // LANES
    half = ROT // 2
    if rep > 1:
        cos_t = jnp.concatenate([cos_t] * rep, axis=1)
        sin_a = jnp.concatenate([sin_a] * rep, axis=1)
        sin_b = jnp.concatenate([sin_b] * rep, axis=1)
    up = pltpu.roll(x, w - half, axis=1)
    down = pltpu.roll(x, half, axis=1)
    return x * cos_t + up * sin_a + down * sin_b


def _swa_prompt_kernel(sink_ref, q_ref, kc_ref, kp_ref, vc_ref, vp_ref, ha_ref):
    i = pl.program_id(1)
    w = WINDOW
    kk = jnp.concatenate([kp_ref[...], kc_ref[...]], axis=0)
    vv = jnp.concatenate([vp_ref[...], vc_ref[...]], axis=0)
    vv_t = vv.T.astype(BF16)
    s_idx = lax.broadcasted_iota(jnp.int32, (2 * w, w), 0)
    a_idx = lax.broadcasted_iota(jnp.int32, (2 * w, w), 1)
    valid = (s_idx >= a_idx) & (s_idx <= a_idx + w) & ((i > 0) | (s_idx >= w))
    lane = lax.broadcasted_iota(jnp.int32, (2 * w, LANES), 1)
    pair_w = 2 * HD
    for half in range(N_KV // 2):
        kh = kk[:, half * pair_w:(half + 1) * pair_w]
        kh_sw = pltpu.roll(kh, HD, axis=1)
        for gg in range(2):
            g = 2 * half + gg
            k_lo = jnp.where(lane < HD, kh if gg == 0 else kh_sw, 0.0).astype(BF16)
            k_hi = jnp.where(lane >= HD, kh_sw if gg == 0 else kh, 0.0).astype(BF16)
            vg_t = vv_t[g * HD:(g + 1) * HD, :]
            sts = []
            for a in range(GROUP // 2):
                col = (g * GROUP + 2 * a) * HD
                qpair = q_ref[:, col:col + pair_w].astype(BF16)
                sts += [_dot_nt(k_lo, qpair), _dot_nt(k_hi, qpair)]
            es, rs = [], []
            for hh in range(GROUP):
                sink = sink_ref[g * GROUP + hh]
                st = jnp.where(valid, sts[hh], -jnp.inf)
                m = jnp.maximum(jnp.max(st, axis=0, keepdims=True), sink)
                e = jnp.exp(st - m)
                es.append(e.astype(BF16))
                rs.append(1.0 / (jnp.sum(e, axis=0, keepdims=True) + jnp.exp(sink - m)))
            ots = [_dot(vg_t, es[hh]) * rs[hh] for hh in range(GROUP)]
            for a in range(GROUP // 2):
                col = (g * GROUP + 2 * a) * HD
                ha_ref[:, col:col + pair_w] = jnp.concatenate(ots[2 * a:2 * a + 2], axis=0).T.astype(BF16)


def _swa_prompt(z, sinks3, batch, seq):
    rows = z.shape[0]
    nb = seq // WINDOW
    w = WINDOW

    def cur(width, off):
        blk = off // width
        return pl.BlockSpec((w, width), lambda b, i: (b * nb + i, blk))

    def prev(width, off):
        blk = off // width
        return pl.BlockSpec((w, width), lambda b, i: (b * nb + jnp.maximum(i - 1, 0), blk))

    return pl.pallas_call(
        _swa_prompt_kernel,
        grid=(batch, nb),
        in_specs=[
            pl.BlockSpec(memory_space=pltpu.SMEM),
            cur(ATT_Q, _Z_QA), cur(ATT_KV, _Z_KA), prev(ATT_KV, _Z_KA),
            cur(ATT_KV, _Z_VA), prev(ATT_KV, _Z_VA),
        ],
        out_specs=pl.BlockSpec((w, ATT_Q), lambda b, i: (b * nb + i, 0)),
        out_shape=jax.ShapeDtypeStruct((rows, ATT_Q), BF16),
        compiler_params=_cparams(("parallel", "parallel")),
        name="swa_prompt",
    )(sinks3, z, z, z, z, z)


def _swa_sample_kernel(q_ref, kn_ref, vn_ref, kp_ref, vp_ref, sink_ref,
                       o_ref, ko_ref, vo_ref, *, bb):
    w = WINDOW
    hrow = lax.broadcasted_iota(jnp.int32, (N_Q, ATT_KV), 0) // GROUP
    glane = lax.broadcasted_iota(jnp.int32, (N_Q, ATT_KV), 1) // HD
    own = hrow == glane
    srow = lax.broadcasted_iota(jnp.int32, (w, ATT_KV), 0)
    sink = sink_ref[...]
    for b in range(bb):
        q = q_ref[b]
        qm = jnp.where(own, jnp.concatenate([q] * N_KV, axis=1), 0.0)
        k_past = kp_ref[b]
        v_past = vp_ref[b]
        k_new = kn_ref[b]
        v_new = vn_ref[b]
        s_past = _dot_nt(qm.astype(BF16), k_past.astype(BF16))
        s_new = jnp.sum(qm * k_new, axis=1, keepdims=True)
        m = jnp.maximum(jnp.maximum(jnp.max(s_past, axis=1, keepdims=True), s_new), sink)
        e_past = jnp.exp(s_past - m)
        e_new = jnp.exp(s_new - m)
        denom = jnp.sum(e_past, axis=1, keepdims=True) + e_new + jnp.exp(sink - m)
        r = 1.0 / denom
        pv = _dot((e_past * r).astype(BF16), v_past.astype(BF16)) + (e_new * r) * v_new
        pv = jnp.where(own, pv, 0.0)
        o = pv[:, 0:HD]
        for g in range(1, N_KV):
            o = o + pv[:, g * HD:(g + 1) * HD]
        o_ref[b] = o
        ko_ref[b] = jnp.where(srow == w - 1, k_new, pltpu.roll(k_past, w - 1, axis=0))
        vo_ref[b] = jnp.where(srow == w - 1, v_new, pltpu.roll(v_past, w - 1, axis=0))


def _swa_sample(q3, k_new3, v_new3, k_past, v_past, sink_col, bb):
    nb = q3.shape[0]
    w = WINDOW
    return pl.pallas_call(
        functools.partial(_swa_sample_kernel, bb=bb),
        grid=(nb // bb,),
        in_specs=[
            pl.BlockSpec((bb, N_Q, HD), lambda i: (i, 0, 0)),
            pl.BlockSpec((bb, 1, ATT_KV), lambda i: (i, 0, _Z_KA // ATT_KV)),
            pl.BlockSpec((bb, 1, ATT_KV), lambda i: (i, 0, _Z_VA // ATT_KV)),
            pl.BlockSpec((bb, w, ATT_KV), lambda i: (i, 0, 0)),
            pl.BlockSpec((bb, w, ATT_KV), lambda i: (i, 0, 0)),
            pl.BlockSpec((N_Q, 1), lambda i: (0, 0)),
        ],
        out_specs=[
            pl.BlockSpec((bb, N_Q, HD), lambda i: (i, 0, 0)),
            pl.BlockSpec((bb, w, ATT_KV), lambda i: (i, 0, 0)),
            pl.BlockSpec((bb, w, ATT_KV), lambda i: (i, 0, 0)),
        ],
        out_shape=[
            jax.ShapeDtypeStruct((nb, N_Q, HD), F32),
            jax.ShapeDtypeStruct((nb, w, ATT_KV), F32),
            jax.ShapeDtypeStruct((nb, w, ATT_KV), F32),
        ],
        compiler_params=_cparams(("parallel",)),
        name="swa_sample",
    )(q3, k_new3, v_new3, k_past, v_past, sink_col)


def _merge1_kernel(hm_ref, ha_ref, ga_ref, gb_ref, wa_ref, wb_ref, t_ref):
    a = _dot(hm_ref[...], wa_ref[...])
    b = _dot(ha_ref[...], wb_ref[...])
    t_ref[...] = (_sigmoid(ga_ref[...]) * a + _sigmoid(gb_ref[...]) * b).astype(BF16)


def _merge1(hm, ha, z, wa, wb, tm, tn):
    rows, d = hm.shape
    n = wa.shape[1]
    return pl.pallas_call(
        _merge1_kernel,
        grid=(rows // tm, n // tn),
        in_specs=[
            pl.BlockSpec((tm, d), lambda i, j: (i, 0)),
            pl.BlockSpec((tm, d), lambda i, j: (i, 0)),
            pl.BlockSpec((tm, tn), lambda i, j: (i, _Z_GA // tn + j)),
            pl.BlockSpec((tm, tn), lambda i, j: (i, _Z_GB // tn + j)),
            pl.BlockSpec((d, tn), lambda i, j: (0, j)),
            pl.BlockSpec((d, tn), lambda i, j: (0, j)),
        ],
        out_specs=pl.BlockSpec((tm, tn), lambda i, j: (i, j)),
        out_shape=jax.ShapeDtypeStruct((rows, n), BF16),
        compiler_params=_cparams(("parallel", "arbitrary")),
        name="merge1",
    )(hm, ha, z, z, wa, wb)


def _merge2_kernel(t_ref, w_ref, x_ref, g1_ref, g2_ref, x1_ref, h2_ref):
    mix = _dot(t_ref[...], w_ref[...])
    x1 = x_ref[...] + _rms(mix, g1_ref[...])
    x1_ref[...] = x1
    h2_ref[...] = _rms(x1, g2_ref[...]).astype(BF16)


def _merge2(t, w_out, x2d, g1, g2, tm):
    rows, d = x2d.shape
    row = lambda i: (i, 0)
    const = lambda i: (0, 0)
    return pl.pallas_call(
        _merge2_kernel,
        grid=(rows // tm,),
        in_specs=[
            pl.BlockSpec((tm, d), row),
            pl.BlockSpec((d, d), const, pipeline_mode=pl.Buffered(1)),
            pl.BlockSpec((tm, d), row),
            pl.BlockSpec((1, d), const),
            pl.BlockSpec((1, d), const),
        ],
        out_specs=[pl.BlockSpec((tm, d), row), pl.BlockSpec((tm, d), row)],
        out_shape=[jax.ShapeDtypeStruct((rows, d), F32), jax.ShapeDtypeStruct((rows, d), BF16)],
        compiler_params=_cparams(("parallel",)),
        name="merge2",
    )(t, w_out, x2d, g1.reshape(1, d), g2.reshape(1, d))


def _gelu_tanh(x):
    return 0.5 * x * (1.0 + jnp.tanh(0.7978845608028654 * (x + 0.044715 * (x * x * x))))


def _conv_taps(u, u1, u2, cw_ref, cb_ref, cs=slice(None)):
    return cb_ref[:, cs] + u2 * cw_ref[0:1, cs] + u1 * cw_ref[1:2, cs] + u * cw_ref[2:3, cs]


def _ffn_tail(j, nf, y, wd_ref, x1_ref, g_ref, y_ref, acc_ref):
    @pl.when(j == 0)
    def _():
        acc_ref[...] = _dot(y, wd_ref[...])

    @pl.when(j > 0)
    def _():
        acc_ref[...] += _dot(y, wd_ref[...])

    @pl.when(j == nf - 1)
    def _():
        y_ref[...] = x1_ref[...] + _rms(acc_ref[...], g_ref[...])


def _ffn_prompt_kernel(h_ref, wg_ref, wv_ref, cwg_ref, cwv_ref, cbg_ref, cbv_ref, wd_ref,
                       x1_ref, g_ref, y_ref, sg_ref, sv_ref, acc_ref, carry_ref,
                       *, tiles_per_seq):
    i = pl.program_id(0)
    j = pl.program_id(1)
    nf = pl.num_programs(1)
    tm = h_ref.shape[0]

    @pl.when((i == 0) & (j == 0))
    def _():
        carry_ref[...] = jnp.zeros_like(carry_ref)

    seq_start = (i % tiles_per_seq) == 0
    h = h_ref[...]

    def branch(w_ref, cw_ref, cb_ref, slot, state_ref):
        u = _dot(h, w_ref[...])
        prev = jnp.where(seq_start, 0.0, carry_ref[slot])
        uu = jnp.concatenate([prev, u], axis=0)
        u1 = uu[7:7 + tm, :]
        u2 = uu[6:6 + tm, :]
        tail = u[tm - 8:tm, :]
        carry_ref[slot] = tail
        state_ref[0] = tail
        return _conv_taps(u, u1, u2, cw_ref, cb_ref)

    cg = branch(wg_ref, cwg_ref, cbg_ref, 2 * j, sg_ref)
    cv = branch(wv_ref, cwv_ref, cbv_ref, 2 * j + 1, sv_ref)
    y = (_gelu_tanh(cg) * cv).astype(BF16)
    _ffn_tail(j, nf, y, wd_ref, x1_ref, g_ref, y_ref, acc_ref)


def _ffn_sample_kernel(h_ref, wg_ref, wv_ref, cwg_ref, cwv_ref, cbg_ref, cbv_ref, wd_ref,
                       x1_ref, g_ref, p2g_ref, p2v_ref, p1g_ref, p1v_ref,
                       y_ref, ug_ref, uv_ref, acc_ref):
    j = pl.program_id(1)
    nf = pl.num_programs(1)
    h = h_ref[...]
    ug = _dot(h, wg_ref[...])
    uv = _dot(h, wv_ref[...])
    ug_ref[...] = ug
    uv_ref[...] = uv
    cg = _conv_taps(ug, p1g_ref[...], p2g_ref[...], cwg_ref, cbg_ref)
    cv = _conv_taps(uv, p1v_ref[...], p2v_ref[...], cwv_ref, cbv_ref)
    y = (_gelu_tanh(cg) * cv).astype(BF16)
    _ffn_tail(j, nf, y, wd_ref, x1_ref, g_ref, y_ref, acc_ref)


def _ffn(h2, x1, w_up, w_conv, b_conv2d, w_down, g, tm, tf, seq=None, past=None):
    rows, d = h2.shape
    f = w_down.shape[0]
    nf = f // tf
    row = lambda i, j: (i, 0)
    gcol = lambda i, j: (0, j)
    vcol = lambda i, j: (0, nf + j)
    in_specs = [
        pl.BlockSpec((tm, d), row),
        pl.BlockSpec((d, tf), gcol), pl.BlockSpec((d, tf), vcol),
        pl.BlockSpec((3, tf), gcol), pl.BlockSpec((3, tf), vcol),
        pl.BlockSpec((1, tf), gcol), pl.BlockSpec((1, tf), vcol),
        pl.BlockSpec((tf, d), lambda i, j: (j, 0)),
        pl.BlockSpec((tm, d), row),
        pl.BlockSpec((1, d), lambda i, j: (0, 0)),
    ]
    args = [h2, w_up, w_up, w_conv, w_conv, b_conv2d, b_conv2d, w_down, x1, g.reshape(1, d)]
    scratch = [pltpu.VMEM((tm, d), F32)]
    if past is None:
        tiles_per_seq = seq // tm
        kern = functools.partial(_ffn_prompt_kernel, tiles_per_seq=tiles_per_seq)
        state_spec = pl.BlockSpec((1, 8, tf), lambda i, j: (i, 0, j))
        out_specs = [pl.BlockSpec((tm, d), row), state_spec, state_spec]
        out_shape = [jax.ShapeDtypeStruct((rows, d), F32),
                     jax.ShapeDtypeStruct((rows // tm, 8, f), F32),
                     jax.ShapeDtypeStruct((rows // tm, 8, f), F32)]
        scratch.append(pltpu.VMEM((2 * nf, 8, tf), F32))
        name = "ffn_prompt"
    else:
        p2, p1 = past
        kern = _ffn_sample_kernel
        ucol = pl.BlockSpec((tm, tf), lambda i, j: (i, j))
        in_specs += [
            pl.BlockSpec((tm, tf), lambda i, j: (i, j)), pl.BlockSpec((tm, tf), lambda i, j: (i, nf + j)),
            pl.BlockSpec((tm, tf), lambda i, j: (i, j)), pl.BlockSpec((tm, tf), lambda i, j: (i, nf + j)),
        ]
        args += [p2, p2, p1, p1]
        out_specs = [pl.BlockSpec((tm, d), row), ucol, ucol]
        out_shape = [jax.ShapeDtypeStruct((rows, d), F32),
                     jax.ShapeDtypeStruct((rows, f), F32),
                     jax.ShapeDtypeStruct((rows, f), F32)]
        name = "ffn_sample"
    return pl.pallas_call(
        kern,
        grid=(rows // tm, nf),
        in_specs=in_specs,
        out_specs=out_specs,
        out_shape=out_shape,
        scratch_shapes=scratch,
        compiler_params=_cparams(("arbitrary", "arbitrary")),
        name=name,
    )(*args)


def _regroup_w_in(w_in):
    d = w_in.shape[0]
    o_if = QK_W * 2 + V_W * 2
    o_qa = o_if + 2 * N_MH
    o_ka = o_qa + ATT_Q
    o_ga = o_ka + 2 * ATT_KV
    assert o_if == _Z_QA
    parts = [
        w_in[:, o_qa:o_ka],
        w_in[:, o_ga:],
        w_in[:, o_ka:o_ga],
        w_in[:, o_if:o_qa],
    ]
    used = o_if + sum(p.shape[1] for p in parts)
    parts.append(jnp.zeros((d, _Z_W - used), w_in.dtype))
    return w_in[:, :o_if].astype(BF16), jnp.concatenate(parts, axis=1).astype(BF16)


def _pick(rows, pref):
    t = min(rows, pref)
    while rows % t:
        t //= 2
    return t


def kernel(x_prompt, x_sample, state_mlstm_c, state_mlstm_n, state_mlstm_m, cache_swa_k, cache_swa_v,
           state_ffn_conv, g_pre_mix, w_in, b_if, attn_sinks, w_branch_a, w_branch_b, w_out, g_post_mix,
           g_pre_ffn, w_up, w_conv, b_conv, w_down, g_post_ffn):
    bp, tp, d = x_prompt.shape
    bs, ts, _ = x_sample.shape
    depth = w_in.shape[0]
    assert depth == 1 and ts == 1 and d == D_MODEL
    f = w_down.shape[1]
    tf = 512
    assert f % tf == 0

    l = 0
    w_lo, w_hi = _regroup_w_in(w_in[l])
    wa = w_branch_a[l].astype(BF16)
    wb = w_branch_b[l].astype(BF16)
    wo = w_out[l].astype(BF16)
    wu = w_up[l].astype(BF16)
    wd = w_down[l].astype(BF16)
    bias_row = jnp.concatenate([b_if[l].astype(F32), jnp.zeros((LANES - 2 * N_MH,), F32)]).reshape(1, LANES)
    sinks = attn_sinks[l].astype(F32)
    b_conv2d = b_conv[l].reshape(1, 2 * f)

    xp = x_prompt.reshape(bp * tp, d)
    zp = _project(xp, g_pre_mix[l], w_lo, w_hi, _rope_tables(jnp.arange(tp)), _pick(tp, 1024))
    hm_p3, c_p, n_p, m_p = _mlstm_prompt(zp, bias_row, bp, tp, _pick(bp, 4))
    hm_p = hm_p3.reshape(bp * tp, V_W)
    ha_p = _swa_prompt(zp, sinks, bp, tp)
    t_p = _merge1(hm_p, ha_p, zp, wa, wb, _pick(bp * tp, 1024), 512)
    x1_p, h2_p = _merge2(t_p, wo, xp, g_post_mix[l], g_pre_ffn[l], _pick(bp * tp, 512))
    tm_ffn = _pick(tp, 512)
    y_p, sg_p, sv_p = _ffn(h2_p, x1_p, wu, w_conv[l], b_conv2d, wd, g_post_ffn[l],
                           tm_ffn, tf, seq=tp)
    seq_end = slice(tp // tm_ffn - 1, None, tp // tm_ffn)

    y_prompt = y_p.reshape(bp, tp, d)
    c_prompt = c_p[None]
    n_prompt = n_p[None]
    m_prompt = m_p[:, :N_MH, 0][None]
    kv_tail = zp.reshape(bp, tp, _Z_W)[:, tp - WINDOW:, _Z_KA:_Z_KA + 2 * ATT_KV]
    k_prompt = kv_tail[..., :ATT_KV].reshape(bp, WINDOW, N_KV, HD)[None]
    v_prompt = kv_tail[..., ATT_KV:].reshape(bp, WINDOW, N_KV, HD)[None]
    conv_prompt = jnp.concatenate([sg_p[seq_end, 6:8, :], sv_p[seq_end, 6:8, :]], axis=-1)[None]

    xs = x_sample.reshape(bs, d)
    tabs_s = tuple(jnp.broadcast_to(a, (bs, LANES)) for a in _rope_tables(PAST_LEN + jnp.arange(ts)))
    zs = _project(xs, g_pre_mix[l], w_lo, w_hi, tabs_s, bs)
    zs3 = zs.reshape(bs, 1, _Z_W)
    m0_pad = jnp.pad(state_mlstm_m[l].astype(F32), ((0, 0), (0, LANES - N_MH))).reshape(bs, 1, LANES)
    hm_s3, c_sample, n_sample, m_s3 = _mlstm_sample(zs3, bias_row, state_mlstm_c, state_mlstm_n, m0_pad, l)
    o_s3, k_s, v_s = _swa_sample(
        zs[:, _Z_QA:_Z_QA + ATT_Q].reshape(bs, N_Q, HD), zs3, zs3,
        cache_swa_k[l].reshape(bs, WINDOW, ATT_KV), cache_swa_v[l].reshape(bs, WINDOW, ATT_KV),
        sinks.reshape(N_Q, 1), _pick(bs, 8))
    ha_s = o_s3.reshape(bs, ATT_Q).astype(BF16)
    t_s = _merge1(hm_s3.reshape(bs, V_W), ha_s, zs, wa, wb, bs, 512)
    x1_s, h2_s = _merge2(t_s, wo, xs, g_post_mix[l], g_pre_ffn[l], bs)
    past = state_ffn_conv[l].astype(F32)
    y_s, ug_s, uv_s = _ffn(h2_s, x1_s, wu, w_conv[l], b_conv2d, wd, g_post_ffn[l],
                           bs, tf, past=(past[:, 0, :], past[:, 1, :]))

    y_sample = y_s.reshape(bs, ts, d)
    m_sample = m_s3[:, 0, :N_MH][None]
    k_sample = k_s.reshape(bs, WINDOW, N_KV, HD)[None]
    v_sample = v_s.reshape(bs, WINDOW, N_KV, HD)[None]
    conv_sample = jnp.stack([past[:, 1, :], jnp.concatenate([ug_s, uv_s], axis=-1)], axis=1)[None]

    return (y_prompt, y_sample,
            c_prompt, n_prompt, m_prompt, k_prompt, v_prompt, conv_prompt,
            c_sample, n_sample, m_sample, k_sample, v_sample, conv_sample)
```

```python
import functools

import jax
import jax.numpy as jnp
from jax import lax
from jax.experimental import pallas as pl
from jax.experimental.pallas import tpu as pltpu

F32 = jnp.float32
BF16 = jnp.bfloat16

N_MH = 4
DK = 256
DV = 512
CHUNK = 64
HD = 64
N_Q = 32
N_KV = 4
GROUP = N_Q // N_KV
WINDOW = 128
ROT = 16
THETA = 500000.0
PAST_LEN = 16384
EPS = 1e-6
D_MODEL = 2048
QK_W = N_MH * DK
V_W = N_MH * DV
ATT_Q = N_Q * HD
ATT_KV = N_KV * HD

_Z_QM = 0
_Z_KM = _Z_QM + QK_W
_Z_VM = _Z_KM + QK_W
_Z_OM = _Z_VM + V_W
_Z_QA = _Z_OM + V_W
_Z_GA = _Z_QA + ATT_Q
_Z_GB = _Z_GA + D_MODEL
_Z_KA = _Z_GB + D_MODEL
_Z_VA = _Z_KA + ATT_KV
_Z_IF = _Z_VA + ATT_KV
LANES = 128
_PROJ_TN = 1024
_Z_W = -(-(_Z_IF + LANES) // _PROJ_TN) * _PROJ_TN
assert _Z_QA % _PROJ_TN == 0 and _Z_GA % _PROJ_TN == 0 and _Z_KA % _PROJ_TN == 0

_VMEM_LIMIT = 56 * 1024 * 1024


def _cparams(sem):
    return pltpu.CompilerParams(dimension_semantics=sem, vmem_limit_bytes=_VMEM_LIMIT)


def _rms(x, g):
    return x * lax.rsqrt(jnp.mean(x * x, axis=-1, keepdims=True) + EPS) * g


def _sigmoid(x):
    return 1.0 / (1.0 + jnp.exp(-x))


def _log_sigmoid(x):
    return jnp.minimum(x, 0.0) - jnp.log1p(jnp.exp(-jnp.abs(x)))


def _dot(a, b):
    return jnp.dot(a, b, preferred_element_type=F32)


def _dot_nt(a, b):
    return lax.dot_general(a, b, (((1,), (1,)), ((), ())), preferred_element_type=F32)


def _dot_tn(a, b):
    return lax.dot_general(a, b, (((0,), (0,)), ((), ())), preferred_element_type=F32)


def _proj_kernel(x_ref, g_ref, wlo_ref, whi_ref, cos_ref, sa_ref, sb_ref, z_ref, h_ref):
    j = pl.program_id(1)
    jq0 = _Z_QA // _PROJ_TN
    jq1 = _Z_GA // _PROJ_TN
    jk = _Z_KA // _PROJ_TN

    @pl.when(j == 0)
    def _():
        h_ref[...] = _rms(x_ref[...], g_ref[...]).astype(BF16)

    is_lo = j < jq0
    is_q = (j >= jq0) & (j < jq1)
    is_k = j == jk

    @pl.when(is_lo)
    def _():
        z_ref[...] = _dot(h_ref[...], wlo_ref[...])

    @pl.when(is_q)
    def _():
        acc = _dot(h_ref[...], whi_ref[...])
        z_ref[...] = _rope(acc, cos_ref[...], sa_ref[...], sb_ref[...]) * (HD ** -0.5)

    @pl.when(is_k)
    def _():
        acc = _dot(h_ref[...], whi_ref[...])
        z_ref[...] = acc
        z_ref[:, :ATT_KV] = _rope(acc[:, :ATT_KV], cos_ref[...], sa_ref[...], sb_ref[...])

    @pl.when(jnp.logical_not(is_lo | is_q | is_k))
    def _():
        z_ref[...] = _dot(h_ref[...], whi_ref[...])


def _project(x2d, g, w_lo, w_hi, tables, tm):
    rows, d = x2d.shape
    tn = _PROJ_TN
    n_lo = _Z_QA // tn
    n = n_lo * tn + w_hi.shape[1]
    assert n == _Z_W and w_lo.shape[1] >= n_lo * tn
    tab_tiles = tables[0].shape[0] // tm
    tab = pl.BlockSpec((tm, LANES), lambda i, j: (i % tab_tiles, 0))
    return pl.pallas_call(
        _proj_kernel,
        grid=(rows // tm, n // tn),
        in_specs=[
            pl.BlockSpec((tm, d), lambda i, j: (i, 0)),
            pl.BlockSpec((1, d), lambda i, j: (0, 0)),
            pl.BlockSpec((d, tn), lambda i, j: (0, jnp.minimum(j, n_lo - 1))),
            pl.BlockSpec((d, tn), lambda i, j: (0, jnp.maximum(j - n_lo, 0))),
            tab, tab, tab,
        ],
        out_specs=pl.BlockSpec((tm, tn), lambda i, j: (i, j)),
        out_shape=jax.ShapeDtypeStruct((rows, n), F32),
        scratch_shapes=[pltpu.VMEM((tm, d), BF16)],
        compiler_params=_cparams(("parallel", "arbitrary")),
        name="proj",
    )(x2d, g.reshape(1, d), w_lo, w_hi, *tables)


def _mlstm_prompt_kernel(q_ref, k_ref, v_ref, o_ref, g_ref, bias_ref,
                         hm_ref, c_out, n_out, m_out, ct_s, n_s, m_s):
    c = pl.program_id(1)
    last = pl.num_programs(1) - 1
    L = CHUNK
    nseq = q_ref.shape[0]

    @pl.when(c == 0)
    def _():
        ct_s[...] = jnp.zeros_like(ct_s)
        n_s[...] = jnp.zeros_like(n_s)
        m_s[...] = jnp.zeros_like(m_s)

    row = lax.broadcasted_iota(jnp.int32, (L, LANES), 0)
    ri = lax.broadcasted_iota(jnp.int32, (L, L), 0)
    ci = lax.broadcasted_iota(jnp.int32, (L, L), 1)
    eye = ri == ci
    causal = ci <= ri

    heads = []
    for bi in range(nseq):
        gates = g_ref[bi] + bias_ref[...]
        bcum = _log_sigmoid(gates)
        for d in (1, 2, 4, 8, 16, 32):
            bcum = bcum + jnp.where(row >= d, pltpu.roll(bcum, d, axis=0), 0.0)
        for h in range(N_MH):
            q = q_ref[bi, :, h * DK:(h + 1) * DK]
            k = k_ref[bi, :, h * DK:(h + 1) * DK] * (DK ** -0.5)
            qb = q.astype(BF16)
            ct = ct_s[bi * N_MH + h]
            heads.append((bi, h, gates, bcum, q, k, _dot_nt(qb, k.astype(BF16)), _dot(qb, ct.astype(BF16))))

    for bi, h, gates, bcum, q, k, qk, cq in heads:
        v = v_ref[bi, :, h * DV:(h + 1) * DV]
        o = o_ref[bi, :, h * DV:(h + 1) * DV]
        b_col = bcum[:, N_MH + h:N_MH + h + 1]
        ig_col = gates[:, h:h + 1]
        a_col = ig_col - b_col
        a_lane = jnp.sum(jnp.where(eye, a_col, 0.0), axis=0, keepdims=True)
        dmat = jnp.where(causal, b_col + a_lane, -jnp.inf)
        m_prev = m_s[bi, h:h + 1, 0:1]
        inter = b_col + m_prev
        m_t = jnp.maximum(jnp.max(dmat, axis=1, keepdims=True), inter)
        vb = v.astype(BF16)
        s = qk * jnp.exp(dmat - m_t)
        w_inter = jnp.exp(inter - m_t)
        m_new = m_t[L - 1:L, :]
        b_last = b_col[L - 1:L, :]
        w_last = jnp.exp(b_last - b_col + ig_col - m_new)
        decay = jnp.exp(b_last + m_prev - m_new)
        kw = k * w_last
        ct = ct_s[bi * N_MH + h]
        ct_s[bi * N_MH + h] = decay * ct + _dot_tn(kw.astype(BF16), vb)
        num = _dot(s.astype(BF16), vb) + w_inter * cq
        n_row = n_s[bi, h:h + 1, :]
        den = jnp.sum(s, axis=1, keepdims=True) + w_inter * jnp.sum(q * n_row, axis=1, keepdims=True)
        hout = num / jnp.maximum(jnp.abs(den), jnp.exp(-m_t))
        hm_ref[bi, :, h * DV:(h + 1) * DV] = (_sigmoid(o) * hout).astype(BF16)
        n_s[bi, h:h + 1, :] = decay * n_row + jnp.sum(kw, axis=0, keepdims=True)
        m_s[bi, h:h + 1, :] = jnp.broadcast_to(m_new, (1, LANES))

    @pl.when(c == last)
    def _():
        for bi in range(nseq):
            for h in range(N_MH):
                c_out[bi, h] = ct_s[bi * N_MH + h].T
        n_out[...] = n_s[:, :N_MH, :]
        m_out[...] = m_s[...]


def _mlstm_prompt(z, bias_row, batch, seq, bb):
    nc = seq // CHUNK
    z3 = z.reshape(batch, seq, z.shape[1])

    def zspec(width, off):
        blk = off // width
        return pl.BlockSpec((bb, CHUNK, width), lambda b, c: (b, c, blk))

    return pl.pallas_call(
        _mlstm_prompt_kernel,
        grid=(batch // bb, nc),
        in_specs=[
            zspec(QK_W, _Z_QM), zspec(QK_W, _Z_KM), zspec(V_W, _Z_VM), zspec(V_W, _Z_OM),
            zspec(LANES, _Z_IF),
            pl.BlockSpec((1, LANES), lambda b, c: (0, 0)),
        ],
        out_specs=[
            pl.BlockSpec((bb, CHUNK, V_W), lambda b, c: (b, c, 0)),
            pl.BlockSpec((bb, N_MH, DV, DK), lambda b, c: (b, 0, 0, 0)),
            pl.BlockSpec((bb, N_MH, DK), lambda b, c: (b, 0, 0)),
            pl.BlockSpec((bb, 8, LANES), lambda b, c: (b, 0, 0)),
        ],
        out_shape=[
            jax.ShapeDtypeStruct((batch, seq, V_W), BF16),
            jax.ShapeDtypeStruct((batch, N_MH, DV, DK), F32),
            jax.ShapeDtypeStruct((batch, N_MH, DK), F32),
            jax.ShapeDtypeStruct((batch, 8, LANES), F32),
        ],
        scratch_shapes=[
            pltpu.VMEM((bb * N_MH, DK, DV), F32),
            pltpu.VMEM((bb, 8, DK), F32),
            pltpu.VMEM((bb, 8, LANES), F32),
        ],
        compiler_params=_cparams(("parallel", "arbitrary")),
        name="mlstm_prompt",
    )(z3, z3, z3, z3, z3, bias_row)


def _mlstm_sample_kernel(q_ref, k_ref, v_ref, o_ref, g_ref, bias_ref, c_ref, n_ref, m_ref,
                         hm_ref, c_out, n_out, m_out):
    gates = g_ref[0] + bias_ref[...]
    lf_all = _log_sigmoid(gates)
    m_all = m_ref[0]
    ri = lax.broadcasted_iota(jnp.int32, (DV, DV), 0)
    ci = lax.broadcasted_iota(jnp.int32, (DV, DV), 1)
    eye = ri == ci
    m_new_all = jnp.zeros((1, LANES), F32)
    lane = lax.broadcasted_iota(jnp.int32, (1, LANES), 1)

    for h in range(N_MH):
        q = q_ref[0, :, h * DK:(h + 1) * DK]
        k = k_ref[0, :, h * DK:(h + 1) * DK] * (DK ** -0.5)
        v = v_ref[0, :, h * DV:(h + 1) * DV]
        o = o_ref[0, :, h * DV:(h + 1) * DV]
        ig = gates[:, h:h + 1]
        lf = lf_all[:, N_MH + h:N_MH + h + 1]
        m_prev = m_all[:, h:h + 1]
        inter = lf + m_prev
        m_t = jnp.maximum(ig, inter)
        s = jnp.sum(q * k, axis=1, keepdims=True) * jnp.exp(ig - m_t)
        w_inter = jnp.exp(inter - m_t)
        cmat = c_ref[0, 0, h]
        q8 = jnp.broadcast_to(q, (8, DK)).astype(BF16)
        cq = _dot_nt(q8, cmat.astype(BF16))[0:1, :]
        n_row = n_ref[0, 0, h:h + 1, :]
        num = s * v + w_inter * cq
        den = s + w_inter * jnp.sum(n_row * q, axis=1, keepdims=True)
        hout = num / jnp.maximum(jnp.abs(den), jnp.exp(-m_t))
        hm_ref[0, :, h * DV:(h + 1) * DV] = (_sigmoid(o) * hout).astype(BF16)

        w_last = jnp.exp(ig - m_t)
        decay = jnp.exp(inter - m_t)
        vw = v * w_last
        vw_col = jnp.sum(jnp.where(eye, vw, 0.0), axis=1, keepdims=True)
        c_out[0, 0, h] = decay * cmat + vw_col * k
        n_out[0, 0, h:h + 1, :] = decay * n_row + w_last * k
        m_new_all = jnp.where(lane == h, m_t, m_new_all)

    m_out[0] = m_new_all


def _mlstm_sample(z3, bias_row, c0, n0, m0_pad, layer):
    nb = z3.shape[0]

    def zspec(width, off):
        blk = off // width
        return pl.BlockSpec((1, 1, width), lambda b: (b, 0, blk))

    c_in = pl.BlockSpec((1, 1, N_MH, DV, DK), lambda b: (layer, b, 0, 0, 0))
    n_in = pl.BlockSpec((1, 1, N_MH, DK), lambda b: (layer, b, 0, 0))
    c_blk = pl.BlockSpec((1, 1, N_MH, DV, DK), lambda b: (0, b, 0, 0, 0))
    n_blk = pl.BlockSpec((1, 1, N_MH, DK), lambda b: (0, b, 0, 0))

    return pl.pallas_call(
        _mlstm_sample_kernel,
        grid=(nb,),
        in_specs=[
            zspec(QK_W, _Z_QM), zspec(QK_W, _Z_KM), zspec(V_W, _Z_VM), zspec(V_W, _Z_OM),
            zspec(LANES, _Z_IF),
            pl.BlockSpec((1, LANES), lambda b: (0, 0)),
            c_in,
            n_in,
            pl.BlockSpec((1, 1, LANES), lambda b: (b, 0, 0)),
        ],
        out_specs=[
            pl.BlockSpec((1, 1, V_W), lambda b: (b, 0, 0)),
            c_blk,
            n_blk,
            pl.BlockSpec((1, 1, LANES), lambda b: (b, 0, 0)),
        ],
        out_shape=[
            jax.ShapeDtypeStruct((nb, 1, V_W), BF16),
            jax.ShapeDtypeStruct((1, nb, N_MH, DV, DK), F32),
            jax.ShapeDtypeStruct((1, nb, N_MH, DK), F32),
            jax.ShapeDtypeStruct((nb, 1, LANES), F32),
        ],
        compiler_params=_cparams(("parallel",)),
        name="mlstm_sample",
    )(z3, z3, z3, z3, z3, bias_row, c0, n0, m0_pad)


def _rope_tables(pos):
    half = ROT // 2
    inv = THETA ** (-jnp.arange(half, dtype=F32) * 2.0 / ROT)
    ang = pos.astype(F32)[:, None] * inv[None, :]
    cos = jnp.cos(ang)
    sin = jnp.sin(ang)
    t = pos.shape[0]
    cos_t = jnp.concatenate([cos, cos, jnp.ones((t, HD - ROT), F32)], axis=1)
    sin_a = jnp.concatenate([-sin, jnp.zeros((t, HD - half), F32)], axis=1)
    sin_b = jnp.concatenate([jnp.zeros((t, half), F32), sin, jnp.zeros((t, HD - ROT), F32)], axis=1)
    rep = LANES // HD
    return tuple(jnp.tile(a, (1, rep)) for a in (cos_t, sin_a, sin_b))


def _rope(x, cos_t, sin_a, sin_b):
    w = x.shape[1]
    rep = w // LANES
    half = ROT // 2
    if rep > 1:
        cos_t = jnp.concatenate([cos_t] * rep, axis=1)
        sin_a = jnp.concatenate([sin_a] * rep, axis=1)
        sin_b = jnp.concatenate([sin_b] * rep, axis=1)
    up = pltpu.roll(x, w - half, axis=1)
    down = pltpu.roll(x, half, axis=1)
    return x * cos_t + up * sin_a + down * sin_b


def _swa_prompt_kernel(sink_ref, q_ref, kc_ref, kp_ref, vc_ref, vp_ref, ha_ref):
    i = pl.program_id(1)
    w = WINDOW
    kk = jnp.concatenate([kp_ref[...], kc_ref[...]], axis=0)
    vv = jnp.concatenate([vp_ref[...], vc_ref[...]], axis=0)
    vv_t = vv.T.astype(BF16)
    s_idx = lax.broadcasted_iota(jnp.int32, (2 * w, w), 0)
    a_idx = lax.broadcasted_iota(jnp.int32, (2 * w, w), 1)
    valid = (s_idx >= a_idx) & (s_idx <= a_idx + w) & ((i > 0) | (s_idx >= w))
    lane = lax.broadcasted_iota(jnp.int32, (2 * w, LANES), 1)
    pair_w = 2 * HD
    for half in range(N_KV // 2):
        kh = kk[:, half * pair_w:(half + 1) * pair_w]
        kh_sw = pltpu.roll(kh, HD, axis=1)
        for gg in range(2):
            g = 2 * half + gg
            k_lo = jnp.where(lane < HD, kh if gg == 0 else kh_sw, 0.0).astype(BF16)
            k_hi = jnp.where(lane >= HD, kh_sw if gg == 0 else kh, 0.0).astype(BF16)
            vg_t = vv_t[g * HD:(g + 1) * HD, :]
            sts = []
            for a in range(GROUP // 2):
                col = (g * GROUP + 2 * a) * HD
                qpair = q_ref[:, col:col + pair_w].astype(BF16)
                sts += [_dot_nt(k_lo, qpair), _dot_nt(k_hi, qpair)]
            es, rs = [], []
            for hh in range(GROUP):
                sink = sink_ref[g * GROUP + hh]
                st = jnp.where(valid, sts[hh], -jnp.inf)
                m = jnp.maximum(jnp.max(st, axis=0, keepdims=True), sink)
                e = jnp.exp(st - m)
                es.append(e.astype(BF16))
                rs.append(1.0 / (jnp.sum(e, axis=0, keepdims=True) + jnp.exp(sink - m)))
            ots = [_dot(vg_t, es[hh]) * rs[hh] for hh in range(GROUP)]
            for a in range(GROUP // 2):
                col = (g * GROUP + 2 * a) * HD
                ha_ref[:, col:col + pair_w] = jnp.concatenate(ots[2 * a:2 * a + 2], axis=0).T.astype(BF16)


def _swa_prompt(z, sinks3, batch, seq):
    rows = z.shape[0]
    nb = seq // WINDOW
    w = WINDOW

    def cur(width, off):
        blk = off // width
        return pl.BlockSpec((w, width), lambda b, i: (b * nb + i, blk))

    def prev(width, off):
        blk = off // width
        return pl.BlockSpec((w, width), lambda b, i: (b * nb + jnp.maximum(i - 1, 0), blk))

    return pl.pallas_call(
        _swa_prompt_kernel,
        grid=(batch, nb),
        in_specs=[
            pl.BlockSpec(memory_space=pltpu.SMEM),
            cur(ATT_Q, _Z_QA), cur(ATT_KV, _Z_KA), prev(ATT_KV, _Z_KA),
            cur(ATT_KV, _Z_VA), prev(ATT_KV, _Z_VA),
        ],
        out_specs=pl.BlockSpec((w, ATT_Q), lambda b, i: (b * nb + i, 0)),
        out_shape=jax.ShapeDtypeStruct((rows, ATT_Q), BF16),
        compiler_params=_cparams(("parallel", "parallel")),
        name="swa_prompt",
    )(sinks3, z, z, z, z, z)


def _swa_sample_kernel(q_ref, kn_ref, vn_ref, kp_ref, vp_ref, sink_ref,
                       o_ref, ko_ref, vo_ref, *, bb):
    w = WINDOW
    hrow = lax.broadcasted_iota(jnp.int32, (N_Q, ATT_KV), 0) // GROUP
    glane = lax.broadcasted_iota(jnp.int32, (N_Q, ATT_KV), 1) // HD
    own = hrow == glane
    srow = lax.broadcasted_iota(jnp.int32, (w, ATT_KV), 0)
    sink = sink_ref[...]
    for b in range(bb):
        q = q_ref[b]
        qm = jnp.where(own, jnp.concatenate([q] * N_KV, axis=1), 0.0)
        k_past = kp_ref[b]
        v_past = vp_ref[b]
        k_new = kn_ref[b]
        v_new = vn_ref[b]
        s_past = _dot_nt(qm.astype(BF16), k_past.astype(BF16))
        s_new = jnp.sum(qm * k_new, axis=1, keepdims=True)
        m = jnp.maximum(jnp.maximum(jnp.max(s_past, axis=1, keepdims=True), s_new), sink)
        e_past = jnp.exp(s_past - m)
        e_new = jnp.exp(s_new - m)
        denom = jnp.sum(e_past, axis=1, keepdims=True) + e_new + jnp.exp(sink - m)
        r = 1.0 / denom
        pv = _dot((e_past * r).astype(BF16), v_past.astype(BF16)) + (e_new * r) * v_new
        pv = jnp.where(own, pv, 0.0)
        o = pv[:, 0:HD]
        for g in range(1, N_KV):
            o = o + pv[:, g * HD:(g + 1) * HD]
        o_ref[b] = o
        ko_ref[b] = jnp.where(srow == w - 1, k_new, pltpu.roll(k_past, w - 1, axis=0))
        vo_ref[b] = jnp.where(srow == w - 1, v_new, pltpu.roll(v_past, w - 1, axis=0))


def _swa_sample(q3, k_new3, v_new3, k_past, v_past, sink_col, bb):
    nb = q3.shape[0]
    w = WINDOW
    return pl.pallas_call(
        functools.partial(_swa_sample_kernel, bb=bb),
        grid=(nb // bb,),
        in_specs=[
            pl.BlockSpec((bb, N_Q, HD), lambda i: (i, 0, 0)),
            pl.BlockSpec((bb, 1, ATT_KV), lambda i: (i, 0, _Z_KA // ATT_KV)),
            pl.BlockSpec((bb, 1, ATT_KV), lambda i: (i, 0, _Z_VA // ATT_KV)),
            pl.BlockSpec((bb, w, ATT_KV), lambda i: (i, 0, 0)),
            pl.BlockSpec((bb, w, ATT_KV), lambda i: (i, 0, 0)),
            pl.BlockSpec((N_Q, 1), lambda i: (0, 0)),
        ],
        out_specs=[
            pl.BlockSpec((bb, N_Q, HD), lambda i: (i, 0, 0)),
            pl.BlockSpec((bb, w, ATT_KV), lambda i: (i, 0, 0)),
            pl.BlockSpec((bb, w, ATT_KV), lambda i: (i, 0, 0)),
        ],
        out_shape=[
            jax.ShapeDtypeStruct((nb, N_Q, HD), F32),
            jax.ShapeDtypeStruct((nb, w, ATT_KV), F32),
            jax.ShapeDtypeStruct((nb, w, ATT_KV), F32),
        ],
        compiler_params=_cparams(("parallel",)),
        name="swa_sample",
    )(q3, k_new3, v_new3, k_past, v_past, sink_col)


def _merge1_kernel(hm_ref, ha_ref, ga_ref, gb_ref, wa_ref, wb_ref, t_ref):
    a = _dot(hm_ref[...], wa_ref[...])
    b = _dot(ha_ref[...], wb_ref[...])
    t_ref[...] = (_sigmoid(ga_ref[...]) * a + _sigmoid(gb_ref[...]) * b).astype(BF16)


def _merge1(hm, ha, z, wa, wb, tm, tn):
    rows, d = hm.shape
    n = wa.shape[1]
    return pl.pallas_call(
        _merge1_kernel,
        grid=(rows // tm, n // tn),
        in_specs=[
            pl.BlockSpec((tm, d), lambda i, j: (i, 0)),
            pl.BlockSpec((tm, d), lambda i, j: (i, 0)),
            pl.BlockSpec((tm, tn), lambda i, j: (i, _Z_GA // tn + j)),
            pl.BlockSpec((tm, tn), lambda i, j: (i, _Z_GB // tn + j)),
            pl.BlockSpec((d, tn), lambda i, j: (0, j)),
            pl.BlockSpec((d, tn), lambda i, j: (0, j)),
        ],
        out_specs=pl.BlockSpec((tm, tn), lambda i, j: (i, j)),
        out_shape=jax.ShapeDtypeStruct((rows, n), BF16),
        compiler_params=_cparams(("parallel", "arbitrary")),
        name="merge1",
    )(hm, ha, z, z, wa, wb)


def _merge2_kernel(t_ref, w_ref, x_ref, g1_ref, g2_ref, x1_ref, h2_ref):
    mix = _dot(t_ref[...], w_ref[...])
    x1 = x_ref[...] + _rms(mix, g1_ref[...])
    x1_ref[...] = x1
    h2_ref[...] = _rms(x1, g2_ref[...]).astype(BF16)


def _merge2(t, w_out, x2d, g1, g2, tm):
    rows, d = x2d.shape
    row = lambda i: (i, 0)
    const = lambda i: (0, 0)
    return pl.pallas_call(
        _merge2_kernel,
        grid=(rows // tm,),
        in_specs=[
            pl.BlockSpec((tm, d), row),
            pl.BlockSpec((d, d), const, pipeline_mode=pl.Buffered(1)),
            pl.BlockSpec((tm, d), row),
            pl.BlockSpec((1, d), const),
            pl.BlockSpec((1, d), const),
        ],
        out_specs=[pl.BlockSpec((tm, d), row), pl.BlockSpec((tm, d), row)],
        out_shape=[jax.ShapeDtypeStruct((rows, d), F32), jax.ShapeDtypeStruct((rows, d), BF16)],
        compiler_params=_cparams(("parallel",)),
        name="merge2",
    )(t, w_out, x2d, g1.reshape(1, d), g2.reshape(1, d))


def _gelu_tanh(x):
    return 0.5 * x * (1.0 + jnp.tanh(0.7978845608028654 * (x + 0.044715 * (x * x * x))))


def _conv_taps(u, u1, u2, cw_ref, cb_ref, cs=slice(None)):
    return cb_ref[:, cs] + u2 * cw_ref[0:1, cs] + u1 * cw_ref[1:2, cs] + u * cw_ref[2:3, cs]


def _ffn_tail(j, nf, y, wd_ref, x1_ref, g_ref, y_ref, acc_ref):
    @pl.when(j == 0)
    def _():
        acc_ref[...] = _dot(y, wd_ref[...])

    @pl.when(j > 0)
    def _():
        acc_ref[...] += _dot(y, wd_ref[...])

    @pl.when(j == nf - 1)
    def _():
        y_ref[...] = x1_ref[...] + _rms(acc_ref[...], g_ref[...])


def _ffn_prompt_kernel(h_ref, wg_ref, wv_ref, cwg_ref, cwv_ref, cbg_ref, cbv_ref, wd_ref,
                       x1_ref, g_ref, y_ref, sg_ref, sv_ref, acc_ref, carry_ref,
                       *, tiles_per_seq):
    i = pl.program_id(0)
    j = pl.program_id(1)
    nf = pl.num_programs(1)
    tm = h_ref.shape[0]

    @pl.when((i == 0) & (j == 0))
    def _():
        carry_ref[...] = jnp.zeros_like(carry_ref)

    seq_start = (i % tiles_per_seq) == 0
    h = h_ref[...]

    def branch(w_ref, cw_ref, cb_ref, slot, state_ref):
        u = _dot(h, w_ref[...])
        prev = jnp.where(seq_start, 0.0, carry_ref[slot])
        uu = jnp.concatenate([prev, u], axis=0)
        u1 = uu[7:7 + tm, :]
        u2 = uu[6:6 + tm, :]
        tail = u[tm - 8:tm, :]
        carry_ref[slot] = tail
        state_ref[0] = tail
        return _conv_taps(u, u1, u2, cw_ref, cb_ref)

    cg = branch(wg_ref, cwg_ref, cbg_ref, 2 * j, sg_ref)
    cv = branch(wv_ref, cwv_ref, cbv_ref, 2 * j + 1, sv_ref)
    y = (_gelu_tanh(cg) * cv).astype(BF16)
    _ffn_tail(j, nf, y, wd_ref, x1_ref, g_ref, y_ref, acc_ref)


def _ffn_sample_kernel(h_ref, wg_ref, wv_ref, cwg_ref, cwv_ref, cbg_ref, cbv_ref, wd_ref,
                       x1_ref, g_ref, p2g_ref, p2v_ref, p1g_ref, p1v_ref,
                       y_ref, ug_ref, uv_ref, acc_ref):
    j = pl.program_id(1)
    nf = pl.num_programs(1)
    h = h_ref[...]
    ug = _dot(h, wg_ref[...])
    uv = _dot(h, wv_ref[...])
    ug_ref[...] = ug
    uv_ref[...] = uv
    cg = _conv_taps(ug, p1g_ref[...], p2g_ref[...], cwg_ref, cbg_ref)
    cv = _conv_taps(uv, p1v_ref[...], p2v_ref[...], cwv_ref, cbv_ref)
    y = (_gelu_tanh(cg) * cv).astype(BF16)
    _ffn_tail(j, nf, y, wd_ref, x1_ref, g_ref, y_ref, acc_ref)


def _ffn(h2, x1, w_up, w_conv, b_conv2d, w_down, g, tm, tf, seq=None, past=None):
    rows, d = h2.shape
    f = w_down.shape[0]
    nf = f // tf
    row = lambda i, j: (i, 0)
    gcol = lambda i, j: (0, j)
    vcol = lambda i, j: (0, nf + j)
    in_specs = [
        pl.BlockSpec((tm, d), row),
        pl.BlockSpec((d, tf), gcol), pl.BlockSpec((d, tf), vcol),
        pl.BlockSpec((3, tf), gcol), pl.BlockSpec((3, tf), vcol),
        pl.BlockSpec((1, tf), gcol), pl.BlockSpec((1, tf), vcol),
        pl.BlockSpec((tf, d), lambda i, j: (j, 0)),
        pl.BlockSpec((tm, d), row),
        pl.BlockSpec((1, d), lambda i, j: (0, 0)),
    ]
    args = [h2, w_up, w_up, w_conv, w_conv, b_conv2d, b_conv2d, w_down, x1, g.reshape(1, d)]
    scratch = [pltpu.VMEM((tm, d), F32)]
    if past is None:
        tiles_per_seq = seq // tm
        kern = functools.partial(_ffn_prompt_kernel, tiles_per_seq=tiles_per_seq)
        state_spec = pl.BlockSpec((1, 8, tf), lambda i, j: (i, 0, j))
        out_specs = [pl.BlockSpec((tm, d), row), state_spec, state_spec]
        out_shape = [jax.ShapeDtypeStruct((rows, d), F32),
                     jax.ShapeDtypeStruct((rows // tm, 8, f), F32),
                     jax.ShapeDtypeStruct((rows // tm, 8, f), F32)]
        scratch.append(pltpu.VMEM((2 * nf, 8, tf), F32))
        name = "ffn_prompt"
    else:
        p2, p1 = past
        kern = _ffn_sample_kernel
        ucol = pl.BlockSpec((tm, tf), lambda i, j: (i, j))
        in_specs += [
            pl.BlockSpec((tm, tf), lambda i, j: (i, j)), pl.BlockSpec((tm, tf), lambda i, j: (i, nf + j)),
            pl.BlockSpec((tm, tf), lambda i, j: (i, j)), pl.BlockSpec((tm, tf), lambda i, j: (i, nf + j)),
        ]
        args += [p2, p2, p1, p1]
        out_specs = [pl.BlockSpec((tm, d), row), ucol, ucol]
        out_shape = [jax.ShapeDtypeStruct((rows, d), F32),
                     jax.ShapeDtypeStruct((rows, f), F32),
                     jax.ShapeDtypeStruct((rows, f), F32)]
        name = "ffn_sample"
    return pl.pallas_call(
        kern,
        grid=(rows // tm, nf),
        in_specs=in_specs,
        out_specs=out_specs,
        out_shape=out_shape,
        scratch_shapes=scratch,
        compiler_params=_cparams(("arbitrary", "arbitrary")),
        name=name,
    )(*args)


def _regroup_w_in(w_in):
    d = w_in.shape[0]
    o_if = QK_W * 2 + V_W * 2
    o_qa = o_if + 2 * N_MH
    o_ka = o_qa + ATT_Q
    o_ga = o_ka + 2 * ATT_KV
    assert o_if == _Z_QA
    parts = [
        w_in[:, o_qa:o_ka],
        w_in[:, o_ga:],
        w_in[:, o_ka:o_ga],
        w_in[:, o_if:o_qa],
    ]
    used = o_if + sum(p.shape[1] for p in parts)
    parts.append(jnp.zeros((d, _Z_W - used), w_in.dtype))
    return jnp.concatenate(parts, axis=1)


def _pick(rows, pref):
    t = min(rows, pref)
    while rows % t:
        t //= 2
    return t


def kernel(x_prompt, x_sample, state_mlstm_c, state_mlstm_n, state_mlstm_m, cache_swa_k, cache_swa_v,
           state_ffn_conv, g_pre_mix, w_in, b_if, attn_sinks, w_branch_a, w_branch_b, w_out, g_post_mix,
           g_pre_ffn, w_up, w_conv, b_conv, w_down, g_post_ffn):
    bp, tp, d = x_prompt.shape
    bs, ts, _ = x_sample.shape
    depth = w_in.shape[0]
    assert depth == 1 and ts == 1 and d == D_MODEL
    f = w_down.shape[1]
    tf = 512
    assert f % tf == 0

    l = 0
    w_lo = w_in[l].astype(BF16)
    w_hi = _regroup_w_in(w_lo)
    wa = w_branch_a[l].astype(BF16)
    wb = w_branch_b[l].astype(BF16)
    wo = w_out[l].astype(BF16)
    wu = w_up[l].astype(BF16)
    wd = w_down[l].astype(BF16)
    bias_row = jnp.concatenate([b_if[l].astype(F32), jnp.zeros((LANES - 2 * N_MH,), F32)]).reshape(1, LANES)
    sinks = attn_sinks[l].astype(F32)
    b_conv2d = b_conv[l].reshape(1, 2 * f)

    xp = x_prompt.reshape(bp * tp, d)
    zp = _project(xp, g_pre_mix[l], w_lo, w_hi, _rope_tables(jnp.arange(tp)), _pick(tp, 1024))
    hm_p3, c_p, n_p, m_p = _mlstm_prompt(zp, bias_row, bp, tp, _pick(bp, 4))
    hm_p = hm_p3.reshape(bp * tp, V_W)
    ha_p = _swa_prompt(zp, sinks, bp, tp)
    t_p = _merge1(hm_p, ha_p, zp, wa, wb, _pick(bp * tp, 1024), 512)
    x1_p, h2_p = _merge2(t_p, wo, xp, g_post_mix[l], g_pre_ffn[l], _pick(bp * tp, 512))
    tm_ffn = _pick(tp, 512)
    y_p, sg_p, sv_p = _ffn(h2_p, x1_p, wu, w_conv[l], b_conv2d, wd, g_post_ffn[l],
                           tm_ffn, tf, seq=tp)
    seq_end = slice(tp // tm_ffn - 1, None, tp // tm_ffn)

    y_prompt = y_p.reshape(bp, tp, d)
    c_prompt = c_p[None]
    n_prompt = n_p[None]
    m_prompt = m_p[:, :N_MH, 0][None]
    kv_tail = zp.reshape(bp, tp, _Z_W)[:, tp - WINDOW:, _Z_KA:_Z_KA + 2 * ATT_KV]
    k_prompt = kv_tail[..., :ATT_KV].reshape(bp, WINDOW, N_KV, HD)[None]
    v_prompt = kv_tail[..., ATT_KV:].reshape(bp, WINDOW, N_KV, HD)[None]
    conv_prompt = jnp.concatenate([sg_p[seq_end, 6:8, :], sv_p[seq_end, 6:8, :]], axis=-1)[None]

    xs = x_sample.reshape(bs, d)
    tabs_s = tuple(jnp.broadcast_to(a, (bs, LANES)) for a in _rope_tables(PAST_LEN + jnp.arange(ts)))
    zs = _project(xs, g_pre_mix[l], w_lo, w_hi, tabs_s, bs)
    zs3 = zs.reshape(bs, 1, _Z_W)
    m0_pad = jnp.pad(state_mlstm_m[l].astype(F32), ((0, 0), (0, LANES - N_MH))).reshape(bs, 1, LANES)
    hm_s3, c_sample, n_sample, m_s3 = _mlstm_sample(zs3, bias_row, state_mlstm_c, state_mlstm_n, m0_pad, l)
    o_s3, k_s, v_s = _swa_sample(
        zs[:, _Z_QA:_Z_QA + ATT_Q].reshape(bs, N_Q, HD), zs3, zs3,
        cache_swa_k[l].reshape(bs, WINDOW, ATT_KV), cache_swa_v[l].reshape(bs, WINDOW, ATT_KV),
        sinks.reshape(N_Q, 1), _pick(bs, 8))
    ha_s = o_s3.reshape(bs, ATT_Q).astype(BF16)
    t_s = _merge1(hm_s3.reshape(bs, V_W), ha_s, zs, wa, wb, bs, 512)
    x1_s, h2_s = _merge2(t_s, wo, xs, g_post_mix[l], g_pre_ffn[l], bs)
    past = state_ffn_conv[l].astype(F32)
    y_s, ug_s, uv_s = _ffn(h2_s, x1_s, wu, w_conv[l], b_conv2d, wd, g_post_ffn[l],
                           bs, tf, past=(past[:, 0, :], past[:, 1, :]))

    y_sample = y_s.reshape(bs, ts, d)
    m_sample = m_s3[:, 0, :N_MH][None]
    k_sample = k_s.reshape(bs, WINDOW, N_KV, HD)[None]
    v_sample = v_s.reshape(bs, WINDOW, N_KV, HD)[None]
    conv_sample = jnp.stack([past[:, 1, :], jnp.concatenate([ug_s, uv_s], axis=-1)], axis=1)[None]

    return (y_prompt, y_sample,
            c_prompt, n_prompt, m_prompt, k_prompt, v_prompt, conv_prompt,
            c_sample, n_sample, m_sample, k_sample, v_sample, conv_sample)
```

```python
import functools

import jax
import jax.numpy as jnp
from jax import lax
from jax.experimental import pallas as pl
from jax.experimental.pallas import tpu as pltpu

F32 = jnp.float32
BF16 = jnp.bfloat16

N_MH = 4
DK = 256
DV = 512
CHUNK = 64
HD = 64
N_Q = 32
N_KV = 4
GROUP = N_Q // N_KV
WINDOW = 128
ROT = 16
THETA = 500000.0
PAST_LEN = 16384
EPS = 1e-6
D_MODEL = 2048
QK_W = N_MH * DK
V_W = N_MH * DV
ATT_Q = N_Q * HD
ATT_KV = N_KV * HD

_Z_QM = 0
_Z_KM = _Z_QM + QK_W
_Z_VM = _Z_KM + QK_W
_Z_OM = _Z_VM + V_W
_Z_QA = _Z_OM + V_W
_Z_GA = _Z_QA + ATT_Q
_Z_GB = _Z_GA + D_MODEL
_Z_KA = _Z_GB + D_MODEL
_Z_VA = _Z_KA + ATT_KV
_Z_IF = _Z_VA + ATT_KV
LANES = 128
_PROJ_TN = 1024
_Z_W = -(-(_Z_IF + LANES) // _PROJ_TN) * _PROJ_TN
assert _Z_QA % _PROJ_TN == 0 and _Z_GA % _PROJ_TN == 0 and _Z_KA % _PROJ_TN == 0

_VMEM_LIMIT = 56 * 1024 * 1024


def _cparams(sem):
    return pltpu.CompilerParams(dimension_semantics=sem, vmem_limit_bytes=_VMEM_LIMIT)


def _rms(x, g):
    return x * lax.rsqrt(jnp.mean(x * x, axis=-1, keepdims=True) + EPS) * g


def _sigmoid(x):
    return 1.0 / (1.0 + jnp.exp(-x))


def _log_sigmoid(x):
    return jnp.minimum(x, 0.0) - jnp.log1p(jnp.exp(-jnp.abs(x)))


def _dot(a, b):
    return jnp.dot(a, b, preferred_element_type=F32)


def _dot_nt(a, b):
    return lax.dot_general(a, b, (((1,), (1,)), ((), ())), preferred_element_type=F32)


def _dot_tn(a, b):
    return lax.dot_general(a, b, (((0,), (0,)), ((), ())), preferred_element_type=F32)


def _proj_kernel(x_ref, g_ref, wlo_ref, whi_ref, cos_ref, sa_ref, sb_ref, z_ref, h_ref):
    j = pl.program_id(1)
    jq0 = _Z_QA // _PROJ_TN
    jq1 = _Z_GA // _PROJ_TN
    jk = _Z_KA // _PROJ_TN

    @pl.when(j == 0)
    def _():
        h_ref[...] = _rms(x_ref[...], g_ref[...]).astype(BF16)

    is_lo = j < jq0
    is_q = (j >= jq0) & (j < jq1)
    is_k = j == jk

    @pl.when(is_lo)
    def _():
        z_ref[...] = _dot(h_ref[...], wlo_ref[...])

    @pl.when(is_q)
    def _():
        acc = _dot(h_ref[...], whi_ref[...])
        z_ref[...] = _rope(acc, cos_ref[...], sa_ref[...], sb_ref[...]) * (HD ** -0.5)

    @pl.when(is_k)
    def _():
        acc = _dot(h_ref[...], whi_ref[...])
        z_ref[...] = acc
        z_ref[:, :ATT_KV] = _rope(acc[:, :ATT_KV], cos_ref[...], sa_ref[...], sb_ref[...])

    @pl.when(jnp.logical_not(is_lo | is_q | is_k))
    def _():
        z_ref[...] = _dot(h_ref[...], whi_ref[...])


def _project(x2d, g, w_lo, w_hi, tables, tm):
    rows, d = x2d.shape
    tn = _PROJ_TN
    n_lo = _Z_QA // tn
    n = n_lo * tn + w_hi.shape[1]
    assert n == _Z_W and w_lo.shape[1] >= n_lo * tn
    tab_tiles = tables[0].shape[0] // tm
    tab = pl.BlockSpec((tm, LANES), lambda i, j: (i % tab_tiles, 0))
    return pl.pallas_call(
        _proj_kernel,
        grid=(rows // tm, n // tn),
        in_specs=[
            pl.BlockSpec((tm, d), lambda i, j: (i, 0)),
            pl.BlockSpec((1, d), lambda i, j: (0, 0)),
            pl.BlockSpec((d, tn), lambda i, j: (0, jnp.minimum(j, n_lo - 1))),
            pl.BlockSpec((d, tn), lambda i, j: (0, jnp.maximum(j - n_lo, 0))),
            tab, tab, tab,
        ],
        out_specs=pl.BlockSpec((tm, tn), lambda i, j: (i, j)),
        out_shape=jax.ShapeDtypeStruct((rows, n), F32),
        scratch_shapes=[pltpu.VMEM((tm, d), BF16)],
        compiler_params=_cparams(("parallel", "arbitrary")),
        name="proj",
    )(x2d, g.reshape(1, d), w_lo, w_hi, *tables)


def _mlstm_prompt_kernel(q_ref, k_ref, v_ref, o_ref, g_ref, bias_ref,
                         hm_ref, c_out, n_out, m_out, ct_s, n_s, m_s):
    c = pl.program_id(1)
    last = pl.num_programs(1) - 1
    L = CHUNK
    nseq = q_ref.shape[0]

    @pl.when(c == 0)
    def _():
        ct_s[...] = jnp.zeros_like(ct_s)
        n_s[...] = jnp.zeros_like(n_s)
        m_s[...] = jnp.zeros_like(m_s)

    row = lax.broadcasted_iota(jnp.int32, (L, LANES), 0)
    ri = lax.broadcasted_iota(jnp.int32, (L, L), 0)
    ci = lax.broadcasted_iota(jnp.int32, (L, L), 1)
    eye = ri == ci
    causal = ci <= ri

    heads = []
    for bi in range(nseq):
        gates = g_ref[bi] + bias_ref[...]
        bcum = _log_sigmoid(gates)
        for d in (1, 2, 4, 8, 16, 32):
            bcum = bcum + jnp.where(row >= d, pltpu.roll(bcum, d, axis=0), 0.0)
        for h in range(N_MH):
            q = q_ref[bi, :, h * DK:(h + 1) * DK]
            k = k_ref[bi, :, h * DK:(h + 1) * DK] * (DK ** -0.5)
            qb = q.astype(BF16)
            ct = ct_s[bi * N_MH + h]
            heads.append((bi, h, gates, bcum, q, k, _dot_nt(qb, k.astype(BF16)), _dot(qb, ct.astype(BF16))))

    for bi, h, gates, bcum, q, k, qk, cq in heads:
        v = v_ref[bi, :, h * DV:(h + 1) * DV]
        o = o_ref[bi, :, h * DV:(h + 1) * DV]
        b_col = bcum[:, N_MH + h:N_MH + h + 1]
        ig_col = gates[:, h:h + 1]
        a_col = ig_col - b_col
        a_lane = jnp.sum(jnp.where(eye, a_col, 0.0), axis=0, keepdims=True)
        dmat = jnp.where(causal, b_col + a_lane, -jnp.inf)
        m_prev = m_s[bi, h:h + 1, 0:1]
        inter = b_col + m_prev
        m_t = jnp.maximum(jnp.max(dmat, axis=1, keepdims=True), inter)
        vb = v.astype(BF16)
        s = qk * jnp.exp(dmat - m_t)
        w_inter = jnp.exp(inter - m_t)
        m_new = m_t[L - 1:L, :]
        b_last = b_col[L - 1:L, :]
        w_last = jnp.exp(b_last - b_col + ig_col - m_new)
        decay = jnp.exp(b_last + m_prev - m_new)
        kw = k * w_last
        ct = ct_s[bi * N_MH + h]
        ct_s[bi * N_MH + h] = decay * ct + _dot_tn(kw.astype(BF16), vb)
        num = _dot(s.astype(BF16), vb) + w_inter * cq
        n_row = n_s[bi, h:h + 1, :]
        den = jnp.sum(s, axis=1, keepdims=True) + w_inter * jnp.sum(q * n_row, axis=1, keepdims=True)
        hout = num / jnp.maximum(jnp.abs(den), jnp.exp(-m_t))
        hm_ref[bi, :, h * DV:(h + 1) * DV] = (_sigmoid(o) * hout).astype(BF16)
        n_s[bi, h:h + 1, :] = decay * n_row + jnp.sum(kw, axis=0, keepdims=True)
        m_s[bi, h:h + 1, :] = jnp.broadcast_to(m_new, (1, LANES))

    @pl.when(c == last)
    def _():
        for bi in range(nseq):
            for h in range(N_MH):
                c_out[bi, h] = ct_s[bi * N_MH + h].T
        n_out[...] = n_s[:, :N_MH, :]
        m_out[...] = m_s[...]


def _mlstm_prompt(z, bias_row, batch, seq, bb):
    nc = seq // CHUNK
    z3 = z.reshape(batch, seq, z.shape[1])

    def zspec(width, off):
        blk = off // width
        return pl.BlockSpec((bb, CHUNK, width), lambda b, c: (b, c, blk))

    return pl.pallas_call(
        _mlstm_prompt_kernel,
        grid=(batch // bb, nc),
        in_specs=[
            zspec(QK_W, _Z_QM), zspec(QK_W, _Z_KM), zspec(V_W, _Z_VM), zspec(V_W, _Z_OM),
            zspec(LANES, _Z_IF),
            pl.BlockSpec((1, LANES), lambda b, c: (0, 0)),
        ],
        out_specs=[
            pl.BlockSpec((bb, CHUNK, V_W), lambda b, c: (b, c, 0)),
            pl.BlockSpec((bb, N_MH, DV, DK), lambda b, c: (b, 0, 0, 0)),
            pl.BlockSpec((bb, N_MH, DK), lambda b, c: (b, 0, 0)),
            pl.BlockSpec((bb, 8, LANES), lambda b, c: (b, 0, 0)),
        ],
        out_shape=[
            jax.ShapeDtypeStruct((batch, seq, V_W), BF16),
            jax.ShapeDtypeStruct((batch, N_MH, DV, DK), F32),
            jax.ShapeDtypeStruct((batch, N_MH, DK), F32),
            jax.ShapeDtypeStruct((batch, 8, LANES), F32),
        ],
        scratch_shapes=[
            pltpu.VMEM((bb * N_MH, DK, DV), F32),
            pltpu.VMEM((bb, 8, DK), F32),
            pltpu.VMEM((bb, 8, LANES), F32),
        ],
        compiler_params=_cparams(("parallel", "arbitrary")),
        name="mlstm_prompt",
    )(z3, z3, z3, z3, z3, bias_row)


def _mlstm_sample_kernel(q_ref, k_ref, v_ref, o_ref, g_ref, bias_ref, c_ref, n_ref, m_ref,
                         hm_ref, c_out, n_out, m_out):
    ri = lax.broadcasted_iota(jnp.int32, (DV, DV), 0)
    ci = lax.broadcasted_iota(jnp.int32, (DV, DV), 1)
    eye = ri == ci
    lane = lax.broadcasted_iota(jnp.int32, (1, LANES), 1)

    for b in range(q_ref.shape[0]):
        gates = g_ref[b] + bias_ref[...]
        lf_all = _log_sigmoid(gates)
        m_all = m_ref[b]
        m_new_all = jnp.zeros((1, LANES), F32)
        for h in range(N_MH):
            q = q_ref[b, :, h * DK:(h + 1) * DK]
            k = k_ref[b, :, h * DK:(h + 1) * DK] * (DK ** -0.5)
            v = v_ref[b, :, h * DV:(h + 1) * DV]
            o = o_ref[b, :, h * DV:(h + 1) * DV]
            ig = gates[:, h:h + 1]
            lf = lf_all[:, N_MH + h:N_MH + h + 1]
            m_prev = m_all[:, h:h + 1]
            inter = lf + m_prev
            m_t = jnp.maximum(ig, inter)
            s = jnp.sum(q * k, axis=1, keepdims=True) * jnp.exp(ig - m_t)
            w_inter = jnp.exp(inter - m_t)
            cmat = c_ref[0, b, h]
            q8 = jnp.broadcast_to(q, (8, DK)).astype(BF16)
            cq = _dot_nt(q8, cmat.astype(BF16))[0:1, :]
            n_row = n_ref[0, b, h:h + 1, :]
            num = s * v + w_inter * cq
            den = s + w_inter * jnp.sum(n_row * q, axis=1, keepdims=True)
            hout = num / jnp.maximum(jnp.abs(den), jnp.exp(-m_t))
            hm_ref[b, :, h * DV:(h + 1) * DV] = (_sigmoid(o) * hout).astype(BF16)

            w_last = jnp.exp(ig - m_t)
            decay = jnp.exp(inter - m_t)
            vw = v * w_last
            vw_col = jnp.sum(jnp.where(eye, vw, 0.0), axis=1, keepdims=True)
            c_out[0, b, h] = decay * cmat + vw_col * k
            n_out[0, b, h:h + 1, :] = decay * n_row + w_last * k
            m_new_all = jnp.where(lane == h, m_t, m_new_all)
        m_out[b] = m_new_all


def _mlstm_sample(z3, bias_row, c0, n0, m0_pad, layer, bb):
    nb = z3.shape[0]

    def zspec(width, off):
        blk = off // width
        return pl.BlockSpec((bb, 1, width), lambda b: (b, 0, blk))

    c_in = pl.BlockSpec((1, bb, N_MH, DV, DK), lambda b: (layer, b, 0, 0, 0))
    n_in = pl.BlockSpec((1, bb, N_MH, DK), lambda b: (layer, b, 0, 0))
    c_blk = pl.BlockSpec((1, bb, N_MH, DV, DK), lambda b: (0, b, 0, 0, 0))
    n_blk = pl.BlockSpec((1, bb, N_MH, DK), lambda b: (0, b, 0, 0))

    return pl.pallas_call(
        _mlstm_sample_kernel,
        grid=(nb // bb,),
        in_specs=[
            zspec(QK_W, _Z_QM), zspec(QK_W, _Z_KM), zspec(V_W, _Z_VM), zspec(V_W, _Z_OM),
            zspec(LANES, _Z_IF),
            pl.BlockSpec((1, LANES), lambda b: (0, 0)),
            c_in,
            n_in,
            pl.BlockSpec((bb, 1, LANES), lambda b: (b, 0, 0)),
        ],
        out_specs=[
            pl.BlockSpec((bb, 1, V_W), lambda b: (b, 0, 0)),
            c_blk,
            n_blk,
            pl.BlockSpec((bb, 1, LANES), lambda b: (b, 0, 0)),
        ],
        out_shape=[
            jax.ShapeDtypeStruct((nb, 1, V_W), BF16),
            jax.ShapeDtypeStruct((1, nb, N_MH, DV, DK), F32),
            jax.ShapeDtypeStruct((1, nb, N_MH, DK), F32),
            jax.ShapeDtypeStruct((nb, 1, LANES), F32),
        ],
        compiler_params=_cparams(("parallel",)),
        name="mlstm_sample",
    )(z3, z3, z3, z3, z3, bias_row, c0, n0, m0_pad)


def _rope_tables(pos):
    half = ROT // 2
    inv = THETA ** (-jnp.arange(half, dtype=F32) * 2.0 / ROT)
    ang = pos.astype(F32)[:, None] * inv[None, :]
    cos = jnp.cos(ang)
    sin = jnp.sin(ang)
    t = pos.shape[0]
    cos_t = jnp.concatenate([cos, cos, jnp.ones((t, HD - ROT), F32)], axis=1)
    sin_a = jnp.concatenate([-sin, jnp.zeros((t, HD - half), F32)], axis=1)
    sin_b = jnp.concatenate([jnp.zeros((t, half), F32), sin, jnp.zeros((t, HD - ROT), F32)], axis=1)
    rep = LANES // HD
    return tuple(jnp.tile(a, (1, rep)) for a in (cos_t, sin_a, sin_b))


def _rope(x, cos_t, sin_a, sin_b):
    w = x.shape[1]
    rep = w // LANES
    half = ROT // 2
    if rep > 1:
        cos_t = jnp.concatenate([cos_t] * rep, axis=1)
        sin_a = jnp.concatenate([sin_a] * rep, axis=1)
        sin_b = jnp.concatenate([sin_b] * rep, axis=1)
    up = pltpu.roll(x, w - half, axis=1)
    down = pltpu.roll(x, half, axis=1)
    return x * cos_t + up * sin_a + down * sin_b


def _swa_prompt_kernel(sink_ref, q_ref, kc_ref, kp_ref, vc_ref, vp_ref, ha_ref):
    i = pl.program_id(1)
    w = WINDOW
    kk = jnp.concatenate([kp_ref[...], kc_ref[...]], axis=0)
    vv = jnp.concatenate([vp_ref[...], vc_ref[...]], axis=0)
    vv_t = vv.T.astype(BF16)
    s_idx = lax.broadcasted_iota(jnp.int32, (2 * w, w), 0)
    a_idx = lax.broadcasted_iota(jnp.int32, (2 * w, w), 1)
    valid = (s_idx >= a_idx) & (s_idx <= a_idx + w) & ((i > 0) | (s_idx >= w))
    lane = lax.broadcasted_iota(jnp.int32, (2 * w, LANES), 1)
    pair_w = 2 * HD
    for half in range(N_KV // 2):
        kh = kk[:, half * pair_w:(half + 1) * pair_w]
        kh_sw = pltpu.roll(kh, HD, axis=1)
        for gg in range(2):
            g = 2 * half + gg
            k_lo = jnp.where(lane < HD, kh if gg == 0 else kh_sw, 0.0).astype(BF16)
            k_hi = jnp.where(lane >= HD, kh_sw if gg == 0 else kh, 0.0).astype(BF16)
            vg_t = vv_t[g * HD:(g + 1) * HD, :]
            sts = []
            for a in range(GROUP // 2):
                col = (g * GROUP + 2 * a) * HD
                qpair = q_ref[:, col:col + pair_w].astype(BF16)
                sts += [_dot_nt(k_lo, qpair), _dot_nt(k_hi, qpair)]
            es, rs = [], []
            for hh in range(GROUP):
                sink = sink_ref[g * GROUP + hh]
                st = jnp.where(valid, sts[hh], -jnp.inf)
                m = jnp.maximum(jnp.max(st, axis=0, keepdims=True), sink)
                e = jnp.exp(st - m)
                es.append(e.astype(BF16))
                rs.append(1.0 / (jnp.sum(e, axis=0, keepdims=True) + jnp.exp(sink - m)))
            ots = [_dot(vg_t, es[hh]) * rs[hh] for hh in range(GROUP)]
            for a in range(GROUP // 2):
                col = (g * GROUP + 2 * a) * HD
                ha_ref[:, col:col + pair_w] = jnp.concatenate(ots[2 * a:2 * a + 2], axis=0).T.astype(BF16)


def _swa_prompt(z, sinks3, batch, seq):
    rows = z.shape[0]
    nb = seq // WINDOW
    w = WINDOW

    def cur(width, off):
        blk = off // width
        return pl.BlockSpec((w, width), lambda b, i: (b * nb + i, blk))

    def prev(width, off):
        blk = off // width
        return pl.BlockSpec((w, width), lambda b, i: (b * nb + jnp.maximum(i - 1, 0), blk))

    return pl.pallas_call(
        _swa_prompt_kernel,
        grid=(batch, nb),
        in_specs=[
            pl.BlockSpec(memory_space=pltpu.SMEM),
            cur(ATT_Q, _Z_QA), cur(ATT_KV, _Z_KA), prev(ATT_KV, _Z_KA),
            cur(ATT_KV, _Z_VA), prev(ATT_KV, _Z_VA),
        ],
        out_specs=pl.BlockSpec((w, ATT_Q), lambda b, i: (b * nb + i, 0)),
        out_shape=jax.ShapeDtypeStruct((rows, ATT_Q), BF16),
        compiler_params=_cparams(("parallel", "parallel")),
        name="swa_prompt",
    )(sinks3, z, z, z, z, z)


def _swa_sample_kernel(q_ref, kn_ref, vn_ref, kp_ref, vp_ref, sink_ref,
                       o_ref, ko_ref, vo_ref, *, bb):
    w = WINDOW
    hrow = lax.broadcasted_iota(jnp.int32, (N_Q, ATT_KV), 0) // GROUP
    glane = lax.broadcasted_iota(jnp.int32, (N_Q, ATT_KV), 1) // HD
    own = hrow == glane
    srow = lax.broadcasted_iota(jnp.int32, (w, ATT_KV), 0)
    sink = sink_ref[...]
    for b in range(bb):
        q = q_ref[b]
        qm = jnp.where(own, jnp.concatenate([q] * N_KV, axis=1), 0.0)
        k_past = kp_ref[b]
        v_past = vp_ref[b]
        k_new = kn_ref[b]
        v_new = vn_ref[b]
        s_past = _dot_nt(qm.astype(BF16), k_past.astype(BF16))
        s_new = jnp.sum(qm * k_new, axis=1, keepdims=True)
        m = jnp.maximum(jnp.maximum(jnp.max(s_past, axis=1, keepdims=True), s_new), sink)
        e_past = jnp.exp(s_past - m)
        e_new = jnp.exp(s_new - m)
        denom = jnp.sum(e_past, axis=1, keepdims=True) + e_new + jnp.exp(sink - m)
        r = 1.0 / denom
        pv = _dot((e_past * r).astype(BF16), v_past.astype(BF16)) + (e_new * r) * v_new
        pv = jnp.where(own, pv, 0.0)
        o = pv[:, 0:HD]
        for g in range(1, N_KV):
            o = o + pv[:, g * HD:(g + 1) * HD]
        o_ref[b] = o
        ko_ref[b] = jnp.where(srow == w - 1, k_new, pltpu.roll(k_past, w - 1, axis=0))
        vo_ref[b] = jnp.where(srow == w - 1, v_new, pltpu.roll(v_past, w - 1, axis=0))


def _swa_sample(q3, k_new3, v_new3, k_past, v_past, sink_col, bb):
    nb = q3.shape[0]
    w = WINDOW
    return pl.pallas_call(
        functools.partial(_swa_sample_kernel, bb=bb),
        grid=(nb // bb,),
        in_specs=[
            pl.BlockSpec((bb, N_Q, HD), lambda i: (i, 0, 0)),
            pl.BlockSpec((bb, 1, ATT_KV), lambda i: (i, 0, _Z_KA // ATT_KV)),
            pl.BlockSpec((bb, 1, ATT_KV), lambda i: (i, 0, _Z_VA // ATT_KV)),
            pl.BlockSpec((bb, w, ATT_KV), lambda i: (i, 0, 0)),
            pl.BlockSpec((bb, w, ATT_KV), lambda i: (i, 0, 0)),
            pl.BlockSpec((N_Q, 1), lambda i: (0, 0)),
        ],
        out_specs=[
            pl.BlockSpec((bb, N_Q, HD), lambda i: (i, 0, 0)),
            pl.BlockSpec((bb, w, ATT_KV), lambda i: (i, 0, 0)),
            pl.BlockSpec((bb, w, ATT_KV), lambda i: (i, 0, 0)),
        ],
        out_shape=[
            jax.ShapeDtypeStruct((nb, N_Q, HD), F32),
            jax.ShapeDtypeStruct((nb, w, ATT_KV), F32),
            jax.ShapeDtypeStruct((nb, w, ATT_KV), F32),
        ],
        compiler_params=_cparams(("parallel",)),
        name="swa_sample",
    )(q3, k_new3, v_new3, k_past, v_past, sink_col)


def _merge1_kernel(hm_ref, ha_ref, ga_ref, gb_ref, wa_ref, wb_ref, t_ref):
    a = _dot(hm_ref[...], wa_ref[...])
    b = _dot(ha_ref[...], wb_ref[...])
    t_ref[...] = (_sigmoid(ga_ref[...]) * a + _sigmoid(gb_ref[...]) * b).astype(BF16)


def _merge1(hm, ha, z, wa, wb, tm, tn):
    rows, d = hm.shape
    n = wa.shape[1]
    return pl.pallas_call(
        _merge1_kernel,
        grid=(rows // tm, n // tn),
        in_specs=[
            pl.BlockSpec((tm, d), lambda i, j: (i, 0)),
            pl.BlockSpec((tm, d), lambda i, j: (i, 0)),
            pl.BlockSpec((tm, tn), lambda i, j: (i, _Z_GA // tn + j)),
            pl.BlockSpec((tm, tn), lambda i, j: (i, _Z_GB // tn + j)),
            pl.BlockSpec((d, tn), lambda i, j: (0, j)),
            pl.BlockSpec((d, tn), lambda i, j: (0, j)),
        ],
        out_specs=pl.BlockSpec((tm, tn), lambda i, j: (i, j)),
        out_shape=jax.ShapeDtypeStruct((rows, n), BF16),
        compiler_params=_cparams(("parallel", "arbitrary")),
        name="merge1",
    )(hm, ha, z, z, wa, wb)


def _merge2_kernel(t_ref, w_ref, x_ref, g1_ref, g2_ref, x1_ref, h2_ref):
    mix = _dot(t_ref[...], w_ref[...])
    x1 = x_ref[...] + _rms(mix, g1_ref[...])
    x1_ref[...] = x1
    h2_ref[...] = _rms(x1, g2_ref[...]).astype(BF16)


def _merge2(t, w_out, x2d, g1, g2, tm):
    rows, d = x2d.shape
    row = lambda i: (i, 0)
    const = lambda i: (0, 0)
    return pl.pallas_call(
        _merge2_kernel,
        grid=(rows // tm,),
        in_specs=[
            pl.BlockSpec((tm, d), row),
            pl.BlockSpec((d, d), const, pipeline_mode=pl.Buffered(1)),
            pl.BlockSpec((tm, d), row),
            pl.BlockSpec((1, d), const),
            pl.BlockSpec((1, d), const),
        ],
        out_specs=[pl.BlockSpec((tm, d), row), pl.BlockSpec((tm, d), row)],
        out_shape=[jax.ShapeDtypeStruct((rows, d), F32), jax.ShapeDtypeStruct((rows, d), BF16)],
        compiler_params=_cparams(("parallel",)),
        name="merge2",
    )(t, w_out, x2d, g1.reshape(1, d), g2.reshape(1, d))


def _gelu_tanh(x):
    return 0.5 * x * (1.0 + jnp.tanh(0.7978845608028654 * (x + 0.044715 * (x * x * x))))


def _conv_taps(u, u1, u2, cw_ref, cb_ref, cs=slice(None)):
    return cb_ref[:, cs] + u2 * cw_ref[0:1, cs] + u1 * cw_ref[1:2, cs] + u * cw_ref[2:3, cs]


def _ffn_tail(j, nf, y, wd_ref, x1_ref, g_ref, y_ref, acc_ref):
    @pl.when(j == 0)
    def _():
        acc_ref[...] = _dot(y, wd_ref[...])

    @pl.when(j > 0)
    def _():
        acc_ref[...] += _dot(y, wd_ref[...])

    @pl.when(j == nf - 1)
    def _():
        y_ref[...] = x1_ref[...] + _rms(acc_ref[...], g_ref[...])


def _ffn_prompt_kernel(h_ref, wg_ref, wv_ref, cwg_ref, cwv_ref, cbg_ref, cbv_ref, wd_ref,
                       x1_ref, g_ref, y_ref, sg_ref, sv_ref, acc_ref, carry_ref,
                       *, tiles_per_seq):
    i = pl.program_id(0)
    j = pl.program_id(1)
    nf = pl.num_programs(1)
    tm = h_ref.shape[0]

    @pl.when((i == 0) & (j == 0))
    def _():
        carry_ref[...] = jnp.zeros_like(carry_ref)

    seq_start = (i % tiles_per_seq) == 0
    h = h_ref[...]

    def branch(w_ref, cw_ref, cb_ref, slot, state_ref):
        u = _dot(h, w_ref[...])
        prev = jnp.where(seq_start, 0.0, carry_ref[slot])
        uu = jnp.concatenate([prev, u], axis=0)
        u1 = uu[7:7 + tm, :]
        u2 = uu[6:6 + tm, :]
        tail = u[tm - 8:tm, :]
        carry_ref[slot] = tail
        state_ref[0] = tail
        return _conv_taps(u, u1, u2, cw_ref, cb_ref)

    cg = branch(wg_ref, cwg_ref, cbg_ref, 2 * j, sg_ref)
    cv = branch(wv_ref, cwv_ref, cbv_ref, 2 * j + 1, sv_ref)
    y = (_gelu_tanh(cg) * cv).astype(BF16)
    _ffn_tail(j, nf, y, wd_ref, x1_ref, g_ref, y_ref, acc_ref)


def _ffn_sample_kernel(h_ref, wg_ref, wv_ref, cwg_ref, cwv_ref, cbg_ref, cbv_ref, wd_ref,
                       x1_ref, g_ref, p2g_ref, p2v_ref, p1g_ref, p1v_ref,
                       y_ref, ug_ref, uv_ref, acc_ref):
    j = pl.program_id(1)
    nf = pl.num_programs(1)
    h = h_ref[...]
    ug = _dot(h, wg_ref[...])
    uv = _dot(h, wv_ref[...])
    ug_ref[...] = ug
    uv_ref[...] = uv
    cg = _conv_taps(ug, p1g_ref[...], p2g_ref[...], cwg_ref, cbg_ref)
    cv = _conv_taps(uv, p1v_ref[...], p2v_ref[...], cwv_ref, cbv_ref)
    y = (_gelu_tanh(cg) * cv).astype(BF16)
    _ffn_tail(j, nf, y, wd_ref, x1_ref, g_ref, y_ref, acc_ref)


def _ffn(h2, x1, w_up, w_conv, b_conv2d, w_down, g, tm, tf, seq=None, past=None):
    rows, d = h2.shape
    f = w_down.shape[0]
    nf = f // tf
    row = lambda i, j: (i, 0)
    gcol = lambda i, j: (0, j)
    vcol = lambda i, j: (0, nf + j)
    in_specs = [
        pl.BlockSpec((tm, d), row),
        pl.BlockSpec((d, tf), gcol), pl.BlockSpec((d, tf), vcol),
        pl.BlockSpec((3, tf), gcol), pl.BlockSpec((3, tf), vcol),
        pl.BlockSpec((1, tf), gcol), pl.BlockSpec((1, tf), vcol),
        pl.BlockSpec((tf, d), lambda i, j: (j, 0)),
        pl.BlockSpec((tm, d), row),
        pl.BlockSpec((1, d), lambda i, j: (0, 0)),
    ]
    args = [h2, w_up, w_up, w_conv, w_conv, b_conv2d, b_conv2d, w_down, x1, g.reshape(1, d)]
    scratch = [pltpu.VMEM((tm, d), F32)]
    if past is None:
        tiles_per_seq = seq // tm
        kern = functools.partial(_ffn_prompt_kernel, tiles_per_seq=tiles_per_seq)
        state_spec = pl.BlockSpec((1, 8, tf), lambda i, j: (i, 0, j))
        out_specs = [pl.BlockSpec((tm, d), row), state_spec, state_spec]
        out_shape = [jax.ShapeDtypeStruct((rows, d), F32),
                     jax.ShapeDtypeStruct((rows // tm, 8, f), F32),
                     jax.ShapeDtypeStruct((rows // tm, 8, f), F32)]
        scratch.append(pltpu.VMEM((2 * nf, 8, tf), F32))
        name = "ffn_prompt"
    else:
        p2, p1 = past
        kern = _ffn_sample_kernel
        ucol = pl.BlockSpec((tm, tf), lambda i, j: (i, j))
        in_specs += [
            pl.BlockSpec((tm, tf), lambda i, j: (i, j)), pl.BlockSpec((tm, tf), lambda i, j: (i, nf + j)),
            pl.BlockSpec((tm, tf), lambda i, j: (i, j)), pl.BlockSpec((tm, tf), lambda i, j: (i, nf + j)),
        ]
        args += [p2, p2, p1, p1]
        out_specs = [pl.BlockSpec((tm, d), row), ucol, ucol]
        out_shape = [jax.ShapeDtypeStruct((rows, d), F32),
                     jax.ShapeDtypeStruct((rows, f), F32),
                     jax.ShapeDtypeStruct((rows, f), F32)]
        name = "ffn_sample"
    return pl.pallas_call(
        kern,
        grid=(rows // tm, nf),
        in_specs=in_specs,
        out_specs=out_specs,
        out_shape=out_shape,
        scratch_shapes=scratch,
        compiler_params=_cparams(("arbitrary", "arbitrary")),
        name=name,
    )(*args)


def _regroup_w_in(w_in):
    d = w_in.shape[0]
    o_if = QK_W * 2 + V_W * 2
    o_qa = o_if + 2 * N_MH
    o_ka = o_qa + ATT_Q
    o_ga = o_ka + 2 * ATT_KV
    assert o_if == _Z_QA
    parts = [
        w_in[:, o_qa:o_ka],
        w_in[:, o_ga:],
        w_in[:, o_ka:o_ga],
        w_in[:, o_if:o_qa],
    ]
    used = o_if + sum(p.shape[1] for p in parts)
    parts.append(jnp.zeros((d, _Z_W - used), w_in.dtype))
    return jnp.concatenate(parts, axis=1)


def _pick(rows, pref):
    t = min(rows, pref)
    while rows % t:
        t //= 2
    return t


def kernel(x_prompt, x_sample, state_mlstm_c, state_mlstm_n, state_mlstm_m, cache_swa_k, cache_swa_v,
           state_ffn_conv, g_pre_mix, w_in, b_if, attn_sinks, w_branch_a, w_branch_b, w_out, g_post_mix,
           g_pre_ffn, w_up, w_conv, b_conv, w_down, g_post_ffn):
    bp, tp, d = x_prompt.shape
    bs, ts, _ = x_sample.shape
    depth = w_in.shape[0]
    assert depth == 1 and ts == 1 and d == D_MODEL
    f = w_down.shape[1]
    tf = 512
    assert f % tf == 0

    l = 0
    w_lo = w_in[l].astype(BF16)
    w_hi = _regroup_w_in(w_lo)
    wa = w_branch_a[l].astype(BF16)
    wb = w_branch_b[l].astype(BF16)
    wo = w_out[l].astype(BF16)
    wu = w_up[l].astype(BF16)
    wd = w_down[l].astype(BF16)
    bias_row = jnp.concatenate([b_if[l].astype(F32), jnp.zeros((LANES - 2 * N_MH,), F32)]).reshape(1, LANES)
    sinks = attn_sinks[l].astype(F32)
    b_conv2d = b_conv[l].reshape(1, 2 * f)

    xp = x_prompt.reshape(bp * tp, d)
    zp = _project(xp, g_pre_mix[l], w_lo, w_hi, _rope_tables(jnp.arange(tp)), _pick(tp, 1024))
    hm_p3, c_p, n_p, m_p = _mlstm_prompt(zp, bias_row, bp, tp, _pick(bp, 4))
    hm_p = hm_p3.reshape(bp * tp, V_W)
    ha_p = _swa_prompt(zp, sinks, bp, tp)
    t_p = _merge1(hm_p, ha_p, zp, wa, wb, _pick(bp * tp, 1024), 512)
    x1_p, h2_p = _merge2(t_p, wo, xp, g_post_mix[l], g_pre_ffn[l], _pick(bp * tp, 512))
    tm_ffn = _pick(tp, 512)
    y_p, sg_p, sv_p = _ffn(h2_p, x1_p, wu, w_conv[l], b_conv2d, wd, g_post_ffn[l],
                           tm_ffn, tf, seq=tp)
    seq_end = slice(tp // tm_ffn - 1, None, tp // tm_ffn)

    y_prompt = y_p.reshape(bp, tp, d)
    c_prompt = c_p[None]
    n_prompt = n_p[None]
    m_prompt = m_p[:, :N_MH, 0][None]
    kv_tail = zp.reshape(bp, tp, _Z_W)[:, tp - WINDOW:, _Z_KA:_Z_KA + 2 * ATT_KV]
    k_prompt = kv_tail[..., :ATT_KV].reshape(bp, WINDOW, N_KV, HD)[None]
    v_prompt = kv_tail[..., ATT_KV:].reshape(bp, WINDOW, N_KV, HD)[None]
    conv_prompt = jnp.concatenate([sg_p[seq_end, 6:8, :], sv_p[seq_end, 6:8, :]], axis=-1)[None]

    xs = x_sample.reshape(bs, d)
    tabs_s = tuple(jnp.broadcast_to(a, (bs, LANES)) for a in _rope_tables(PAST_LEN + jnp.arange(ts)))
    zs = _project(xs, g_pre_mix[l], w_lo, w_hi, tabs_s, bs)
    zs3 = zs.reshape(bs, 1, _Z_W)
    m0_pad = jnp.pad(state_mlstm_m[l].astype(F32), ((0, 0), (0, LANES - N_MH))).reshape(bs, 1, LANES)
    hm_s3, c_sample, n_sample, m_s3 = _mlstm_sample(zs3, bias_row, state_mlstm_c, state_mlstm_n, m0_pad, l,
                                                          _pick(bs, 4))
    o_s3, k_s, v_s = _swa_sample(
        zs[:, _Z_QA:_Z_QA + ATT_Q].reshape(bs, N_Q, HD), zs3, zs3,
        cache_swa_k[l].reshape(bs, WINDOW, ATT_KV), cache_swa_v[l].reshape(bs, WINDOW, ATT_KV),
        sinks.reshape(N_Q, 1), _pick(bs, 8))
    ha_s = o_s3.reshape(bs, ATT_Q).astype(BF16)
    t_s = _merge1(hm_s3.reshape(bs, V_W), ha_s, zs, wa, wb, bs, 512)
    x1_s, h2_s = _merge2(t_s, wo, xs, g_post_mix[l], g_pre_ffn[l], bs)
    past = state_ffn_conv[l].astype(F32)
    y_s, ug_s, uv_s = _ffn(h2_s, x1_s, wu, w_conv[l], b_conv2d, wd, g_post_ffn[l],
                           bs, tf, past=(past[:, 0, :], past[:, 1, :]))

    y_sample = y_s.reshape(bs, ts, d)
    m_sample = m_s3[:, 0, :N_MH][None]
    k_sample = k_s.reshape(bs, WINDOW, N_KV, HD)[None]
    v_sample = v_s.reshape(bs, WINDOW, N_KV, HD)[None]
    conv_sample = jnp.stack([past[:, 1, :], jnp.concatenate([ug_s, uv_s], axis=-1)], axis=1)[None]

    return (y_prompt, y_sample,
            c_prompt, n_prompt, m_prompt, k_prompt, v_prompt, conv_prompt,
            c_sample, n_sample, m_sample, k_sample, v_sample, conv_sample)
```

```python
import functools

import jax
import jax.numpy as jnp
from jax import lax
from jax.experimental import pallas as pl
from jax.experimental.pallas import tpu as pltpu

F32 = jnp.float32
BF16 = jnp.bfloat16

N_MH = 4
DK = 256
DV = 512
CHUNK = 64
HD = 64
N_Q = 32
N_KV = 4
GROUP = N_Q // N_KV
WINDOW = 128
ROT = 16
THETA = 500000.0
PAST_LEN = 16384
EPS = 1e-6
D_MODEL = 2048
QK_W = N_MH * DK
V_W = N_MH * DV
ATT_Q = N_Q * HD
ATT_KV = N_KV * HD

_Z_QM = 0
_Z_KM = _Z_QM + QK_W
_Z_VM = _Z_KM + QK_W
_Z_OM = _Z_VM + V_W
_Z_QA = _Z_OM + V_W
_Z_GA = _Z_QA + ATT_Q
_Z_GB = _Z_GA + D_MODEL
_Z_KA = _Z_GB + D_MODEL
_Z_VA = _Z_KA + ATT_KV
_Z_IF = _Z_VA + ATT_KV
LANES = 128
_PROJ_TN = 1024
_Z_W = -(-(_Z_IF + LANES) // _PROJ_TN) * _PROJ_TN
assert _Z_QA % _PROJ_TN == 0 and _Z_GA % _PROJ_TN == 0 and _Z_KA % _PROJ_TN == 0

_VMEM_LIMIT = 56 * 1024 * 1024


def _cparams(sem):
    return pltpu.CompilerParams(dimension_semantics=sem, vmem_limit_bytes=_VMEM_LIMIT)


def _rms(x, g):
    return x * lax.rsqrt(jnp.mean(x * x, axis=-1, keepdims=True) + EPS) * g


def _sigmoid(x):
    return 1.0 / (1.0 + jnp.exp(-x))


def _log_sigmoid(x):
    return jnp.minimum(x, 0.0) - jnp.log1p(jnp.exp(-jnp.abs(x)))


def _dot(a, b):
    return jnp.dot(a, b, preferred_element_type=F32)


def _dot_nt(a, b):
    return lax.dot_general(a, b, (((1,), (1,)), ((), ())), preferred_element_type=F32)


def _dot_tn(a, b):
    return lax.dot_general(a, b, (((0,), (0,)), ((), ())), preferred_element_type=F32)


def _proj_kernel(x_ref, g_ref, wlo_ref, whi_ref, cos_ref, sa_ref, sb_ref, z_ref, h_ref):
    j = pl.program_id(1)
    jq0 = _Z_QA // _PROJ_TN
    jq1 = _Z_GA // _PROJ_TN
    jk = _Z_KA // _PROJ_TN

    @pl.when(j == 0)
    def _():
        h_ref[...] = _rms(x_ref[...], g_ref[...]).astype(BF16)

    is_lo = j < jq0
    is_q = (j >= jq0) & (j < jq1)
    is_k = j == jk

    @pl.when(is_lo)
    def _():
        z_ref[...] = _dot(h_ref[...], wlo_ref[...])

    @pl.when(is_q)
    def _():
        acc = _dot(h_ref[...], whi_ref[...])
        z_ref[...] = _rope(acc, cos_ref[...], sa_ref[...], sb_ref[...]) * (HD ** -0.5)

    @pl.when(is_k)
    def _():
        acc = _dot(h_ref[...], whi_ref[...])
        z_ref[...] = acc
        z_ref[:, :ATT_KV] = _rope(acc[:, :ATT_KV], cos_ref[...], sa_ref[...], sb_ref[...])

    @pl.when(jnp.logical_not(is_lo | is_q | is_k))
    def _():
        z_ref[...] = _dot(h_ref[...], whi_ref[...])


def _project(x2d, g, w_lo, w_hi, tables, tm):
    rows, d = x2d.shape
    tn = _PROJ_TN
    n_lo = _Z_QA // tn
    n = n_lo * tn + w_hi.shape[1]
    assert n == _Z_W and w_lo.shape[1] >= n_lo * tn
    tab_tiles = tables[0].shape[0] // tm
    tab = pl.BlockSpec((tm, LANES), lambda i, j: (i % tab_tiles, 0))
    return pl.pallas_call(
        _proj_kernel,
        grid=(rows // tm, n // tn),
        in_specs=[
            pl.BlockSpec((tm, d), lambda i, j: (i, 0)),
            pl.BlockSpec((1, d), lambda i, j: (0, 0)),
            pl.BlockSpec((d, tn), lambda i, j: (0, jnp.minimum(j, n_lo - 1))),
            pl.BlockSpec((d, tn), lambda i, j: (0, jnp.maximum(j - n_lo, 0))),
            tab, tab, tab,
        ],
        out_specs=pl.BlockSpec((tm, tn), lambda i, j: (i, j)),
        out_shape=jax.ShapeDtypeStruct((rows, n), F32),
        scratch_shapes=[pltpu.VMEM((tm, d), BF16)],
        compiler_params=_cparams(("parallel", "arbitrary")),
        name="proj",
    )(x2d, g.reshape(1, d), w_lo, w_hi, *tables)


def _mlstm_prompt_kernel(q_ref, k_ref, v_ref, o_ref, g_ref, bias_ref,
                         hm_ref, c_out, n_out, m_out, ct_s, n_s, m_s):
    c = pl.program_id(1)
    last = pl.num_programs(1) - 1
    L = CHUNK
    nseq = q_ref.shape[0]

    @pl.when(c == 0)
    def _():
        ct_s[...] = jnp.zeros_like(ct_s)
        n_s[...] = jnp.zeros_like(n_s)
        m_s[...] = jnp.zeros_like(m_s)

    row = lax.broadcasted_iota(jnp.int32, (L, LANES), 0)
    ri = lax.broadcasted_iota(jnp.int32, (L, L), 0)
    ci = lax.broadcasted_iota(jnp.int32, (L, L), 1)
    eye = ri == ci
    causal = ci <= ri

    heads = []
    for bi in range(nseq):
        gates = g_ref[bi] + bias_ref[...]
        bcum = _log_sigmoid(gates)
        for d in (1, 2, 4, 8, 16, 32):
            bcum = bcum + jnp.where(row >= d, pltpu.roll(bcum, d, axis=0), 0.0)
        for h in range(N_MH):
            q = q_ref[bi, :, h * DK:(h + 1) * DK]
            k = k_ref[bi, :, h * DK:(h + 1) * DK] * (DK ** -0.5)
            qb = q.astype(BF16)
            ct = ct_s[bi * N_MH + h]
            heads.append((bi, h, gates, bcum, q, k, _dot_nt(qb, k.astype(BF16)), _dot(qb, ct.astype(BF16))))

    for bi, h, gates, bcum, q, k, qk, cq in heads:
        v = v_ref[bi, :, h * DV:(h + 1) * DV]
        o = o_ref[bi, :, h * DV:(h + 1) * DV]
        b_col = bcum[:, N_MH + h:N_MH + h + 1]
        ig_col = gates[:, h:h + 1]
        a_col = ig_col - b_col
        a_lane = jnp.sum(jnp.where(eye, a_col, 0.0), axis=0, keepdims=True)
        dmat = jnp.where(causal, b_col + a_lane, -jnp.inf)
        m_prev = m_s[bi, h:h + 1, 0:1]
        inter = b_col + m_prev
        m_t = jnp.maximum(jnp.max(dmat, axis=1, keepdims=True), inter)
        vb = v.astype(BF16)
        s = qk * jnp.exp(dmat - m_t)
        w_inter = jnp.exp(inter - m_t)
        m_new = m_t[L - 1:L, :]
        b_last = b_col[L - 1:L, :]
        w_last = jnp.exp(b_last - b_col + ig_col - m_new)
        decay = jnp.exp(b_last + m_prev - m_new)
        kw = k * w_last
        ct = ct_s[bi * N_MH + h]
        ct_s[bi * N_MH + h] = decay * ct + _dot_tn(kw.astype(BF16), vb)
        num = _dot(s.astype(BF16), vb) + w_inter * cq
        n_row = n_s[bi, h:h + 1, :]
        den = jnp.sum(s, axis=1, keepdims=True) + w_inter * jnp.sum(q * n_row, axis=1, keepdims=True)
        hout = num / jnp.maximum(jnp.abs(den), jnp.exp(-m_t))
        hm_ref[bi, :, h * DV:(h + 1) * DV] = (_sigmoid(o) * hout).astype(BF16)
        n_s[bi, h:h + 1, :] = decay * n_row + jnp.sum(kw, axis=0, keepdims=True)
        m_s[bi, h:h + 1, :] = jnp.broadcast_to(m_new, (1, LANES))

    @pl.when(c == last)
    def _():
        for bi in range(nseq):
            for h in range(N_MH):
                c_out[bi, h] = ct_s[bi * N_MH + h].T
        n_out[...] = n_s[:, :N_MH, :]
        m_out[...] = m_s[...]


def _mlstm_prompt(z, bias_row, batch, seq, bb):
    nc = seq // CHUNK
    z3 = z.reshape(batch, seq, z.shape[1])

    def zspec(width, off):
        blk = off // width
        return pl.BlockSpec((bb, CHUNK, width), lambda b, c: (b, c, blk))

    return pl.pallas_call(
        _mlstm_prompt_kernel,
        grid=(batch // bb, nc),
        in_specs=[
            zspec(QK_W, _Z_QM), zspec(QK_W, _Z_KM), zspec(V_W, _Z_VM), zspec(V_W, _Z_OM),
            zspec(LANES, _Z_IF),
            pl.BlockSpec((1, LANES), lambda b, c: (0, 0)),
        ],
        out_specs=[
            pl.BlockSpec((bb, CHUNK, V_W), lambda b, c: (b, c, 0)),
            pl.BlockSpec((bb, N_MH, DV, DK), lambda b, c: (b, 0, 0, 0)),
            pl.BlockSpec((bb, N_MH, DK), lambda b, c: (b, 0, 0)),
            pl.BlockSpec((bb, 8, LANES), lambda b, c: (b, 0, 0)),
        ],
        out_shape=[
            jax.ShapeDtypeStruct((batch, seq, V_W), BF16),
            jax.ShapeDtypeStruct((batch, N_MH, DV, DK), F32),
            jax.ShapeDtypeStruct((batch, N_MH, DK), F32),
            jax.ShapeDtypeStruct((batch, 8, LANES), F32),
        ],
        scratch_shapes=[
            pltpu.VMEM((bb * N_MH, DK, DV), F32),
            pltpu.VMEM((bb, 8, DK), F32),
            pltpu.VMEM((bb, 8, LANES), F32),
        ],
        compiler_params=_cparams(("parallel", "arbitrary")),
        name="mlstm_prompt",
    )(z3, z3, z3, z3, z3, bias_row)


def _mlstm_sample_kernel(q_ref, k_ref, v_ref, o_ref, g_ref, bias_ref, c_ref, n_ref, m_ref,
                         hm_ref, c_out, n_out, m_out):
    ri = lax.broadcasted_iota(jnp.int32, (DV, DV), 0)
    ci = lax.broadcasted_iota(jnp.int32, (DV, DV), 1)
    eye = ri == ci
    lane = lax.broadcasted_iota(jnp.int32, (1, LANES), 1)

    for b in range(q_ref.shape[0]):
        gates = g_ref[b] + bias_ref[...]
        lf_all = _log_sigmoid(gates)
        m_all = m_ref[b]
        m_new_all = jnp.zeros((1, LANES), F32)
        for h in range(N_MH):
            q = q_ref[b, :, h * DK:(h + 1) * DK]
            k = k_ref[b, :, h * DK:(h + 1) * DK] * (DK ** -0.5)
            v = v_ref[b, :, h * DV:(h + 1) * DV]
            o = o_ref[b, :, h * DV:(h + 1) * DV]
            ig = gates[:, h:h + 1]
            lf = lf_all[:, N_MH + h:N_MH + h + 1]
            m_prev = m_all[:, h:h + 1]
            inter = lf + m_prev
            m_t = jnp.maximum(ig, inter)
            s = jnp.sum(q * k, axis=1, keepdims=True) * jnp.exp(ig - m_t)
            w_inter = jnp.exp(inter - m_t)
            cmat = c_ref[0, b, h]
            q8 = jnp.broadcast_to(q, (8, DK)).astype(BF16)
            cq = _dot_nt(q8, cmat.astype(BF16))[0:1, :]
            n_row = n_ref[0, b, h:h + 1, :]
            num = s * v + w_inter * cq
            den = s + w_inter * jnp.sum(n_row * q, axis=1, keepdims=True)
            hout = num / jnp.maximum(jnp.abs(den), jnp.exp(-m_t))
            hm_ref[b, :, h * DV:(h + 1) * DV] = (_sigmoid(o) * hout).astype(BF16)

            w_last = jnp.exp(ig - m_t)
            decay = jnp.exp(inter - m_t)
            vw = v * w_last
            vw_col = jnp.sum(jnp.where(eye, vw, 0.0), axis=1, keepdims=True)
            c_out[0, b, h] = decay * cmat + vw_col * k
            n_out[0, b, h:h + 1, :] = decay * n_row + w_last * k
            m_new_all = jnp.where(lane == h, m_t, m_new_all)
        m_out[b] = m_new_all


def _mlstm_sample(z3, bias_row, c0, n0, m0_pad, layer, bb):
    nb = z3.shape[0]

    def zspec(width, off):
        blk = off // width
        return pl.BlockSpec((bb, 1, width), lambda b: (b, 0, blk))

    c_in = pl.BlockSpec((1, bb, N_MH, DV, DK), lambda b: (layer, b, 0, 0, 0))
    n_in = pl.BlockSpec((1, bb, N_MH, DK), lambda b: (layer, b, 0, 0))
    c_blk = pl.BlockSpec((1, bb, N_MH, DV, DK), lambda b: (0, b, 0, 0, 0))
    n_blk = pl.BlockSpec((1, bb, N_MH, DK), lambda b: (0, b, 0, 0))

    return pl.pallas_call(
        _mlstm_sample_kernel,
        grid=(nb // bb,),
        in_specs=[
            zspec(QK_W, _Z_QM), zspec(QK_W, _Z_KM), zspec(V_W, _Z_VM), zspec(V_W, _Z_OM),
            zspec(LANES, _Z_IF),
            pl.BlockSpec((1, LANES), lambda b: (0, 0)),
            c_in,
            n_in,
            pl.BlockSpec((bb, 1, LANES), lambda b: (b, 0, 0)),
        ],
        out_specs=[
            pl.BlockSpec((bb, 1, V_W), lambda b: (b, 0, 0)),
            c_blk,
            n_blk,
            pl.BlockSpec((bb, 1, LANES), lambda b: (b, 0, 0)),
        ],
        out_shape=[
            jax.ShapeDtypeStruct((nb, 1, V_W), BF16),
            jax.ShapeDtypeStruct((1, nb, N_MH, DV, DK), F32),
            jax.ShapeDtypeStruct((1, nb, N_MH, DK), F32),
            jax.ShapeDtypeStruct((nb, 1, LANES), F32),
        ],
        compiler_params=_cparams(("parallel",)),
        name="mlstm_sample",
    )(z3, z3, z3, z3, z3, bias_row, c0, n0, m0_pad)


def _rope_tables(pos):
    half = ROT // 2
    inv = THETA ** (-jnp.arange(half, dtype=F32) * 2.0 / ROT)
    ang = pos.astype(F32)[:, None] * inv[None, :]
    cos = jnp.cos(ang)
    sin = jnp.sin(ang)
    t = pos.shape[0]
    cos_t = jnp.concatenate([cos, cos, jnp.ones((t, HD - ROT), F32)], axis=1)
    sin_a = jnp.concatenate([-sin, jnp.zeros((t, HD - half), F32)], axis=1)
    sin_b = jnp.concatenate([jnp.zeros((t, half), F32), sin, jnp.zeros((t, HD - ROT), F32)], axis=1)
    rep = LANES // HD
    return tuple(jnp.tile(a, (1, rep)) for a in (cos_t, sin_a, sin_b))


def _rope(x, cos_t, sin_a, sin_b):
    w = x.shape[1]
    rep = w // LANES
    half = ROT // 2
    if rep > 1:
        cos_t = jnp.concatenate([cos_t] * rep, axis=1)
        sin_a = jnp.concatenate([sin_a] * rep, axis=1)
        sin_b = jnp.concatenate([sin_b] * rep, axis=1)
    up = pltpu.roll(x, w - half, axis=1)
    down = pltpu.roll(x, half, axis=1)
    return x * cos_t + up * sin_a + down * sin_b


def _swa_prompt_kernel(sink_ref, q_ref, kc_ref, kp_ref, vc_ref, vp_ref, ha_ref):
    i = pl.program_id(1)
    w = WINDOW
    kk = jnp.concatenate([kp_ref[...], kc_ref[...]], axis=0)
    vv = jnp.concatenate([vp_ref[...], vc_ref[...]], axis=0)
    vv_t = vv.T.astype(BF16)
    s_idx = lax.broadcasted_iota(jnp.int32, (2 * w, w), 0)
    a_idx = lax.broadcasted_iota(jnp.int32, (2 * w, w), 1)
    valid = (s_idx >= a_idx) & (s_idx <= a_idx + w) & ((i > 0) | (s_idx >= w))
    lane = lax.broadcasted_iota(jnp.int32, (2 * w, LANES), 1)
    pair_w = 2 * HD
    for half in range(N_KV // 2):
        kh = kk[:, half * pair_w:(half + 1) * pair_w]
        kh_sw = pltpu.roll(kh, HD, axis=1)
        for gg in range(2):
            g = 2 * half + gg
            k_lo = jnp.where(lane < HD, kh if gg == 0 else kh_sw, 0.0).astype(BF16)
            k_hi = jnp.where(lane >= HD, kh_sw if gg == 0 else kh, 0.0).astype(BF16)
            vg_t = vv_t[g * HD:(g + 1) * HD, :]
            sts = []
            for a in range(GROUP // 2):
                col = (g * GROUP + 2 * a) * HD
                qpair = q_ref[:, col:col + pair_w].astype(BF16)
                sts += [_dot_nt(k_lo, qpair), _dot_nt(k_hi, qpair)]
            es, rs = [], []
            for hh in range(GROUP):
                sink = sink_ref[g * GROUP + hh]
                st = jnp.where(valid, sts[hh], -jnp.inf)
                m = jnp.maximum(jnp.max(st, axis=0, keepdims=True), sink)
                e = jnp.exp(st - m)
                es.append(e.astype(BF16))
                rs.append(1.0 / (jnp.sum(e, axis=0, keepdims=True) + jnp.exp(sink - m)))
            ots = [_dot(vg_t, es[hh]) * rs[hh] for hh in range(GROUP)]
            for a in range(GROUP // 2):
                col = (g * GROUP + 2 * a) * HD
                ha_ref[:, col:col + pair_w] = jnp.concatenate(ots[2 * a:2 * a + 2], axis=0).T.astype(BF16)


def _swa_prompt(z, sinks3, batch, seq):
    rows = z.shape[0]
    nb = seq // WINDOW
    w = WINDOW

    def cur(width, off):
        blk = off // width
        return pl.BlockSpec((w, width), lambda b, i: (b * nb + i, blk))

    def prev(width, off):
        blk = off // width
        return pl.BlockSpec((w, width), lambda b, i: (b * nb + jnp.maximum(i - 1, 0), blk))

    return pl.pallas_call(
        _swa_prompt_kernel,
        grid=(batch, nb),
        in_specs=[
            pl.BlockSpec(memory_space=pltpu.SMEM),
            cur(ATT_Q, _Z_QA), cur(ATT_KV, _Z_KA), prev(ATT_KV, _Z_KA),
            cur(ATT_KV, _Z_VA), prev(ATT_KV, _Z_VA),
        ],
        out_specs=pl.BlockSpec((w, ATT_Q), lambda b, i: (b * nb + i, 0)),
        out_shape=jax.ShapeDtypeStruct((rows, ATT_Q), BF16),
        compiler_params=_cparams(("parallel", "parallel")),
        name="swa_prompt",
    )(sinks3, z, z, z, z, z)


def _swa_sample_kernel(q_ref, kn_ref, vn_ref, kp_ref, vp_ref, sink_ref,
                       o_ref, ko_ref, vo_ref, *, bb):
    w = WINDOW
    hrow = lax.broadcasted_iota(jnp.int32, (N_Q, ATT_KV), 0) // GROUP
    glane = lax.broadcasted_iota(jnp.int32, (N_Q, ATT_KV), 1) // HD
    own = hrow == glane
    srow = lax.broadcasted_iota(jnp.int32, (w, ATT_KV), 0)
    sink = sink_ref[...]
    for b in range(bb):
        q = q_ref[b]
        qm = jnp.where(own, jnp.concatenate([q] * N_KV, axis=1), 0.0)
        k_past = kp_ref[b]
        v_past = vp_ref[b]
        k_new = kn_ref[b]
        v_new = vn_ref[b]
        s_past = _dot_nt(qm.astype(BF16), k_past.astype(BF16))
        s_new = jnp.sum(qm * k_new, axis=1, keepdims=True)
        m = jnp.maximum(jnp.maximum(jnp.max(s_past, axis=1, keepdims=True), s_new), sink)
        e_past = jnp.exp(s_past - m)
        e_new = jnp.exp(s_new - m)
        denom = jnp.sum(e_past, axis=1, keepdims=True) + e_new + jnp.exp(sink - m)
        r = 1.0 / denom
        pv = _dot((e_past * r).astype(BF16), v_past.astype(BF16)) + (e_new * r) * v_new
        pv = jnp.where(own, pv, 0.0)
        o = pv[:, 0:HD]
        for g in range(1, N_KV):
            o = o + pv[:, g * HD:(g + 1) * HD]
        o_ref[b] = o
        ko_ref[b] = jnp.where(srow == w - 1, k_new, pltpu.roll(k_past, w - 1, axis=0))
        vo_ref[b] = jnp.where(srow == w - 1, v_new, pltpu.roll(v_past, w - 1, axis=0))


def _swa_sample(q3, k_new3, v_new3, k_past, v_past, sink_col, bb):
    nb = q3.shape[0]
    w = WINDOW
    return pl.pallas_call(
        functools.partial(_swa_sample_kernel, bb=bb),
        grid=(nb // bb,),
        in_specs=[
            pl.BlockSpec((bb, N_Q, HD), lambda i: (i, 0, 0)),
            pl.BlockSpec((bb, 1, ATT_KV), lambda i: (i, 0, _Z_KA // ATT_KV)),
            pl.BlockSpec((bb, 1, ATT_KV), lambda i: (i, 0, _Z_VA // ATT_KV)),
            pl.BlockSpec((bb, w, ATT_KV), lambda i: (i, 0, 0)),
            pl.BlockSpec((bb, w, ATT_KV), lambda i: (i, 0, 0)),
            pl.BlockSpec((N_Q, 1), lambda i: (0, 0)),
        ],
        out_specs=[
            pl.BlockSpec((bb, N_Q, HD), lambda i: (i, 0, 0)),
            pl.BlockSpec((bb, w, ATT_KV), lambda i: (i, 0, 0)),
            pl.BlockSpec((bb, w, ATT_KV), lambda i: (i, 0, 0)),
        ],
        out_shape=[
            jax.ShapeDtypeStruct((nb, N_Q, HD), F32),
            jax.ShapeDtypeStruct((nb, w, ATT_KV), F32),
            jax.ShapeDtypeStruct((nb, w, ATT_KV), F32),
        ],
        compiler_params=_cparams(("parallel",)),
        name="swa_sample",
    )(q3, k_new3, v_new3, k_past, v_past, sink_col)


def _merge1_kernel(hm_ref, ha_ref, ga_ref, gb_ref, wa_ref, wb_ref, t_ref):
    a = _dot(hm_ref[...], wa_ref[...])
    b = _dot(ha_ref[...], wb_ref[...])
    t_ref[...] = (_sigmoid(ga_ref[...]) * a + _sigmoid(gb_ref[...]) * b).astype(BF16)


def _merge1(hm, ha, z, wa, wb, tm, tn):
    rows, d = hm.shape
    n = wa.shape[1]
    return pl.pallas_call(
        _merge1_kernel,
        grid=(rows // tm, n // tn),
        in_specs=[
            pl.BlockSpec((tm, d), lambda i, j: (i, 0)),
            pl.BlockSpec((tm, d), lambda i, j: (i, 0)),
            pl.BlockSpec((tm, tn), lambda i, j: (i, _Z_GA // tn + j)),
            pl.BlockSpec((tm, tn), lambda i, j: (i, _Z_GB // tn + j)),
            pl.BlockSpec((d, tn), lambda i, j: (0, j)),
            pl.BlockSpec((d, tn), lambda i, j: (0, j)),
        ],
        out_specs=pl.BlockSpec((tm, tn), lambda i, j: (i, j)),
        out_shape=jax.ShapeDtypeStruct((rows, n), BF16),
        compiler_params=_cparams(("parallel", "arbitrary")),
        name="merge1",
    )(hm, ha, z, z, wa, wb)


def _merge2_kernel(t_ref, w_ref, x_ref, g1_ref, g2_ref, x1_ref, h2_ref):
    mix = _dot(t_ref[...], w_ref[...])
    x1 = x_ref[...] + _rms(mix, g1_ref[...])
    x1_ref[...] = x1
    h2_ref[...] = _rms(x1, g2_ref[...]).astype(BF16)


def _merge2(t, w_out, x2d, g1, g2, tm):
    rows, d = x2d.shape
    row = lambda i: (i, 0)
    const = lambda i: (0, 0)
    return pl.pallas_call(
        _merge2_kernel,
        grid=(rows // tm,),
        in_specs=[
            pl.BlockSpec((tm, d), row),
            pl.BlockSpec((d, d), const, pipeline_mode=pl.Buffered(1)),
            pl.BlockSpec((tm, d), row),
            pl.BlockSpec((1, d), const),
            pl.BlockSpec((1, d), const),
        ],
        out_specs=[pl.BlockSpec((tm, d), row), pl.BlockSpec((tm, d), row)],
        out_shape=[jax.ShapeDtypeStruct((rows, d), F32), jax.ShapeDtypeStruct((rows, d), BF16)],
        compiler_params=_cparams(("parallel",)),
        name="merge2",
    )(t, w_out, x2d, g1.reshape(1, d), g2.reshape(1, d))


def _gelu_tanh(x):
    return 0.5 * x * (1.0 + jnp.tanh(0.7978845608028654 * (x + 0.044715 * (x * x * x))))


def _conv_taps(u, u1, u2, cw_ref, cb_ref, cs=slice(None)):
    return cb_ref[:, cs] + u2 * cw_ref[0:1, cs] + u1 * cw_ref[1:2, cs] + u * cw_ref[2:3, cs]


def _ffn_tail(j, nf, y, wd_ref, x1_ref, g_ref, y_ref, acc_ref):
    @pl.when(j == 0)
    def _():
        acc_ref[...] = _dot(y, wd_ref[...])

    @pl.when(j > 0)
    def _():
        acc_ref[...] += _dot(y, wd_ref[...])

    @pl.when(j == nf - 1)
    def _():
        y_ref[...] = x1_ref[...] + _rms(acc_ref[...], g_ref[...])


def _ffn_prompt_kernel(h_ref, wg_ref, wv_ref, cwg_ref, cwv_ref, cbg_ref, cbv_ref, wd_ref,
                       x1_hbm, g_ref, y_ref, sg_ref, sv_ref, acc_ref, carry_ref, x1_buf, x1_sem,
                       *, tiles_per_seq):
    i = pl.program_id(0)
    j = pl.program_id(1)
    nf = pl.num_programs(1)
    tm = h_ref.shape[0]

    def x1_copy():
        r0 = pl.multiple_of(i * tm, tm)
        return pltpu.make_async_copy(x1_hbm.at[pl.ds(r0, tm), :], x1_buf, x1_sem.at[0])

    @pl.when(j == nf - 2)
    def _():
        x1_copy().start()

    @pl.when(j == nf - 1)
    def _():
        x1_copy().wait()

    @pl.when((i == 0) & (j == 0))
    def _():
        carry_ref[...] = jnp.zeros_like(carry_ref)

    seq_start = (i % tiles_per_seq) == 0
    h = h_ref[...]

    def branch(w_ref, cw_ref, cb_ref, slot, state_ref):
        u = _dot(h, w_ref[...])
        prev = jnp.where(seq_start, 0.0, carry_ref[slot])
        uu = jnp.concatenate([prev, u], axis=0)
        u1 = uu[7:7 + tm, :]
        u2 = uu[6:6 + tm, :]
        tail = u[tm - 8:tm, :]
        carry_ref[slot] = tail
        state_ref[0] = tail
        return _conv_taps(u, u1, u2, cw_ref, cb_ref)

    cg = branch(wg_ref, cwg_ref, cbg_ref, 2 * j, sg_ref)
    cv = branch(wv_ref, cwv_ref, cbv_ref, 2 * j + 1, sv_ref)
    y = (_gelu_tanh(cg) * cv).astype(BF16)
    _ffn_tail(j, nf, y, wd_ref, x1_buf, g_ref, y_ref, acc_ref)


def _ffn_sample_kernel(h_ref, wg_ref, wv_ref, cwg_ref, cwv_ref, cbg_ref, cbv_ref, wd_ref,
                       x1_ref, g_ref, p2g_ref, p2v_ref, p1g_ref, p1v_ref,
                       y_ref, ug_ref, uv_ref, acc_ref):
    j = pl.program_id(1)
    nf = pl.num_programs(1)
    h = h_ref[...]
    ug = _dot(h, wg_ref[...])
    uv = _dot(h, wv_ref[...])
    ug_ref[...] = ug
    uv_ref[...] = uv
    cg = _conv_taps(ug, p1g_ref[...], p2g_ref[...], cwg_ref, cbg_ref)
    cv = _conv_taps(uv, p1v_ref[...], p2v_ref[...], cwv_ref, cbv_ref)
    y = (_gelu_tanh(cg) * cv).astype(BF16)
    _ffn_tail(j, nf, y, wd_ref, x1_ref, g_ref, y_ref, acc_ref)


def _ffn(h2, x1, w_up, w_conv, b_conv2d, w_down, g, tm, tf, seq=None, past=None):
    rows, d = h2.shape
    f = w_down.shape[0]
    nf = f // tf
    row = lambda i, j: (i, 0)
    gcol = lambda i, j: (0, j)
    vcol = lambda i, j: (0, nf + j)
    in_specs = [
        pl.BlockSpec((tm, d), row),
        pl.BlockSpec((d, tf), gcol), pl.BlockSpec((d, tf), vcol),
        pl.BlockSpec((3, tf), gcol), pl.BlockSpec((3, tf), vcol),
        pl.BlockSpec((1, tf), gcol), pl.BlockSpec((1, tf), vcol),
        pl.BlockSpec((tf, d), lambda i, j: (j, 0)),
        pl.BlockSpec((tm, d), row),
        pl.BlockSpec((1, d), lambda i, j: (0, 0)),
    ]
    args = [h2, w_up, w_up, w_conv, w_conv, b_conv2d, b_conv2d, w_down, x1, g.reshape(1, d)]
    scratch = [pltpu.VMEM((tm, d), F32)]
    if past is None:
        tiles_per_seq = seq // tm
        kern = functools.partial(_ffn_prompt_kernel, tiles_per_seq=tiles_per_seq)
        state_spec = pl.BlockSpec((1, 8, tf), lambda i, j: (i, 0, j))
        out_specs = [pl.BlockSpec((tm, d), row), state_spec, state_spec]
        out_shape = [jax.ShapeDtypeStruct((rows, d), F32),
                     jax.ShapeDtypeStruct((rows // tm, 8, f), F32),
                     jax.ShapeDtypeStruct((rows // tm, 8, f), F32)]
        scratch.append(pltpu.VMEM((2 * nf, 8, tf), F32))
        assert nf >= 2
        in_specs[8] = pl.BlockSpec(memory_space=pl.ANY)
        scratch += [pltpu.VMEM((tm, d), F32), pltpu.SemaphoreType.DMA((1,))]
        name = "ffn_prompt"
    else:
        p2, p1 = past
        kern = _ffn_sample_kernel
        ucol = pl.BlockSpec((tm, tf), lambda i, j: (i, j))
        in_specs += [
            pl.BlockSpec((tm, tf), lambda i, j: (i, j)), pl.BlockSpec((tm, tf), lambda i, j: (i, nf + j)),
            pl.BlockSpec((tm, tf), lambda i, j: (i, j)), pl.BlockSpec((tm, tf), lambda i, j: (i, nf + j)),
        ]
        args += [p2, p2, p1, p1]
        out_specs = [pl.BlockSpec((tm, d), row), ucol, ucol]
        out_shape = [jax.ShapeDtypeStruct((rows, d), F32),
                     jax.ShapeDtypeStruct((rows, f), F32),
                     jax.ShapeDtypeStruct((rows, f), F32)]
        name = "ffn_sample"
    return pl.pallas_call(
        kern,
        grid=(rows // tm, nf),
        in_specs=in_specs,
        out_specs=out_specs,
        out_shape=out_shape,
        scratch_shapes=scratch,
        compiler_params=_cparams(("arbitrary", "arbitrary")),
        name=name,
    )(*args)


def _regroup_w_in(w_in):
    d = w_in.shape[0]
    o_if = QK_W * 2 + V_W * 2
    o_qa = o_if + 2 * N_MH
    o_ka = o_qa + ATT_Q
    o_ga = o_ka + 2 * ATT_KV
    assert o_if == _Z_QA
    parts = [
        w_in[:, o_qa:o_ka],
        w_in[:, o_ga:],
        w_in[:, o_ka:o_ga],
        w_in[:, o_if:o_qa],
    ]
    used = o_if + sum(p.shape[1] for p in parts)
    parts.append(jnp.zeros((d, _Z_W - used), w_in.dtype))
    return jnp.concatenate(parts, axis=1)


def _pick(rows, pref):
    t = min(rows, pref)
    while rows % t:
        t //= 2
    return t


def kernel(x_prompt, x_sample, state_mlstm_c, state_mlstm_n, state_mlstm_m, cache_swa_k, cache_swa_v,
           state_ffn_conv, g_pre_mix, w_in, b_if, attn_sinks, w_branch_a, w_branch_b, w_out, g_post_mix,
           g_pre_ffn, w_up, w_conv, b_conv, w_down, g_post_ffn):
    bp, tp, d = x_prompt.shape
    bs, ts, _ = x_sample.shape
    depth = w_in.shape[0]
    assert depth == 1 and ts == 1 and d == D_MODEL
    f = w_down.shape[1]
    tf = 512
    assert f % tf == 0

    l = 0
    w_lo = w_in[l].astype(BF16)
    w_hi = _regroup_w_in(w_lo)
    wa = w_branch_a[l].astype(BF16)
    wb = w_branch_b[l].astype(BF16)
    wo = w_out[l].astype(BF16)
    wu = w_up[l].astype(BF16)
    wd = w_down[l].astype(BF16)
    bias_row = jnp.concatenate([b_if[l].astype(F32), jnp.zeros((LANES - 2 * N_MH,), F32)]).reshape(1, LANES)
    sinks = attn_sinks[l].astype(F32)
    b_conv2d = b_conv[l].reshape(1, 2 * f)

    xp = x_prompt.reshape(bp * tp, d)
    zp = _project(xp, g_pre_mix[l], w_lo, w_hi, _rope_tables(jnp.arange(tp)), _pick(tp, 1024))
    hm_p3, c_p, n_p, m_p = _mlstm_prompt(zp, bias_row, bp, tp, _pick(bp, 4))
    hm_p = hm_p3.reshape(bp * tp, V_W)
    ha_p = _swa_prompt(zp, sinks, bp, tp)
    t_p = _merge1(hm_p, ha_p, zp, wa, wb, _pick(bp * tp, 1024), 512)
    x1_p, h2_p = _merge2(t_p, wo, xp, g_post_mix[l], g_pre_ffn[l], _pick(bp * tp, 512))
    tm_ffn = _pick(tp, 512)
    y_p, sg_p, sv_p = _ffn(h2_p, x1_p, wu, w_conv[l], b_conv2d, wd, g_post_ffn[l],
                           tm_ffn, tf, seq=tp)
    seq_end = slice(tp // tm_ffn - 1, None, tp // tm_ffn)

    y_prompt = y_p.reshape(bp, tp, d)
    c_prompt = c_p[None]
    n_prompt = n_p[None]
    m_prompt = m_p[:, :N_MH, 0][None]
    kv_tail = zp.reshape(bp, tp, _Z_W)[:, tp - WINDOW:, _Z_KA:_Z_KA + 2 * ATT_KV]
    k_prompt = kv_tail[..., :ATT_KV].reshape(bp, WINDOW, N_KV, HD)[None]
    v_prompt = kv_tail[..., ATT_KV:].reshape(bp, WINDOW, N_KV, HD)[None]
    conv_prompt = jnp.concatenate([sg_p[seq_end, 6:8, :], sv_p[seq_end, 6:8, :]], axis=-1)[None]

    xs = x_sample.reshape(bs, d)
    tabs_s = tuple(jnp.broadcast_to(a, (bs, LANES)) for a in _rope_tables(PAST_LEN + jnp.arange(ts)))
    zs = _project(xs, g_pre_mix[l], w_lo, w_hi, tabs_s, bs)
    zs3 = zs.reshape(bs, 1, _Z_W)
    m0_pad = jnp.pad(state_mlstm_m[l].astype(F32), ((0, 0), (0, LANES - N_MH))).reshape(bs, 1, LANES)
    hm_s3, c_sample, n_sample, m_s3 = _mlstm_sample(zs3, bias_row, state_mlstm_c, state_mlstm_n, m0_pad, l,
                                                          _pick(bs, 4))
    o_s3, k_s, v_s = _swa_sample(
        zs[:, _Z_QA:_Z_QA + ATT_Q].reshape(bs, N_Q, HD), zs3, zs3,
        cache_swa_k[l].reshape(bs, WINDOW, ATT_KV), cache_swa_v[l].reshape(bs, WINDOW, ATT_KV),
        sinks.reshape(N_Q, 1), _pick(bs, 8))
    ha_s = o_s3.reshape(bs, ATT_Q).astype(BF16)
    t_s = _merge1(hm_s3.reshape(bs, V_W), ha_s, zs, wa, wb, bs, 512)
    x1_s, h2_s = _merge2(t_s, wo, xs, g_post_mix[l], g_pre_ffn[l], bs)
    past = state_ffn_conv[l].astype(F32)
    y_s, ug_s, uv_s = _ffn(h2_s, x1_s, wu, w_conv[l], b_conv2d, wd, g_post_ffn[l],
                           bs, tf, past=(past[:, 0, :], past[:, 1, :]))

    y_sample = y_s.reshape(bs, ts, d)
    m_sample = m_s3[:, 0, :N_MH][None]
    k_sample = k_s.reshape(bs, WINDOW, N_KV, HD)[None]
    v_sample = v_s.reshape(bs, WINDOW, N_KV, HD)[None]
    conv_sample = jnp.stack([past[:, 1, :], jnp.concatenate([ug_s, uv_s], axis=-1)], axis=1)[None]

    return (y_prompt, y_sample,
            c_prompt, n_prompt, m_prompt, k_prompt, v_prompt, conv_prompt,
            c_sample, n_sample, m_sample, k_sample, v_sample, conv_sample)
```

```python
import functools

import jax
import jax.numpy as jnp
from jax import lax
from jax.experimental import pallas as pl
from jax.experimental.pallas import tpu as pltpu

F32 = jnp.float32
BF16 = jnp.bfloat16

N_MH = 4
DK = 256
DV = 512
CHUNK = 64
HD = 64
N_Q = 32
N_KV = 4
GROUP = N_Q // N_KV
WINDOW = 128
ROT = 16
THETA = 500000.0
PAST_LEN = 16384
EPS = 1e-6
D_MODEL = 2048
QK_W = N_MH * DK
V_W = N_MH * DV
ATT_Q = N_Q * HD
ATT_KV = N_KV * HD

_Z_QM = 0
_Z_KM = _Z_QM + QK_W
_Z_VM = _Z_KM + QK_W
_Z_OM = _Z_VM + V_W
_Z_QA = _Z_OM + V_W
_Z_GA = _Z_QA + ATT_Q
_Z_GB = _Z_GA + D_MODEL
_Z_KA = _Z_GB + D_MODEL
_Z_VA = _Z_KA + ATT_KV
_Z_IF = _Z_VA + ATT_KV
LANES = 128
_PROJ_TN = 1024
_Z_W = -(-(_Z_IF + LANES) // _PROJ_TN) * _PROJ_TN
assert _Z_QA % _PROJ_TN == 0 and _Z_GA % _PROJ_TN == 0 and _Z_KA % _PROJ_TN == 0

_VMEM_LIMIT = 56 * 1024 * 1024


def _cparams(sem):
    return pltpu.CompilerParams(dimension_semantics=sem, vmem_limit_bytes=_VMEM_LIMIT)


def _rms(x, g):
    return x * lax.rsqrt(jnp.mean(x * x, axis=-1, keepdims=True) + EPS) * g


def _sigmoid(x):
    return 1.0 / (1.0 + jnp.exp(-x))


def _log_sigmoid(x):
    return jnp.minimum(x, 0.0) - jnp.log1p(jnp.exp(-jnp.abs(x)))


def _dot(a, b):
    return jnp.dot(a, b, preferred_element_type=F32)


def _dot_nt(a, b):
    return lax.dot_general(a, b, (((1,), (1,)), ((), ())), preferred_element_type=F32)


def _dot_tn(a, b):
    return lax.dot_general(a, b, (((0,), (0,)), ((), ())), preferred_element_type=F32)


def _proj_kernel(x_hbm, g_ref, wlo_ref, whi_ref, cos_ref, sa_ref, sb_ref, z_ref, h_ref, xbuf, xsem):
    i = pl.program_id(0)
    j = pl.program_id(1)
    jq0 = _Z_QA // _PROJ_TN
    jq1 = _Z_GA // _PROJ_TN
    jk = _Z_KA // _PROJ_TN
    tm = h_ref.shape[0]

    def x_copy(tile, slot):
        r0 = pl.multiple_of(tile * tm, tm)
        return pltpu.make_async_copy(x_hbm.at[pl.ds(r0, tm), :], xbuf.at[slot], xsem.at[slot])

    @pl.when((i == 0) & (j == 0))
    def _():
        x_copy(0, 0).start()

    @pl.when((j == 1) & (i + 1 < pl.num_programs(0)))
    def _():
        x_copy(i + 1, (i + 1) % 2).start()

    @pl.when(j == 0)
    def _():
        x_copy(i, i % 2).wait()
        h_ref[...] = _rms(xbuf[i % 2], g_ref[...]).astype(BF16)

    is_lo = j < jq0
    is_q = (j >= jq0) & (j < jq1)
    is_k = j == jk

    @pl.when(is_lo)
    def _():
        z_ref[...] = _dot(h_ref[...], wlo_ref[...])

    @pl.when(is_q)
    def _():
        acc = _dot(h_ref[...], whi_ref[...])
        z_ref[...] = _rope(acc, cos_ref[...], sa_ref[...], sb_ref[...]) * (HD ** -0.5)

    @pl.when(is_k)
    def _():
        acc = _dot(h_ref[...], whi_ref[...])
        z_ref[...] = acc
        z_ref[:, :ATT_KV] = _rope(acc[:, :ATT_KV], cos_ref[...], sa_ref[...], sb_ref[...])

    @pl.when(jnp.logical_not(is_lo | is_q | is_k))
    def _():
        z_ref[...] = _dot(h_ref[...], whi_ref[...])


def _project(x2d, g, w_lo, w_hi, tables, tm):
    rows, d = x2d.shape
    tn = _PROJ_TN
    n_lo = _Z_QA // tn
    n = n_lo * tn + w_hi.shape[1]
    assert n == _Z_W and w_lo.shape[1] >= n_lo * tn
    tab_tiles = tables[0].shape[0] // tm
    tab = pl.BlockSpec((tm, LANES), lambda i, j: (i % tab_tiles, 0))
    return pl.pallas_call(
        _proj_kernel,
        grid=(rows // tm, n // tn),
        in_specs=[
            pl.BlockSpec(memory_space=pl.ANY),
            pl.BlockSpec((1, d), lambda i, j: (0, 0)),
            pl.BlockSpec((d, tn), lambda i, j: (0, jnp.minimum(j, n_lo - 1))),
            pl.BlockSpec((d, tn), lambda i, j: (0, jnp.maximum(j - n_lo, 0))),
            tab, tab, tab,
        ],
        out_specs=pl.BlockSpec((tm, tn), lambda i, j: (i, j)),
        out_shape=jax.ShapeDtypeStruct((rows, n), F32),
        scratch_shapes=[pltpu.VMEM((tm, d), BF16), pltpu.VMEM((2, tm, d), F32),
                        pltpu.SemaphoreType.DMA((2,))],
        compiler_params=_cparams(("arbitrary", "arbitrary")),
        name="proj",
    )(x2d, g.reshape(1, d), w_lo, w_hi, *tables)


def _mlstm_prompt_kernel(q_ref, k_ref, v_ref, o_ref, g_ref, bias_ref,
                         hm_ref, c_out, n_out, m_out, ct_s, n_s, m_s):
    c = pl.program_id(1)
    last = pl.num_programs(1) - 1
    L = CHUNK
    nseq = q_ref.shape[0]

    @pl.when(c == 0)
    def _():
        ct_s[...] = jnp.zeros_like(ct_s)
        n_s[...] = jnp.zeros_like(n_s)
        m_s[...] = jnp.zeros_like(m_s)

    row = lax.broadcasted_iota(jnp.int32, (L, LANES), 0)
    ri = lax.broadcasted_iota(jnp.int32, (L, L), 0)
    ci = lax.broadcasted_iota(jnp.int32, (L, L), 1)
    eye = ri == ci
    causal = ci <= ri

    heads = []
    for bi in range(nseq):
        gates = g_ref[bi] + bias_ref[...]
        bcum = _log_sigmoid(gates)
        for d in (1, 2, 4, 8, 16, 32):
            bcum = bcum + jnp.where(row >= d, pltpu.roll(bcum, d, axis=0), 0.0)
        for h in range(N_MH):
            q = q_ref[bi, :, h * DK:(h + 1) * DK]
            k = k_ref[bi, :, h * DK:(h + 1) * DK] * (DK ** -0.5)
            qb = q.astype(BF16)
            ct = ct_s[bi * N_MH + h]
            heads.append((bi, h, gates, bcum, q, k, _dot_nt(qb, k.astype(BF16)), _dot(qb, ct.astype(BF16))))

    for bi, h, gates, bcum, q, k, qk, cq in heads:
        v = v_ref[bi, :, h * DV:(h + 1) * DV]
        o = o_ref[bi, :, h * DV:(h + 1) * DV]
        b_col = bcum[:, N_MH + h:N_MH + h + 1]
        ig_col = gates[:, h:h + 1]
        a_col = ig_col - b_col
        a_lane = jnp.sum(jnp.where(eye, a_col, 0.0), axis=0, keepdims=True)
        dmat = jnp.where(causal, b_col + a_lane, -jnp.inf)
        m_prev = m_s[bi, h:h + 1, 0:1]
        inter = b_col + m_prev
        m_t = jnp.maximum(jnp.max(dmat, axis=1, keepdims=True), inter)
        vb = v.astype(BF16)
        s = qk * jnp.exp(dmat - m_t)
        w_inter = jnp.exp(inter - m_t)
        m_new = m_t[L - 1:L, :]
        b_last = b_col[L - 1:L, :]
        w_last = jnp.exp(b_last - b_col + ig_col - m_new)
        decay = jnp.exp(b_last + m_prev - m_new)
        kw = k * w_last
        ct = ct_s[bi * N_MH + h]
        ct_s[bi * N_MH + h] = decay * ct + _dot_tn(kw.astype(BF16), vb)
        num = _dot(s.astype(BF16), vb) + w_inter * cq
        n_row = n_s[bi, h:h + 1, :]
        den = jnp.sum(s, axis=1, keepdims=True) + w_inter * jnp.sum(q * n_row, axis=1, keepdims=True)
        hout = num / jnp.maximum(jnp.abs(den), jnp.exp(-m_t))
        hm_ref[bi, :, h * DV:(h + 1) * DV] = (_sigmoid(o) * hout).astype(BF16)
        n_s[bi, h:h + 1, :] = decay * n_row + jnp.sum(kw, axis=0, keepdims=True)
        m_s[bi, h:h + 1, :] = jnp.broadcast_to(m_new, (1, LANES))

    @pl.when(c == last)
    def _():
        for bi in range(nseq):
            for h in range(N_MH):
                c_out[bi, h] = ct_s[bi * N_MH + h].T
        n_out[...] = n_s[:, :N_MH, :]
        m_out[...] = m_s[...]


def _mlstm_prompt(z, bias_row, batch, seq, bb):
    nc = seq // CHUNK
    z3 = z.reshape(batch, seq, z.shape[1])

    def zspec(width, off):
        blk = off // width
        return pl.BlockSpec((bb, CHUNK, width), lambda b, c: (b, c, blk))

    return pl.pallas_call(
        _mlstm_prompt_kernel,
        grid=(batch // bb, nc),
        in_specs=[
            zspec(QK_W, _Z_QM), zspec(QK_W, _Z_KM), zspec(V_W, _Z_VM), zspec(V_W, _Z_OM),
            zspec(LANES, _Z_IF),
            pl.BlockSpec((1, LANES), lambda b, c: (0, 0)),
        ],
        out_specs=[
            pl.BlockSpec((bb, CHUNK, V_W), lambda b, c: (b, c, 0)),
            pl.BlockSpec((bb, N_MH, DV, DK), lambda b, c: (b, 0, 0, 0)),
            pl.BlockSpec((bb, N_MH, DK), lambda b, c: (b, 0, 0)),
            pl.BlockSpec((bb, 8, LANES), lambda b, c: (b, 0, 0)),
        ],
        out_shape=[
            jax.ShapeDtypeStruct((batch, seq, V_W), BF16),
            jax.ShapeDtypeStruct((batch, N_MH, DV, DK), F32),
            jax.ShapeDtypeStruct((batch, N_MH, DK), F32),
            jax.ShapeDtypeStruct((batch, 8, LANES), F32),
        ],
        scratch_shapes=[
            pltpu.VMEM((bb * N_MH, DK, DV), F32),
            pltpu.VMEM((bb, 8, DK), F32),
            pltpu.VMEM((bb, 8, LANES), F32),
        ],
        compiler_params=_cparams(("parallel", "arbitrary")),
        name="mlstm_prompt",
    )(z3, z3, z3, z3, z3, bias_row)


def _mlstm_sample_kernel(q_ref, k_ref, v_ref, o_ref, g_ref, bias_ref, c_ref, n_ref, m_ref,
                         hm_ref, c_out, n_out, m_out):
    ri = lax.broadcasted_iota(jnp.int32, (DV, DV), 0)
    ci = lax.broadcasted_iota(jnp.int32, (DV, DV), 1)
    eye = ri == ci
    lane = lax.broadcasted_iota(jnp.int32, (1, LANES), 1)

    for b in range(q_ref.shape[0]):
        gates = g_ref[b] + bias_ref[...]
        lf_all = _log_sigmoid(gates)
        m_all = m_ref[b]
        m_new_all = jnp.zeros((1, LANES), F32)
        for h in range(N_MH):
            q = q_ref[b, :, h * DK:(h + 1) * DK]
            k = k_ref[b, :, h * DK:(h + 1) * DK] * (DK ** -0.5)
            v = v_ref[b, :, h * DV:(h + 1) * DV]
            o = o_ref[b, :, h * DV:(h + 1) * DV]
            ig = gates[:, h:h + 1]
            lf = lf_all[:, N_MH + h:N_MH + h + 1]
            m_prev = m_all[:, h:h + 1]
            inter = lf + m_prev
            m_t = jnp.maximum(ig, inter)
            s = jnp.sum(q * k, axis=1, keepdims=True) * jnp.exp(ig - m_t)
            w_inter = jnp.exp(inter - m_t)
            cmat = c_ref[0, b, h]
            q8 = jnp.broadcast_to(q, (8, DK)).astype(BF16)
            cq = _dot_nt(q8, cmat.astype(BF16))[0:1, :]
            n_row = n_ref[0, b, h:h + 1, :]
            num = s * v + w_inter * cq
            den = s + w_inter * jnp.sum(n_row * q, axis=1, keepdims=True)
            hout = num / jnp.maximum(jnp.abs(den), jnp.exp(-m_t))
            hm_ref[b, :, h * DV:(h + 1) * DV] = (_sigmoid(o) * hout).astype(BF16)

            w_last = jnp.exp(ig - m_t)
            decay = jnp.exp(inter - m_t)
            vw = v * w_last
            vw_col = jnp.sum(jnp.where(eye, vw, 0.0), axis=1, keepdims=True)
            c_out[0, b, h] = decay * cmat + vw_col * k
            n_out[0, b, h:h + 1, :] = decay * n_row + w_last * k
            m_new_all = jnp.where(lane == h, m_t, m_new_all)
        m_out[b] = m_new_all


def _mlstm_sample(z3, bias_row, c0, n0, m0_pad, layer, bb):
    nb = z3.shape[0]

    def zspec(width, off):
        blk = off // width
        return pl.BlockSpec((bb, 1, width), lambda b: (b, 0, blk))

    c_in = pl.BlockSpec((1, bb, N_MH, DV, DK), lambda b: (layer, b, 0, 0, 0))
    n_in = pl.BlockSpec((1, bb, N_MH, DK), lambda b: (layer, b, 0, 0))
    c_blk = pl.BlockSpec((1, bb, N_MH, DV, DK), lambda b: (0, b, 0, 0, 0))
    n_blk = pl.BlockSpec((1, bb, N_MH, DK), lambda b: (0, b, 0, 0))

    return pl.pallas_call(
        _mlstm_sample_kernel,
        grid=(nb // bb,),
        in_specs=[
            zspec(QK_W, _Z_QM), zspec(QK_W, _Z_KM), zspec(V_W, _Z_VM), zspec(V_W, _Z_OM),
            zspec(LANES, _Z_IF),
            pl.BlockSpec((1, LANES), lambda b: (0, 0)),
            c_in,
            n_in,
            pl.BlockSpec((bb, 1, LANES), lambda b: (b, 0, 0)),
        ],
        out_specs=[
            pl.BlockSpec((bb, 1, V_W), lambda b: (b, 0, 0)),
            c_blk,
            n_blk,
            pl.BlockSpec((bb, 1, LANES), lambda b: (b, 0, 0)),
        ],
        out_shape=[
            jax.ShapeDtypeStruct((nb, 1, V_W), BF16),
            jax.ShapeDtypeStruct((1, nb, N_MH, DV, DK), F32),
            jax.ShapeDtypeStruct((1, nb, N_MH, DK), F32),
            jax.ShapeDtypeStruct((nb, 1, LANES), F32),
        ],
        compiler_params=_cparams(("parallel",)),
        name="mlstm_sample",
    )(z3, z3, z3, z3, z3, bias_row, c0, n0, m0_pad)


def _rope_tables(pos):
    half = ROT // 2
    inv = THETA ** (-jnp.arange(half, dtype=F32) * 2.0 / ROT)
    ang = pos.astype(F32)[:, None] * inv[None, :]
    cos = jnp.cos(ang)
    sin = jnp.sin(ang)
    t = pos.shape[0]
    cos_t = jnp.concatenate([cos, cos, jnp.ones((t, HD - ROT), F32)], axis=1)
    sin_a = jnp.concatenate([-sin, jnp.zeros((t, HD - half), F32)], axis=1)
    sin_b = jnp.concatenate([jnp.zeros((t, half), F32), sin, jnp.zeros((t, HD - ROT), F32)], axis=1)
    rep = LANES // HD
    return tuple(jnp.tile(a, (1, rep)) for a in (cos_t, sin_a, sin_b))


def _rope(x, cos_t, sin_a, sin_b):
    w = x.shape[1]
    rep = w // LANES
    half = ROT // 2
    if rep > 1:
        cos_t = jnp.concatenate([cos_t] * rep, axis=1)
        sin_a = jnp.concatenate([sin_a] * rep, axis=1)
        sin_b = jnp.concatenate([sin_b] * rep, axis=1)
    up = pltpu.roll(x, w - half, axis=1)
    down = pltpu.roll(x, half, axis=1)
    return x * cos_t + up * sin_a + down * sin_b


def _swa_prompt_kernel(sink_ref, q_ref, kc_ref, kp_ref, vc_ref, vp_ref, ha_ref):
    i = pl.program_id(1)
    w = WINDOW
    kk = jnp.concatenate([kp_ref[...], kc_ref[...]], axis=0)
    vv = jnp.concatenate([vp_ref[...], vc_ref[...]], axis=0)
    vv_t = vv.T.astype(BF16)
    s_idx = lax.broadcasted_iota(jnp.int32, (2 * w, w), 0)
    a_idx = lax.broadcasted_iota(jnp.int32, (2 * w, w), 1)
    valid = (s_idx >= a_idx) & (s_idx <= a_idx + w) & ((i > 0) | (s_idx >= w))
    lane = lax.broadcasted_iota(jnp.int32, (2 * w, LANES), 1)
    pair_w = 2 * HD
    for half in range(N_KV // 2):
        kh = kk[:, half * pair_w:(half + 1) * pair_w]
        kh_sw = pltpu.roll(kh, HD, axis=1)
        for gg in range(2):
            g = 2 * half + gg
            k_lo = jnp.where(lane < HD, kh if gg == 0 else kh_sw, 0.0).astype(BF16)
            k_hi = jnp.where(lane >= HD, kh_sw if gg == 0 else kh, 0.0).astype(BF16)
            vg_t = vv_t[g * HD:(g + 1) * HD, :]
            sts = []
            for a in range(GROUP // 2):
                col = (g * GROUP + 2 * a) * HD
                qpair = q_ref[:, col:col + pair_w].astype(BF16)
                sts += [_dot_nt(k_lo, qpair), _dot_nt(k_hi, qpair)]
            es, rs = [], []
            for hh in range(GROUP):
                sink = sink_ref[g * GROUP + hh]
                st = jnp.where(valid, sts[hh], -jnp.inf)
                m = jnp.maximum(jnp.max(st, axis=0, keepdims=True), sink)
                e = jnp.exp(st - m)
                es.append(e.astype(BF16))
                rs.append(1.0 / (jnp.sum(e, axis=0, keepdims=True) + jnp.exp(sink - m)))
            ots = [_dot(vg_t, es[hh]) * rs[hh] for hh in range(GROUP)]
            for a in range(GROUP // 2):
                col = (g * GROUP + 2 * a) * HD
                ha_ref[:, col:col + pair_w] = jnp.concatenate(ots[2 * a:2 * a + 2], axis=0).T.astype(BF16)


def _swa_prompt(z, sinks3, batch, seq):
    rows = z.shape[0]
    nb = seq // WINDOW
    w = WINDOW

    def cur(width, off):
        blk = off // width
        return pl.BlockSpec((w, width), lambda b, i: (b * nb + i, blk))

    def prev(width, off):
        blk = off // width
        return pl.BlockSpec((w, width), lambda b, i: (b * nb + jnp.maximum(i - 1, 0), blk))

    return pl.pallas_call(
        _swa_prompt_kernel,
        grid=(batch, nb),
        in_specs=[
            pl.BlockSpec(memory_space=pltpu.SMEM),
            cur(ATT_Q, _Z_QA), cur(ATT_KV, _Z_KA), prev(ATT_KV, _Z_KA),
            cur(ATT_KV, _Z_VA), prev(ATT_KV, _Z_VA),
        ],
        out_specs=pl.BlockSpec((w, ATT_Q), lambda b, i: (b * nb + i, 0)),
        out_shape=jax.ShapeDtypeStruct((rows, ATT_Q), BF16),
        compiler_params=_cparams(("parallel", "parallel")),
        name="swa_prompt",
    )(sinks3, z, z, z, z, z)


def _swa_sample_kernel(q_ref, kn_ref, vn_ref, kp_ref, vp_ref, sink_ref,
                       o_ref, ko_ref, vo_ref, *, bb):
    w = WINDOW
    hrow = lax.broadcasted_iota(jnp.int32, (N_Q, ATT_KV), 0) // GROUP
    glane = lax.broadcasted_iota(jnp.int32, (N_Q, ATT_KV), 1) // HD
    own = hrow == glane
    srow = lax.broadcasted_iota(jnp.int32, (w, ATT_KV), 0)
    sink = sink_ref[...]
    for b in range(bb):
        q = q_ref[b]
        qm = jnp.where(own, jnp.concatenate([q] * N_KV, axis=1), 0.0)
        k_past = kp_ref[b]
        v_past = vp_ref[b]
        k_new = kn_ref[b]
        v_new = vn_ref[b]
        s_past = _dot_nt(qm.astype(BF16), k_past.astype(BF16))
        s_new = jnp.sum(qm * k_new, axis=1, keepdims=True)
        m = jnp.maximum(jnp.maximum(jnp.max(s_past, axis=1, keepdims=True), s_new), sink)
        e_past = jnp.exp(s_past - m)
        e_new = jnp.exp(s_new - m)
        denom = jnp.sum(e_past, axis=1, keepdims=True) + e_new + jnp.exp(sink - m)
        r = 1.0 / denom
        pv = _dot((e_past * r).astype(BF16), v_past.astype(BF16)) + (e_new * r) * v_new
        pv = jnp.where(own, pv, 0.0)
        o = pv[:, 0:HD]
        for g in range(1, N_KV):
            o = o + pv[:, g * HD:(g + 1) * HD]
        o_ref[b] = o
        ko_ref[b] = jnp.where(srow == w - 1, k_new, pltpu.roll(k_past, w - 1, axis=0))
        vo_ref[b] = jnp.where(srow == w - 1, v_new, pltpu.roll(v_past, w - 1, axis=0))


def _swa_sample(q3, k_new3, v_new3, k_past, v_past, sink_col, bb):
    nb = q3.shape[0]
    w = WINDOW
    return pl.pallas_call(
        functools.partial(_swa_sample_kernel, bb=bb),
        grid=(nb // bb,),
        in_specs=[
            pl.BlockSpec((bb, N_Q, HD), lambda i: (i, 0, 0)),
            pl.BlockSpec((bb, 1, ATT_KV), lambda i: (i, 0, _Z_KA // ATT_KV)),
            pl.BlockSpec((bb, 1, ATT_KV), lambda i: (i, 0, _Z_VA // ATT_KV)),
            pl.BlockSpec((bb, w, ATT_KV), lambda i: (i, 0, 0)),
            pl.BlockSpec((bb, w, ATT_KV), lambda i: (i, 0, 0)),
            pl.BlockSpec((N_Q, 1), lambda i: (0, 0)),
        ],
        out_specs=[
            pl.BlockSpec((bb, N_Q, HD), lambda i: (i, 0, 0)),
            pl.BlockSpec((bb, w, ATT_KV), lambda i: (i, 0, 0)),
            pl.BlockSpec((bb, w, ATT_KV), lambda i: (i, 0, 0)),
        ],
        out_shape=[
            jax.ShapeDtypeStruct((nb, N_Q, HD), F32),
            jax.ShapeDtypeStruct((nb, w, ATT_KV), F32),
            jax.ShapeDtypeStruct((nb, w, ATT_KV), F32),
        ],
        compiler_params=_cparams(("parallel",)),
        name="swa_sample",
    )(q3, k_new3, v_new3, k_past, v_past, sink_col)


def _merge1_kernel(hm_ref, ha_ref, ga_ref, gb_ref, wa_ref, wb_ref, t_ref):
    a = _dot(hm_ref[...], wa_ref[...])
    b = _dot(ha_ref[...], wb_ref[...])
    t_ref[...] = (_sigmoid(ga_ref[...]) * a + _sigmoid(gb_ref[...]) * b).astype(BF16)


def _merge1(hm, ha, z, wa, wb, tm, tn):
    rows, d = hm.shape
    n = wa.shape[1]
    return pl.pallas_call(
        _merge1_kernel,
        grid=(rows // tm, n // tn),
        in_specs=[
            pl.BlockSpec((tm, d), lambda i, j: (i, 0)),
            pl.BlockSpec((tm, d), lambda i, j: (i, 0)),
            pl.BlockSpec((tm, tn), lambda i, j: (i, _Z_GA // tn + j)),
            pl.BlockSpec((tm, tn), lambda i, j: (i, _Z_GB // tn + j)),
            pl.BlockSpec((d, tn), lambda i, j: (0, j)),
            pl.BlockSpec((d, tn), lambda i, j: (0, j)),
        ],
        out_specs=pl.BlockSpec((tm, tn), lambda i, j: (i, j)),
        out_shape=jax.ShapeDtypeStruct((rows, n), BF16),
        compiler_params=_cparams(("parallel", "arbitrary")),
        name="merge1",
    )(hm, ha, z, z, wa, wb)


def _merge2_kernel(t_ref, w_ref, x_ref, g1_ref, g2_ref, x1_ref, h2_ref):
    mix = _dot(t_ref[...], w_ref[...])
    x1 = x_ref[...] + _rms(mix, g1_ref[...])
    x1_ref[...] = x1
    h2_ref[...] = _rms(x1, g2_ref[...]).astype(BF16)


def _merge2(t, w_out, x2d, g1, g2, tm):
    rows, d = x2d.shape
    row = lambda i: (i, 0)
    const = lambda i: (0, 0)
    return pl.pallas_call(
        _merge2_kernel,
        grid=(rows // tm,),
        in_specs=[
            pl.BlockSpec((tm, d), row),
            pl.BlockSpec((d, d), const, pipeline_mode=pl.Buffered(1)),
            pl.BlockSpec((tm, d), row),
            pl.BlockSpec((1, d), const),
            pl.BlockSpec((1, d), const),
        ],
        out_specs=[pl.BlockSpec((tm, d), row), pl.BlockSpec((tm, d), row)],
        out_shape=[jax.ShapeDtypeStruct((rows, d), F32), jax.ShapeDtypeStruct((rows, d), BF16)],
        compiler_params=_cparams(("parallel",)),
        name="merge2",
    )(t, w_out, x2d, g1.reshape(1, d), g2.reshape(1, d))


def _gelu_tanh(x):
    return 0.5 * x * (1.0 + jnp.tanh(0.7978845608028654 * (x + 0.044715 * (x * x * x))))


def _conv_taps(u, u1, u2, cw_ref, cb_ref, cs=slice(None)):
    return cb_ref[:, cs] + u2 * cw_ref[0:1, cs] + u1 * cw_ref[1:2, cs] + u * cw_ref[2:3, cs]


def _ffn_tail(j, nf, y, wd_ref, x1_ref, g_ref, y_ref, acc_ref):
    @pl.when(j == 0)
    def _():
        acc_ref[...] = _dot(y, wd_ref[...])

    @pl.when(j > 0)
    def _():
        acc_ref[...] += _dot(y, wd_ref[...])

    @pl.when(j == nf - 1)
    def _():
        y_ref[...] = x1_ref[...] + _rms(acc_ref[...], g_ref[...])


def _ffn_prompt_kernel(h_ref, wg_ref, wv_ref, cwg_ref, cwv_ref, cbg_ref, cbv_ref, wd_ref,
                       x1_hbm, g_ref, y_ref, sg_ref, sv_ref, acc_ref, carry_ref, x1_buf, x1_sem,
                       *, tiles_per_seq):
    i = pl.program_id(0)
    j = pl.program_id(1)
    nf = pl.num_programs(1)
    tm = h_ref.shape[0]

    def x1_copy():
        r0 = pl.multiple_of(i * tm, tm)
        return pltpu.make_async_copy(x1_hbm.at[pl.ds(r0, tm), :], x1_buf, x1_sem.at[0])

    @pl.when(j == nf - 2)
    def _():
        x1_copy().start()

    @pl.when(j == nf - 1)
    def _():
        x1_copy().wait()

    @pl.when((i == 0) & (j == 0))
    def _():
        carry_ref[...] = jnp.zeros_like(carry_ref)

    seq_start = (i % tiles_per_seq) == 0
    h = h_ref[...]

    def branch(w_ref, cw_ref, cb_ref, slot, state_ref):
        u = _dot(h, w_ref[...])
        prev = jnp.where(seq_start, 0.0, carry_ref[slot])
        uu = jnp.concatenate([prev, u], axis=0)
        u1 = uu[7:7 + tm, :]
        u2 = uu[6:6 + tm, :]
        tail = u[tm - 8:tm, :]
        carry_ref[slot] = tail
        state_ref[0] = tail
        return _conv_taps(u, u1, u2, cw_ref, cb_ref)

    cg = branch(wg_ref, cwg_ref, cbg_ref, 2 * j, sg_ref)
    cv = branch(wv_ref, cwv_ref, cbv_ref, 2 * j + 1, sv_ref)
    y = (_gelu_tanh(cg) * cv).astype(BF16)
    _ffn_tail(j, nf, y, wd_ref, x1_buf, g_ref, y_ref, acc_ref)


def _ffn_sample_kernel(h_ref, wg_ref, wv_ref, cwg_ref, cwv_ref, cbg_ref, cbv_ref, wd_ref,
                       x1_ref, g_ref, p2g_ref, p2v_ref, p1g_ref, p1v_ref,
                       y_ref, ug_ref, uv_ref, acc_ref):
    j = pl.program_id(1)
    nf = pl.num_programs(1)
    h = h_ref[...]
    ug = _dot(h, wg_ref[...])
    uv = _dot(h, wv_ref[...])
    ug_ref[...] = ug
    uv_ref[...] = uv
    cg = _conv_taps(ug, p1g_ref[...], p2g_ref[...], cwg_ref, cbg_ref)
    cv = _conv_taps(uv, p1v_ref[...], p2v_ref[...], cwv_ref, cbv_ref)
    y = (_gelu_tanh(cg) * cv).astype(BF16)
    _ffn_tail(j, nf, y, wd_ref, x1_ref, g_ref, y_ref, acc_ref)


def _ffn(h2, x1, w_up, w_conv, b_conv2d, w_down, g, tm, tf, seq=None, past=None):
    rows, d = h2.shape
    f = w_down.shape[0]
    nf = f // tf
    row = lambda i, j: (i, 0)
    gcol = lambda i, j: (0, j)
    vcol = lambda i, j: (0, nf + j)
    in_specs = [
        pl.BlockSpec((tm, d), row),
        pl.BlockSpec((d, tf), gcol), pl.BlockSpec((d, tf), vcol),
        pl.BlockSpec((3, tf), gcol), pl.BlockSpec((3, tf), vcol),
        pl.BlockSpec((1, tf), gcol), pl.BlockSpec((1, tf), vcol),
        pl.BlockSpec((tf, d), lambda i, j: (j, 0)),
        pl.BlockSpec((tm, d), row),
        pl.BlockSpec((1, d), lambda i, j: (0, 0)),
    ]
    args = [h2, w_up, w_up, w_conv, w_conv, b_conv2d, b_conv2d, w_down, x1, g.reshape(1, d)]
    scratch = [pltpu.VMEM((tm, d), F32)]
    if past is None:
        tiles_per_seq = seq // tm
        kern = functools.partial(_ffn_prompt_kernel, tiles_per_seq=tiles_per_seq)
        state_spec = pl.BlockSpec((1, 8, tf), lambda i, j: (i, 0, j))
        out_specs = [pl.BlockSpec((tm, d), row), state_spec, state_spec]
        out_shape = [jax.ShapeDtypeStruct((rows, d), F32),
                     jax.ShapeDtypeStruct((rows // tm, 8, f), F32),
                     jax.ShapeDtypeStruct((rows // tm, 8, f), F32)]
        scratch.append(pltpu.VMEM((2 * nf, 8, tf), F32))
        assert nf >= 2
        in_specs[8] = pl.BlockSpec(memory_space=pl.ANY)
        scratch += [pltpu.VMEM((tm, d), F32), pltpu.SemaphoreType.DMA((1,))]
        name = "ffn_prompt"
    else:
        p2, p1 = past
        kern = _ffn_sample_kernel
        ucol = pl.BlockSpec((tm, tf), lambda i, j: (i, j))
        in_specs += [
            pl.BlockSpec((tm, tf), lambda i, j: (i, j)), pl.BlockSpec((tm, tf), lambda i, j: (i, nf + j)),
            pl.BlockSpec((tm, tf), lambda i, j: (i, j)), pl.BlockSpec((tm, tf), lambda i, j: (i, nf + j)),
        ]
        args += [p2, p2, p1, p1]
        out_specs = [pl.BlockSpec((tm, d), row), ucol, ucol]
        out_shape = [jax.ShapeDtypeStruct((rows, d), F32),
                     jax.ShapeDtypeStruct((rows, f), F32),
                     jax.ShapeDtypeStruct((rows, f), F32)]
        name = "ffn_sample"
    return pl.pallas_call(
        kern,
        grid=(rows // tm, nf),
        in_specs=in_specs,
        out_specs=out_specs,
        out_shape=out_shape,
        scratch_shapes=scratch,
        compiler_params=_cparams(("arbitrary", "arbitrary")),
        name=name,
    )(*args)


def _regroup_w_in(w_in):
    d = w_in.shape[0]
    o_if = QK_W * 2 + V_W * 2
    o_qa = o_if + 2 * N_MH
    o_ka = o_qa + ATT_Q
    o_ga = o_ka + 2 * ATT_KV
    assert o_if == _Z_QA
    parts = [
        w_in[:, o_qa:o_ka],
        w_in[:, o_ga:],
        w_in[:, o_ka:o_ga],
        w_in[:, o_if:o_qa],
    ]
    used = o_if + sum(p.shape[1] for p in parts)
    parts.append(jnp.zeros((d, _Z_W - used), w_in.dtype))
    return jnp.concatenate(parts, axis=1)


def _pick(rows, pref):
    t = min(rows, pref)
    while rows % t:
        t //= 2
    return t


def kernel(x_prompt, x_sample, state_mlstm_c, state_mlstm_n, state_mlstm_m, cache_swa_k, cache_swa_v,
           state_ffn_conv, g_pre_mix, w_in, b_if, attn_sinks, w_branch_a, w_branch_b, w_out, g_post_mix,
           g_pre_ffn, w_up, w_conv, b_conv, w_down, g_post_ffn):
    bp, tp, d = x_prompt.shape
    bs, ts, _ = x_sample.shape
    depth = w_in.shape[0]
    assert depth == 1 and ts == 1 and d == D_MODEL
    f = w_down.shape[1]
    tf = 512
    assert f % tf == 0

    l = 0
    w_lo = w_in[l].astype(BF16)
    w_hi = _regroup_w_in(w_lo)
    wa = w_branch_a[l].astype(BF16)
    wb = w_branch_b[l].astype(BF16)
    wo = w_out[l].astype(BF16)
    wu = w_up[l].astype(BF16)
    wd = w_down[l].astype(BF16)
    bias_row = jnp.concatenate([b_if[l].astype(F32), jnp.zeros((LANES - 2 * N_MH,), F32)]).reshape(1, LANES)
    sinks = attn_sinks[l].astype(F32)
    b_conv2d = b_conv[l].reshape(1, 2 * f)

    xp = x_prompt.reshape(bp * tp, d)
    zp = _project(xp, g_pre_mix[l], w_lo, w_hi, _rope_tables(jnp.arange(tp)), _pick(tp, 1024))
    hm_p3, c_p, n_p, m_p = _mlstm_prompt(zp, bias_row, bp, tp, _pick(bp, 4))
    hm_p = hm_p3.reshape(bp * tp, V_W)
    ha_p = _swa_prompt(zp, sinks, bp, tp)
    t_p = _merge1(hm_p, ha_p, zp, wa, wb, _pick(bp * tp, 1024), 512)
    x1_p, h2_p = _merge2(t_p, wo, xp, g_post_mix[l], g_pre_ffn[l], _pick(bp * tp, 512))
    tm_ffn = _pick(tp, 512)
    y_p, sg_p, sv_p = _ffn(h2_p, x1_p, wu, w_conv[l], b_conv2d, wd, g_post_ffn[l],
                           tm_ffn, tf, seq=tp)
    seq_end = slice(tp // tm_ffn - 1, None, tp // tm_ffn)

    y_prompt = y_p.reshape(bp, tp, d)
    c_prompt = c_p[None]
    n_prompt = n_p[None]
    m_prompt = m_p[:, :N_MH, 0][None]
    kv_tail = zp.reshape(bp, tp, _Z_W)[:, tp - WINDOW:, _Z_KA:_Z_KA + 2 * ATT_KV]
    k_prompt = kv_tail[..., :ATT_KV].reshape(bp, WINDOW, N_KV, HD)[None]
    v_prompt = kv_tail[..., ATT_KV:].reshape(bp, WINDOW, N_KV, HD)[None]
    conv_prompt = jnp.concatenate([sg_p[seq_end, 6:8, :], sv_p[seq_end, 6:8, :]], axis=-1)[None]

    xs = x_sample.reshape(bs, d)
    tabs_s = tuple(jnp.broadcast_to(a, (bs, LANES)) for a in _rope_tables(PAST_LEN + jnp.arange(ts)))
    zs = _project(xs, g_pre_mix[l], w_lo, w_hi, tabs_s, bs)
    zs3 = zs.reshape(bs, 1, _Z_W)
    m0_pad = jnp.pad(state_mlstm_m[l].astype(F32), ((0, 0), (0, LANES - N_MH))).reshape(bs, 1, LANES)
    hm_s3, c_sample, n_sample, m_s3 = _mlstm_sample(zs3, bias_row, state_mlstm_c, state_mlstm_n, m0_pad, l,
                                                          _pick(bs, 4))
    o_s3, k_s, v_s = _swa_sample(
        zs[:, _Z_QA:_Z_QA + ATT_Q].reshape(bs, N_Q, HD), zs3, zs3,
        cache_swa_k[l].reshape(bs, WINDOW, ATT_KV), cache_swa_v[l].reshape(bs, WINDOW, ATT_KV),
        sinks.reshape(N_Q, 1), _pick(bs, 8))
    ha_s = o_s3.reshape(bs, ATT_Q).astype(BF16)
    t_s = _merge1(hm_s3.reshape(bs, V_W), ha_s, zs, wa, wb, bs, 512)
    x1_s, h2_s = _merge2(t_s, wo, xs, g_post_mix[l], g_pre_ffn[l], bs)
    past = state_ffn_conv[l].astype(F32)
    y_s, ug_s, uv_s = _ffn(h2_s, x1_s, wu, w_conv[l], b_conv2d, wd, g_post_ffn[l],
                           bs, tf, past=(past[:, 0, :], past[:, 1, :]))

    y_sample = y_s.reshape(bs, ts, d)
    m_sample = m_s3[:, 0, :N_MH][None]
    k_sample = k_s.reshape(bs, WINDOW, N_KV, HD)[None]
    v_sample = v_s.reshape(bs, WINDOW, N_KV, HD)[None]
    conv_sample = jnp.stack([past[:, 1, :], jnp.concatenate([ug_s, uv_s], axis=-1)], axis=1)[None]

    return (y_prompt, y_sample,
            c_prompt, n_prompt, m_prompt, k_prompt, v_prompt, conv_prompt,
            c_sample, n_sample, m_sample, k_sample, v_sample, conv_sample)
```

```python
import functools

import jax
import jax.numpy as jnp
from jax import lax
from jax.experimental import pallas as pl
from jax.experimental.pallas import tpu as pltpu

F32 = jnp.float32
BF16 = jnp.bfloat16

N_MH = 4
DK = 256
DV = 512
CHUNK = 64
HD = 64
N_Q = 32
N_KV = 4
GROUP = N_Q // N_KV
WINDOW = 128
ROT = 16
THETA = 500000.0
PAST_LEN = 16384
EPS = 1e-6
D_MODEL = 2048
QK_W = N_MH * DK
V_W = N_MH * DV
ATT_Q = N_Q * HD
ATT_KV = N_KV * HD

_Z_QM = 0
_Z_KM = _Z_QM + QK_W
_Z_VM = _Z_KM + QK_W
_Z_OM = _Z_VM + V_W
_Z_QA = _Z_OM + V_W
_Z_GA = _Z_QA + ATT_Q
_Z_GB = _Z_GA + D_MODEL
_Z_KA = _Z_GB + D_MODEL
_Z_VA = _Z_KA + ATT_KV
_Z_IF = _Z_VA + ATT_KV
LANES = 128
_PROJ_TN = 1024
_Z_W = -(-(_Z_IF + LANES) // _PROJ_TN) * _PROJ_TN
assert _Z_QA % _PROJ_TN == 0 and _Z_GA % _PROJ_TN == 0 and _Z_KA % _PROJ_TN == 0

_VMEM_LIMIT = 56 * 1024 * 1024


def _cparams(sem):
    return pltpu.CompilerParams(dimension_semantics=sem, vmem_limit_bytes=_VMEM_LIMIT)


def _rms(x, g):
    return x * lax.rsqrt(jnp.mean(x * x, axis=-1, keepdims=True) + EPS) * g


def _sigmoid(x):
    return 1.0 / (1.0 + jnp.exp(-x))


def _log_sigmoid(x):
    return jnp.minimum(x, 0.0) - jnp.log1p(jnp.exp(-jnp.abs(x)))


def _dot(a, b):
    return jnp.dot(a, b, preferred_element_type=F32)


def _dot_nt(a, b):
    return lax.dot_general(a, b, (((1,), (1,)), ((), ())), preferred_element_type=F32)


def _dot_tn(a, b):
    return lax.dot_general(a, b, (((0,), (0,)), ((), ())), preferred_element_type=F32)


def _proj_kernel(x_hbm, g_ref, wlo_ref, whi_ref, cos_ref, sa_ref, sb_ref, z_ref, h_ref, xbuf, xsem):
    i = pl.program_id(0)
    j = pl.program_id(1)
    jq0 = _Z_QA // _PROJ_TN
    jq1 = _Z_GA // _PROJ_TN
    jk = _Z_KA // _PROJ_TN
    tm = h_ref.shape[0]

    def x_copy(tile, slot):
        r0 = pl.multiple_of(tile * tm, tm)
        return pltpu.make_async_copy(x_hbm.at[pl.ds(r0, tm), :], xbuf.at[slot], xsem.at[slot])

    @pl.when((i == 0) & (j == 0))
    def _():
        x_copy(0, 0).start()

    @pl.when((j == 1) & (i + 1 < pl.num_programs(0)))
    def _():
        x_copy(i + 1, (i + 1) % 2).start()

    @pl.when(j == 0)
    def _():
        x_copy(i, i % 2).wait()
        h_ref[...] = _rms(xbuf[i % 2], g_ref[...]).astype(BF16)

    is_lo = j < jq0
    is_q = (j >= jq0) & (j < jq1)
    is_k = j == jk

    @pl.when(is_lo)
    def _():
        z_ref[...] = _dot(h_ref[...], wlo_ref[...])

    @pl.when(is_q)
    def _():
        acc = _dot(h_ref[...], whi_ref[...])
        z_ref[...] = _rope(acc, cos_ref[...], sa_ref[...], sb_ref[...]) * (HD ** -0.5)

    @pl.when(is_k)
    def _():
        acc = _dot(h_ref[...], whi_ref[...])
        z_ref[...] = acc
        z_ref[:, :ATT_KV] = _rope(acc[:, :ATT_KV], cos_ref[...], sa_ref[...], sb_ref[...])

    @pl.when(jnp.logical_not(is_lo | is_q | is_k))
    def _():
        z_ref[...] = _dot(h_ref[...], whi_ref[...])


def _project(x2d, g, w_lo, w_hi, tables, tm):
    rows, d = x2d.shape
    tn = _PROJ_TN
    n_lo = _Z_QA // tn
    n = n_lo * tn + w_hi.shape[1]
    assert n == _Z_W and w_lo.shape[1] >= n_lo * tn
    tab_tiles = tables[0].shape[0] // tm
    tab = pl.BlockSpec((tm, LANES), lambda i, j: (i % tab_tiles, 0))
    return pl.pallas_call(
        _proj_kernel,
        grid=(rows // tm, n // tn),
        in_specs=[
            pl.BlockSpec(memory_space=pl.ANY),
            pl.BlockSpec((1, d), lambda i, j: (0, 0)),
            pl.BlockSpec((d, tn), lambda i, j: (0, jnp.minimum(j, n_lo - 1))),
            pl.BlockSpec((d, tn), lambda i, j: (0, jnp.maximum(j - n_lo, 0))),
            tab, tab, tab,
        ],
        out_specs=pl.BlockSpec((tm, tn), lambda i, j: (i, j)),
        out_shape=jax.ShapeDtypeStruct((rows, n), F32),
        scratch_shapes=[pltpu.VMEM((tm, d), BF16), pltpu.VMEM((2, tm, d), F32),
                        pltpu.SemaphoreType.DMA((2,))],
        compiler_params=_cparams(("arbitrary", "arbitrary")),
        name="proj",
    )(x2d, g.reshape(1, d), w_lo, w_hi, *tables)


def _mlstm_prompt_kernel(q_ref, k_ref, v_ref, o_ref, g_ref, bias_ref,
                         hm_ref, c_out, n_out, m_out, ct_s, n_s, m_s):
    c = pl.program_id(1)
    last = pl.num_programs(1) - 1
    L = CHUNK
    nseq = q_ref.shape[0]

    @pl.when(c == 0)
    def _():
        ct_s[...] = jnp.zeros_like(ct_s)
        n_s[...] = jnp.zeros_like(n_s)
        m_s[...] = jnp.zeros_like(m_s)

    row = lax.broadcasted_iota(jnp.int32, (L, LANES), 0)
    ri = lax.broadcasted_iota(jnp.int32, (L, L), 0)
    ci = lax.broadcasted_iota(jnp.int32, (L, L), 1)
    eye = ri == ci
    causal = ci <= ri

    heads = []
    for bi in range(nseq):
        gates = g_ref[bi] + bias_ref[...]
        bcum = _log_sigmoid(gates)
        for d in (1, 2, 4, 8, 16, 32):
            bcum = bcum + jnp.where(row >= d, pltpu.roll(bcum, d, axis=0), 0.0)
        for h in range(N_MH):
            q = q_ref[bi, :, h * DK:(h + 1) * DK]
            k = k_ref[bi, :, h * DK:(h + 1) * DK] * (DK ** -0.5)
            qb = q.astype(BF16)
            ct = ct_s[bi * N_MH + h]
            heads.append((bi, h, gates, bcum, q, k, _dot_nt(qb, k.astype(BF16)), _dot(qb, ct.astype(BF16))))

    for bi, h, gates, bcum, q, k, qk, cq in heads:
        v = v_ref[bi, :, h * DV:(h + 1) * DV]
        o = o_ref[bi, :, h * DV:(h + 1) * DV]
        b_col = bcum[:, N_MH + h:N_MH + h + 1]
        ig_col = gates[:, h:h + 1]
        a_col = ig_col - b_col
        a_lane = jnp.sum(jnp.where(eye, a_col, 0.0), axis=0, keepdims=True)
        dmat = jnp.where(causal, b_col + a_lane, -jnp.inf)
        m_prev = m_s[bi, h:h + 1, 0:1]
        inter = b_col + m_prev
        m_t = jnp.maximum(jnp.max(dmat, axis=1, keepdims=True), inter)
        vb = v.astype(BF16)
        s = qk * jnp.exp(dmat - m_t)
        w_inter = jnp.exp(inter - m_t)
        m_new = m_t[L - 1:L, :]
        b_last = b_col[L - 1:L, :]
        w_last = jnp.exp(b_last - b_col + ig_col - m_new)
        decay = jnp.exp(b_last + m_prev - m_new)
        kw = k * w_last
        ct = ct_s[bi * N_MH + h]
        ct_s[bi * N_MH + h] = decay * ct + _dot_tn(kw.astype(BF16), vb)
        num = _dot(s.astype(BF16), vb) + w_inter * cq
        n_row = n_s[bi, h:h + 1, :]
        den = jnp.sum(s, axis=1, keepdims=True) + w_inter * jnp.sum(q * n_row, axis=1, keepdims=True)
        hout = num / jnp.maximum(jnp.abs(den), jnp.exp(-m_t))
        hm_ref[bi, :, h * DV:(h + 1) * DV] = (_sigmoid(o) * hout).astype(BF16)
        n_s[bi, h:h + 1, :] = decay * n_row + jnp.sum(kw, axis=0, keepdims=True)
        m_s[bi, h:h + 1, :] = jnp.broadcast_to(m_new, (1, LANES))

    @pl.when(c == last)
    def _():
        for bi in range(nseq):
            for h in range(N_MH):
                c_out[bi, h] = ct_s[bi * N_MH + h].T
        n_out[...] = n_s[:, :N_MH, :]
        m_out[...] = m_s[...]


def _mlstm_prompt(z, bias_row, batch, seq, bb):
    nc = seq // CHUNK
    z3 = z.reshape(batch, seq, z.shape[1])

    def zspec(width, off):
        blk = off // width
        return pl.BlockSpec((bb, CHUNK, width), lambda b, c: (b, c, blk))

    return pl.pallas_call(
        _mlstm_prompt_kernel,
        grid=(batch // bb, nc),
        in_specs=[
            zspec(QK_W, _Z_QM), zspec(QK_W, _Z_KM), zspec(V_W, _Z_VM), zspec(V_W, _Z_OM),
            zspec(LANES, _Z_IF),
            pl.BlockSpec((1, LANES), lambda b, c: (0, 0)),
        ],
        out_specs=[
            pl.BlockSpec((bb, CHUNK, V_W), lambda b, c: (b, c, 0)),
            pl.BlockSpec((bb, N_MH, DV, DK), lambda b, c: (b, 0, 0, 0)),
            pl.BlockSpec((bb, N_MH, DK), lambda b, c: (b, 0, 0)),
            pl.BlockSpec((bb, 8, LANES), lambda b, c: (b, 0, 0)),
        ],
        out_shape=[
            jax.ShapeDtypeStruct((batch, seq, V_W), BF16),
            jax.ShapeDtypeStruct((batch, N_MH, DV, DK), F32),
            jax.ShapeDtypeStruct((batch, N_MH, DK), F32),
            jax.ShapeDtypeStruct((batch, 8, LANES), F32),
        ],
        scratch_shapes=[
            pltpu.VMEM((bb * N_MH, DK, DV), F32),
            pltpu.VMEM((bb, 8, DK), F32),
            pltpu.VMEM((bb, 8, LANES), F32),
        ],
        compiler_params=_cparams(("parallel", "arbitrary")),
        name="mlstm_prompt",
    )(z3, z3, z3, z3, z3, bias_row)


def _mlstm_sample_kernel(q_ref, k_ref, v_ref, o_ref, g_ref, bias_ref, c_ref, n_ref, m_ref,
                         hm_ref, c_out, n_out, m_out):
    ri = lax.broadcasted_iota(jnp.int32, (DV, DV), 0)
    ci = lax.broadcasted_iota(jnp.int32, (DV, DV), 1)
    eye = ri == ci
    lane = lax.broadcasted_iota(jnp.int32, (1, LANES), 1)

    for b in range(q_ref.shape[0]):
        gates = g_ref[b] + bias_ref[...]
        lf_all = _log_sigmoid(gates)
        m_all = m_ref[b]
        m_new_all = jnp.zeros((1, LANES), F32)
        for h in range(N_MH):
            q = q_ref[b, :, h * DK:(h + 1) * DK]
            k = k_ref[b, :, h * DK:(h + 1) * DK] * (DK ** -0.5)
            v = v_ref[b, :, h * DV:(h + 1) * DV]
            o = o_ref[b, :, h * DV:(h + 1) * DV]
            ig = gates[:, h:h + 1]
            lf = lf_all[:, N_MH + h:N_MH + h + 1]
            m_prev = m_all[:, h:h + 1]
            inter = lf + m_prev
            m_t = jnp.maximum(ig, inter)
            s = jnp.sum(q * k, axis=1, keepdims=True) * jnp.exp(ig - m_t)
            w_inter = jnp.exp(inter - m_t)
            cmat = c_ref[0, b, h]
            q8 = jnp.broadcast_to(q, (8, DK)).astype(BF16)
            cq = _dot_nt(q8, cmat.astype(BF16))[0:1, :]
            n_row = n_ref[0, b, h:h + 1, :]
            num = s * v + w_inter * cq
            den = s + w_inter * jnp.sum(n_row * q, axis=1, keepdims=True)
            hout = num / jnp.maximum(jnp.abs(den), jnp.exp(-m_t))
            hm_ref[b, :, h * DV:(h + 1) * DV] = (_sigmoid(o) * hout).astype(BF16)

            w_last = jnp.exp(ig - m_t)
            decay = jnp.exp(inter - m_t)
            vw = v * w_last
            vw_col = jnp.sum(jnp.where(eye, vw, 0.0), axis=1, keepdims=True)
            c_out[0, b, h] = decay * cmat + vw_col * k
            n_out[0, b, h:h + 1, :] = decay * n_row + w_last * k
            m_new_all = jnp.where(lane == h, m_t, m_new_all)
        m_out[b] = m_new_all


def _mlstm_sample(z3, bias_row, c0, n0, m0_pad, layer, bb):
    nb = z3.shape[0]

    def zspec(width, off):
        blk = off // width
        return pl.BlockSpec((bb, 1, width), lambda b: (b, 0, blk))

    c_in = pl.BlockSpec((1, bb, N_MH, DV, DK), lambda b: (layer, b, 0, 0, 0))
    n_in = pl.BlockSpec((1, bb, N_MH, DK), lambda b: (layer, b, 0, 0))
    c_blk = pl.BlockSpec((1, bb, N_MH, DV, DK), lambda b: (0, b, 0, 0, 0))
    n_blk = pl.BlockSpec((1, bb, N_MH, DK), lambda b: (0, b, 0, 0))

    return pl.pallas_call(
        _mlstm_sample_kernel,
        grid=(nb // bb,),
        in_specs=[
            zspec(QK_W, _Z_QM), zspec(QK_W, _Z_KM), zspec(V_W, _Z_VM), zspec(V_W, _Z_OM),
            zspec(LANES, _Z_IF),
            pl.BlockSpec((1, LANES), lambda b: (0, 0)),
            c_in,
            n_in,
            pl.BlockSpec((bb, 1, LANES), lambda b: (b, 0, 0)),
        ],
        out_specs=[
            pl.BlockSpec((bb, 1, V_W), lambda b: (b, 0, 0)),
            c_blk,
            n_blk,
            pl.BlockSpec((bb, 1, LANES), lambda b: (b, 0, 0)),
        ],
        out_shape=[
            jax.ShapeDtypeStruct((nb, 1, V_W), BF16),
            jax.ShapeDtypeStruct((1, nb, N_MH, DV, DK), F32),
            jax.ShapeDtypeStruct((1, nb, N_MH, DK), F32),
            jax.ShapeDtypeStruct((nb, 1, LANES), F32),
        ],
        compiler_params=_cparams(("parallel",)),
        name="mlstm_sample",
    )(z3, z3, z3, z3, z3, bias_row, c0, n0, m0_pad)


def _rope_tables(pos):
    half = ROT // 2
    inv = THETA ** (-jnp.arange(half, dtype=F32) * 2.0 / ROT)
    ang = pos.astype(F32)[:, None] * inv[None, :]
    cos = jnp.cos(ang)
    sin = jnp.sin(ang)
    t = pos.shape[0]
    cos_t = jnp.concatenate([cos, cos, jnp.ones((t, HD - ROT), F32)], axis=1)
    sin_a = jnp.concatenate([-sin, jnp.zeros((t, HD - half), F32)], axis=1)
    sin_b = jnp.concatenate([jnp.zeros((t, half), F32), sin, jnp.zeros((t, HD - ROT), F32)], axis=1)
    rep = LANES // HD
    return tuple(jnp.tile(a, (1, rep)) for a in (cos_t, sin_a, sin_b))


def _rope(x, cos_t, sin_a, sin_b):
    w = x.shape[1]
    rep = w // LANES
    half = ROT // 2
    if rep > 1:
        cos_t = jnp.concatenate([cos_t] * rep, axis=1)
        sin_a = jnp.concatenate([sin_a] * rep, axis=1)
        sin_b = jnp.concatenate([sin_b] * rep, axis=1)
    up = pltpu.roll(x, w - half, axis=1)
    down = pltpu.roll(x, half, axis=1)
    return x * cos_t + up * sin_a + down * sin_b


def _swa_prompt_kernel(sink_ref, q_ref, kc_ref, kp_ref, vc_ref, vp_ref, ha_ref):
    i = pl.program_id(1)
    w = WINDOW
    kk = jnp.concatenate([kp_ref[...], kc_ref[...]], axis=0)
    vv = jnp.concatenate([vp_ref[...], vc_ref[...]], axis=0)
    vv_t = vv.T.astype(BF16)
    s_idx = lax.broadcasted_iota(jnp.int32, (2 * w, w), 0)
    a_idx = lax.broadcasted_iota(jnp.int32, (2 * w, w), 1)
    valid = (s_idx >= a_idx) & (s_idx <= a_idx + w) & ((i > 0) | (s_idx >= w))
    lane = lax.broadcasted_iota(jnp.int32, (2 * w, LANES), 1)
    pair_w = 2 * HD
    for half in range(N_KV // 2):
        kh = kk[:, half * pair_w:(half + 1) * pair_w]
        kh_sw = pltpu.roll(kh, HD, axis=1)
        for gg in range(2):
            g = 2 * half + gg
            k_lo = jnp.where(lane < HD, kh if gg == 0 else kh_sw, 0.0).astype(BF16)
            k_hi = jnp.where(lane >= HD, kh_sw if gg == 0 else kh, 0.0).astype(BF16)
            vg_t = vv_t[g * HD:(g + 1) * HD, :]
            sts = []
            for a in range(GROUP // 2):
                col = (g * GROUP + 2 * a) * HD
                qpair = q_ref[:, col:col + pair_w].astype(BF16)
                sts += [_dot_nt(k_lo, qpair), _dot_nt(k_hi, qpair)]
            es, rs = [], []
            for hh in range(GROUP):
                sink = sink_ref[g * GROUP + hh]
                st = jnp.where(valid, sts[hh], -jnp.inf)
                m = jnp.maximum(jnp.max(st, axis=0, keepdims=True), sink)
                e = jnp.exp(st - m)
                es.append(e.astype(BF16))
                rs.append(1.0 / (jnp.sum(e, axis=0, keepdims=True) + jnp.exp(sink - m)))
            ots = [_dot(vg_t, es[hh]) * rs[hh] for hh in range(GROUP)]
            for a in range(GROUP // 2):
                col = (g * GROUP + 2 * a) * HD
                ha_ref[:, col:col + pair_w] = jnp.concatenate(ots[2 * a:2 * a + 2], axis=0).T.astype(BF16)


def _swa_prompt(z, sinks3, batch, seq):
    rows = z.shape[0]
    nb = seq // WINDOW
    w = WINDOW

    def cur(width, off):
        blk = off // width
        return pl.BlockSpec((w, width), lambda b, i: (b * nb + i, blk))

    def prev(width, off):
        blk = off // width
        return pl.BlockSpec((w, width), lambda b, i: (b * nb + jnp.maximum(i - 1, 0), blk))

    return pl.pallas_call(
        _swa_prompt_kernel,
        grid=(batch, nb),
        in_specs=[
            pl.BlockSpec(memory_space=pltpu.SMEM),
            cur(ATT_Q, _Z_QA), cur(ATT_KV, _Z_KA), prev(ATT_KV, _Z_KA),
            cur(ATT_KV, _Z_VA), prev(ATT_KV, _Z_VA),
        ],
        out_specs=pl.BlockSpec((w, ATT_Q), lambda b, i: (b * nb + i, 0)),
        out_shape=jax.ShapeDtypeStruct((rows, ATT_Q), BF16),
        compiler_params=_cparams(("parallel", "parallel")),
        name="swa_prompt",
    )(sinks3, z, z, z, z, z)


def _swa_sample_kernel(q_ref, kn_ref, vn_ref, kp_ref, vp_ref, sink_ref,
                       o_ref, ko_ref, vo_ref, *, bb):
    w = WINDOW
    hrow = lax.broadcasted_iota(jnp.int32, (N_Q, ATT_KV), 0) // GROUP
    glane = lax.broadcasted_iota(jnp.int32, (N_Q, ATT_KV), 1) // HD
    own = hrow == glane
    srow = lax.broadcasted_iota(jnp.int32, (w, ATT_KV), 0)
    sink = sink_ref[...]
    for b in range(bb):
        q = q_ref[b]
        qm = jnp.where(own, jnp.concatenate([q] * N_KV, axis=1), 0.0)
        k_past = kp_ref[b]
        v_past = vp_ref[b]
        k_new = kn_ref[b]
        v_new = vn_ref[b]
        s_past = _dot_nt(qm.astype(BF16), k_past.astype(BF16))
        s_new = jnp.sum(qm * k_new, axis=1, keepdims=True)
        m = jnp.maximum(jnp.maximum(jnp.max(s_past, axis=1, keepdims=True), s_new), sink)
        e_past = jnp.exp(s_past - m)
        e_new = jnp.exp(s_new - m)
        denom = jnp.sum(e_past, axis=1, keepdims=True) + e_new + jnp.exp(sink - m)
        r = 1.0 / denom
        pv = _dot((e_past * r).astype(BF16), v_past.astype(BF16)) + (e_new * r) * v_new
        pv = jnp.where(own, pv, 0.0)
        o = pv[:, 0:HD]
        for g in range(1, N_KV):
            o = o + pv[:, g * HD:(g + 1) * HD]
        o_ref[b] = o
        ko_ref[b] = jnp.where(srow == w - 1, k_new, pltpu.roll(k_past, w - 1, axis=0))
        vo_ref[b] = jnp.where(srow == w - 1, v_new, pltpu.roll(v_past, w - 1, axis=0))


def _swa_sample(q3, k_new3, v_new3, k_past, v_past, sink_col, bb):
    nb = q3.shape[0]
    w = WINDOW
    return pl.pallas_call(
        functools.partial(_swa_sample_kernel, bb=bb),
        grid=(nb // bb,),
        in_specs=[
            pl.BlockSpec((bb, N_Q, HD), lambda i: (i, 0, 0)),
            pl.BlockSpec((bb, 1, ATT_KV), lambda i: (i, 0, _Z_KA // ATT_KV)),
            pl.BlockSpec((bb, 1, ATT_KV), lambda i: (i, 0, _Z_VA // ATT_KV)),
            pl.BlockSpec((bb, w, ATT_KV), lambda i: (i, 0, 0)),
            pl.BlockSpec((bb, w, ATT_KV), lambda i: (i, 0, 0)),
            pl.BlockSpec((N_Q, 1), lambda i: (0, 0)),
        ],
        out_specs=[
            pl.BlockSpec((bb, N_Q, HD), lambda i: (i, 0, 0)),
            pl.BlockSpec((bb, w, ATT_KV), lambda i: (i, 0, 0)),
            pl.BlockSpec((bb, w, ATT_KV), lambda i: (i, 0, 0)),
        ],
        out_shape=[
            jax.ShapeDtypeStruct((nb, N_Q, HD), F32),
            jax.ShapeDtypeStruct((nb, w, ATT_KV), F32),
            jax.ShapeDtypeStruct((nb, w, ATT_KV), F32),
        ],
        compiler_params=_cparams(("parallel",)),
        name="swa_sample",
    )(q3, k_new3, v_new3, k_past, v_past, sink_col)


def _merge1_kernel(hm_ref, ha_ref, ga_ref, gb_ref, wa_ref, wb_ref, t_ref):
    a = _dot(hm_ref[...], wa_ref[...])
    b = _dot(ha_ref[...], wb_ref[...])
    t_ref[...] = (_sigmoid(ga_ref[...]) * a + _sigmoid(gb_ref[...]) * b).astype(BF16)


def _merge1(hm, ha, z, wa, wb, tm, tn):
    rows, d = hm.shape
    n = wa.shape[1]
    return pl.pallas_call(
        _merge1_kernel,
        grid=(rows // tm, n // tn),
        in_specs=[
            pl.BlockSpec((tm, d), lambda i, j: (i, 0)),
            pl.BlockSpec((tm, d), lambda i, j: (i, 0)),
            pl.BlockSpec((tm, tn), lambda i, j: (i, _Z_GA // tn + j)),
            pl.BlockSpec((tm, tn), lambda i, j: (i, _Z_GB // tn + j)),
            pl.BlockSpec((d, tn), lambda i, j: (0, j)),
            pl.BlockSpec((d, tn), lambda i, j: (0, j)),
        ],
        out_specs=pl.BlockSpec((tm, tn), lambda i, j: (i, j)),
        out_shape=jax.ShapeDtypeStruct((rows, n), BF16),
        compiler_params=_cparams(("parallel", "arbitrary")),
        name="merge1",
    )(hm, ha, z, z, wa, wb)


def _merge2_kernel(t_ref, w_ref, x_ref, g1_ref, g2_ref, x1_ref, h2_ref):
    mix = _dot(t_ref[...], w_ref[...])
    x1 = x_ref[...] + _rms(mix, g1_ref[...])
    x1_ref[...] = x1
    h2_ref[...] = _rms(x1, g2_ref[...]).astype(BF16)


def _merge2(t, w_out, x2d, g1, g2, tm):
    rows, d = x2d.shape
    row = lambda i: (i, 0)
    const = lambda i: (0, 0)
    return pl.pallas_call(
        _merge2_kernel,
        grid=(rows // tm,),
        in_specs=[
            pl.BlockSpec((tm, d), row),
            pl.BlockSpec((d, d), const, pipeline_mode=pl.Buffered(1)),
            pl.BlockSpec((tm, d), row),
            pl.BlockSpec((1, d), const),
            pl.BlockSpec((1, d), const),
        ],
        out_specs=[pl.BlockSpec((tm, d), row), pl.BlockSpec((tm, d), row)],
        out_shape=[jax.ShapeDtypeStruct((rows, d), F32), jax.ShapeDtypeStruct((rows, d), BF16)],
        compiler_params=_cparams(("parallel",)),
        name="merge2",
    )(t, w_out, x2d, g1.reshape(1, d), g2.reshape(1, d))


def _gelu_tanh(x):
    return 0.5 * x * (1.0 + jnp.tanh(0.7978845608028654 * (x + 0.044715 * (x * x * x))))


def _conv_taps(u, u1, u2, cw_ref, cb_ref, cs=slice(None)):
    return cb_ref[:, cs] + u2 * cw_ref[0:1, cs] + u1 * cw_ref[1:2, cs] + u * cw_ref[2:3, cs]


def _ffn_tail(j, nf, y, wd_ref, x1_ref, g_ref, y_ref, acc_ref):
    @pl.when(j == 0)
    def _():
        acc_ref[...] = _dot(y, wd_ref[...])

    @pl.when(j > 0)
    def _():
        acc_ref[...] += _dot(y, wd_ref[...])

    @pl.when(j == nf - 1)
    def _():
        y_ref[...] = x1_ref[...] + _rms(acc_ref[...], g_ref[...])


def _ffn_prompt_kernel(h_hbm, wg_ref, wv_ref, cwg_ref, cwv_ref, cbg_ref, cbv_ref, wd_ref,
                       x1_hbm, g_ref, y_ref, sg_ref, sv_ref, acc_ref, carry_ref, x1_buf, x1_sem,
                       hbuf, hsem, *, tiles_per_seq):
    i = pl.program_id(0)
    j = pl.program_id(1)
    nf = pl.num_programs(1)
    tm = x1_buf.shape[0]

    def h_copy(tile, slot):
        r0 = pl.multiple_of(tile * tm, tm)
        return pltpu.make_async_copy(h_hbm.at[pl.ds(r0, tm), :], hbuf.at[slot], hsem.at[slot])

    @pl.when((i == 0) & (j == 0))
    def _():
        h_copy(0, 0).start()

    @pl.when((j == 1) & (i + 1 < pl.num_programs(0)))
    def _():
        h_copy(i + 1, (i + 1) % 2).start()

    @pl.when(j == 0)
    def _():
        h_copy(i, i % 2).wait()

    def x1_copy():
        r0 = pl.multiple_of(i * tm, tm)
        return pltpu.make_async_copy(x1_hbm.at[pl.ds(r0, tm), :], x1_buf, x1_sem.at[0])

    @pl.when(j == nf - 2)
    def _():
        x1_copy().start()

    @pl.when(j == nf - 1)
    def _():
        x1_copy().wait()

    @pl.when((i == 0) & (j == 0))
    def _():
        carry_ref[...] = jnp.zeros_like(carry_ref)

    seq_start = (i % tiles_per_seq) == 0
    h = hbuf[i % 2]

    def branch(w_ref, cw_ref, cb_ref, slot, state_ref):
        u = _dot(h, w_ref[...])
        prev = jnp.where(seq_start, 0.0, carry_ref[slot])
        uu = jnp.concatenate([prev, u], axis=0)
        u1 = uu[7:7 + tm, :]
        u2 = uu[6:6 + tm, :]
        tail = u[tm - 8:tm, :]
        carry_ref[slot] = tail
        state_ref[0] = tail
        return _conv_taps(u, u1, u2, cw_ref, cb_ref)

    cg = branch(wg_ref, cwg_ref, cbg_ref, 2 * j, sg_ref)
    cv = branch(wv_ref, cwv_ref, cbv_ref, 2 * j + 1, sv_ref)
    y = (_gelu_tanh(cg) * cv).astype(BF16)
    _ffn_tail(j, nf, y, wd_ref, x1_buf, g_ref, y_ref, acc_ref)


def _ffn_sample_kernel(h_ref, wg_ref, wv_ref, cwg_ref, cwv_ref, cbg_ref, cbv_ref, wd_ref,
                       x1_ref, g_ref, p2g_ref, p2v_ref, p1g_ref, p1v_ref,
                       y_ref, ug_ref, uv_ref, acc_ref):
    j = pl.program_id(1)
    nf = pl.num_programs(1)
    h = h_ref[...]
    ug = _dot(h, wg_ref[...])
    uv = _dot(h, wv_ref[...])
    ug_ref[...] = ug
    uv_ref[...] = uv
    cg = _conv_taps(ug, p1g_ref[...], p2g_ref[...], cwg_ref, cbg_ref)
    cv = _conv_taps(uv, p1v_ref[...], p2v_ref[...], cwv_ref, cbv_ref)
    y = (_gelu_tanh(cg) * cv).astype(BF16)
    _ffn_tail(j, nf, y, wd_ref, x1_ref, g_ref, y_ref, acc_ref)


def _ffn(h2, x1, w_up, w_conv, b_conv2d, w_down, g, tm, tf, seq=None, past=None):
    rows, d = h2.shape
    f = w_down.shape[0]
    nf = f // tf
    row = lambda i, j: (i, 0)
    gcol = lambda i, j: (0, j)
    vcol = lambda i, j: (0, nf + j)
    in_specs = [
        pl.BlockSpec((tm, d), row),
        pl.BlockSpec((d, tf), gcol), pl.BlockSpec((d, tf), vcol),
        pl.BlockSpec((3, tf), gcol), pl.BlockSpec((3, tf), vcol),
        pl.BlockSpec((1, tf), gcol), pl.BlockSpec((1, tf), vcol),
        pl.BlockSpec((tf, d), lambda i, j: (j, 0)),
        pl.BlockSpec((tm, d), row),
        pl.BlockSpec((1, d), lambda i, j: (0, 0)),
    ]
    args = [h2, w_up, w_up, w_conv, w_conv, b_conv2d, b_conv2d, w_down, x1, g.reshape(1, d)]
    scratch = [pltpu.VMEM((tm, d), F32)]
    if past is None:
        tiles_per_seq = seq // tm
        kern = functools.partial(_ffn_prompt_kernel, tiles_per_seq=tiles_per_seq)
        state_spec = pl.BlockSpec((1, 8, tf), lambda i, j: (i, 0, j))
        out_specs = [pl.BlockSpec((tm, d), row), state_spec, state_spec]
        out_shape = [jax.ShapeDtypeStruct((rows, d), F32),
                     jax.ShapeDtypeStruct((rows // tm, 8, f), F32),
                     jax.ShapeDtypeStruct((rows // tm, 8, f), F32)]
        scratch.append(pltpu.VMEM((2 * nf, 8, tf), F32))
        assert nf >= 2
        in_specs[8] = pl.BlockSpec(memory_space=pl.ANY)
        scratch += [pltpu.VMEM((tm, d), F32), pltpu.SemaphoreType.DMA((1,))]
        in_specs[0] = pl.BlockSpec(memory_space=pl.ANY)
        scratch += [pltpu.VMEM((2, tm, d), BF16), pltpu.SemaphoreType.DMA((2,))]
        name = "ffn_prompt"
    else:
        p2, p1 = past
        kern = _ffn_sample_kernel
        ucol = pl.BlockSpec((tm, tf), lambda i, j: (i, j))
        in_specs += [
            pl.BlockSpec((tm, tf), lambda i, j: (i, j)), pl.BlockSpec((tm, tf), lambda i, j: (i, nf + j)),
            pl.BlockSpec((tm, tf), lambda i, j: (i, j)), pl.BlockSpec((tm, tf), lambda i, j: (i, nf + j)),
        ]
        args += [p2, p2, p1, p1]
        out_specs = [pl.BlockSpec((tm, d), row), ucol, ucol]
        out_shape = [jax.ShapeDtypeStruct((rows, d), F32),
                     jax.ShapeDtypeStruct((rows, f), F32),
                     jax.ShapeDtypeStruct((rows, f), F32)]
        name = "ffn_sample"
    return pl.pallas_call(
        kern,
        grid=(rows // tm, nf),
        in_specs=in_specs,
        out_specs=out_specs,
        out_shape=out_shape,
        scratch_shapes=scratch,
        compiler_params=_cparams(("arbitrary", "arbitrary")),
        name=name,
    )(*args)


def _regroup_w_in(w_in):
    d = w_in.shape[0]
    o_if = QK_W * 2 + V_W * 2
    o_qa = o_if + 2 * N_MH
    o_ka = o_qa + ATT_Q
    o_ga = o_ka + 2 * ATT_KV
    assert o_if == _Z_QA
    parts = [
        w_in[:, o_qa:o_ka],
        w_in[:, o_ga:],
        w_in[:, o_ka:o_ga],
        w_in[:, o_if:o_qa],
    ]
    used = o_if + sum(p.shape[1] for p in parts)
    parts.append(jnp.zeros((d, _Z_W - used), w_in.dtype))
    return jnp.concatenate(parts, axis=1)


def _pick(rows, pref):
    t = min(rows, pref)
    while rows % t:
        t //= 2
    return t


def kernel(x_prompt, x_sample, state_mlstm_c, state_mlstm_n, state_mlstm_m, cache_swa_k, cache_swa_v,
           state_ffn_conv, g_pre_mix, w_in, b_if, attn_sinks, w_branch_a, w_branch_b, w_out, g_post_mix,
           g_pre_ffn, w_up, w_conv, b_conv, w_down, g_post_ffn):
    bp, tp, d = x_prompt.shape
    bs, ts, _ = x_sample.shape
    depth = w_in.shape[0]
    assert depth == 1 and ts == 1 and d == D_MODEL
    f = w_down.shape[1]
    tf = 512
    assert f % tf == 0

    l = 0
    w_lo = w_in[l].astype(BF16)
    w_hi = _regroup_w_in(w_lo)
    wa = w_branch_a[l].astype(BF16)
    wb = w_branch_b[l].astype(BF16)
    wo = w_out[l].astype(BF16)
    wu = w_up[l].astype(BF16)
    wd = w_down[l].astype(BF16)
    bias_row = jnp.concatenate([b_if[l].astype(F32), jnp.zeros((LANES - 2 * N_MH,), F32)]).reshape(1, LANES)
    sinks = attn_sinks[l].astype(F32)
    b_conv2d = b_conv[l].reshape(1, 2 * f)

    xp = x_prompt.reshape(bp * tp, d)
    zp = _project(xp, g_pre_mix[l], w_lo, w_hi, _rope_tables(jnp.arange(tp)), _pick(tp, 1024))
    hm_p3, c_p, n_p, m_p = _mlstm_prompt(zp, bias_row, bp, tp, _pick(bp, 4))
    hm_p = hm_p3.reshape(bp * tp, V_W)
    ha_p = _swa_prompt(zp, sinks, bp, tp)
    t_p = _merge1(hm_p, ha_p, zp, wa, wb, _pick(bp * tp, 1024), 512)
    x1_p, h2_p = _merge2(t_p, wo, xp, g_post_mix[l], g_pre_ffn[l], _pick(bp * tp, 512))
    tm_ffn = _pick(tp, 512)
    y_p, sg_p, sv_p = _ffn(h2_p, x1_p, wu, w_conv[l], b_conv2d, wd, g_post_ffn[l],
                           tm_ffn, tf, seq=tp)
    seq_end = slice(tp // tm_ffn - 1, None, tp // tm_ffn)

    y_prompt = y_p.reshape(bp, tp, d)
    c_prompt = c_p[None]
    n_prompt = n_p[None]
    m_prompt = m_p[:, :N_MH, 0][None]
    kv_tail = zp.reshape(bp, tp, _Z_W)[:, tp - WINDOW:, _Z_KA:_Z_KA + 2 * ATT_KV]
    k_prompt = kv_tail[..., :ATT_KV].reshape(bp, WINDOW, N_KV, HD)[None]
    v_prompt = kv_tail[..., ATT_KV:].reshape(bp, WINDOW, N_KV, HD)[None]
    conv_prompt = jnp.concatenate([sg_p[seq_end, 6:8, :], sv_p[seq_end, 6:8, :]], axis=-1)[None]

    xs = x_sample.reshape(bs, d)
    tabs_s = tuple(jnp.broadcast_to(a, (bs, LANES)) for a in _rope_tables(PAST_LEN + jnp.arange(ts)))
    zs = _project(xs, g_pre_mix[l], w_lo, w_hi, tabs_s, bs)
    zs3 = zs.reshape(bs, 1, _Z_W)
    m0_pad = jnp.pad(state_mlstm_m[l].astype(F32), ((0, 0), (0, LANES - N_MH))).reshape(bs, 1, LANES)
    hm_s3, c_sample, n_sample, m_s3 = _mlstm_sample(zs3, bias_row, state_mlstm_c, state_mlstm_n, m0_pad, l,
                                                          _pick(bs, 4))
    o_s3, k_s, v_s = _swa_sample(
        zs[:, _Z_QA:_Z_QA + ATT_Q].reshape(bs, N_Q, HD), zs3, zs3,
        cache_swa_k[l].reshape(bs, WINDOW, ATT_KV), cache_swa_v[l].reshape(bs, WINDOW, ATT_KV),
        sinks.reshape(N_Q, 1), _pick(bs, 8))
    ha_s = o_s3.reshape(bs, ATT_Q).astype(BF16)
    t_s = _merge1(hm_s3.reshape(bs, V_W), ha_s, zs, wa, wb, bs, 512)
    x1_s, h2_s = _merge2(t_s, wo, xs, g_post_mix[l], g_pre_ffn[l], bs)
    past = state_ffn_conv[l].astype(F32)
    y_s, ug_s, uv_s = _ffn(h2_s, x1_s, wu, w_conv[l], b_conv2d, wd, g_post_ffn[l],
                           bs, tf, past=(past[:, 0, :], past[:, 1, :]))

    y_sample = y_s.reshape(bs, ts, d)
    m_sample = m_s3[:, 0, :N_MH][None]
    k_sample = k_s.reshape(bs, WINDOW, N_KV, HD)[None]
    v_sample = v_s.reshape(bs, WINDOW, N_KV, HD)[None]
    conv_sample = jnp.stack([past[:, 1, :], jnp.concatenate([ug_s, uv_s], axis=-1)], axis=1)[None]

    return (y_prompt, y_sample,
            c_prompt, n_prompt, m_prompt, k_prompt, v_prompt, conv_prompt,
            c_sample, n_sample, m_sample, k_sample, v_sample, conv_sample)
```
